```python
import jax
import jax.numpy as jnp
from jax import lax
import numpy as np

D_MODEL = 4096
BATCH = 1
SEQ = 8192
DEPTH = 1

HEAD_DIM = 128
MIX_WIDTH = D_MODEL
FOX_WIDTH = MIX_WIDTH // 2
HGRN_WIDTH = MIX_WIDTH - FOX_WIDTH
FOX_HEADS = FOX_WIDTH // HEAD_DIM
HGRN_HEADS = HGRN_WIDTH // HEAD_DIM
FOX_BLOCK = 128
HGRN_CHUNK = 64
IN_COLS = 3 * FOX_WIDTH + FOX_HEADS + 4 * HGRN_WIDTH
PEER_HEADS = 8
PEER_KEYS = 128
PEER_EXPERTS = PEER_KEYS * PEER_KEYS
PEER_TOPK = 16
PEER_QDIM = 256
PEER_HALF = PEER_QDIM // 2
PEER_TOKEN_BLOCK = 64
N_MOD = 6
EPS = 1e-6

kernel_name = "hymba_fox_hgrn2_peer_adaln"


def rms_norm(x, gain):
    xf = x.astype(jnp.float32)
    y = xf * lax.rsqrt(jnp.mean(xf * xf, axis=-1, keepdims=True) + EPS)
    return (y * gain.astype(jnp.float32)).astype(x.dtype)


def fox_attention(q, k, v, log_f):
    B, S, H, Dh = q.shape
    scale = Dh ** -0.5
    cum_f = jnp.cumsum(log_f, axis=1)
    qh = jnp.transpose(q, (0, 2, 1, 3))
    kh = jnp.transpose(k, (0, 2, 1, 3))
    vh = jnp.transpose(v, (0, 2, 1, 3))
    fh = jnp.transpose(cum_f, (0, 2, 1))
    nb = S // FOX_BLOCK
    q_blocks = jnp.transpose(qh.reshape(B, H, nb, FOX_BLOCK, Dh), (2, 0, 1, 3, 4))
    f_blocks = jnp.transpose(fh.reshape(B, H, nb, FOX_BLOCK), (2, 0, 1, 3))
    key_pos = jnp.arange(S)

    def one_block(args):
        qb, fb, bi = args
        s = jnp.einsum('bhqd,bhkd->bhqk', qb, kh, preferred_element_type=jnp.float32) * scale
        s = s + fb[..., None] - fh[:, :, None, :]
        q_pos = bi * FOX_BLOCK + jnp.arange(FOX_BLOCK)
        causal = key_pos[None, :] <= q_pos[:, None]
        s = jnp.where(causal, s, -jnp.inf)
        p = jax.nn.softmax(s, axis=-1)
        return jnp.einsum('bhqk,bhkd->bhqd', p.astype(vh.dtype), vh)

    out = lax.map(one_block, (q_blocks, f_blocks, jnp.arange(nb)))
    return jnp.transpose(out, (1, 0, 3, 2, 4)).reshape(B, S, H, Dh)


def hgrn2_recurrence(q, k, v, log_f):
    B, S, H, Dk = q.shape
    Dv = v.shape[-1]
    C = HGRN_CHUNK
    nc = S // C

    def to_chunks(t):
        return jnp.transpose(t.reshape(B, nc, C, H, t.shape[-1]), (1, 0, 3, 2, 4))

    qc, kc, vc, gc = to_chunks(q), to_chunks(k), to_chunks(v), to_chunks(log_f)
    causal = jnp.tril(jnp.ones((C, C), dtype=bool))[:, :, None]

    def step(state, inp):
        qb, kb, vb, gb = inp
        b = jnp.cumsum(gb, axis=2)
        diff = b[:, :, :, None, :] - b[:, :, None, :, :]
        decay = jnp.exp(jnp.where(causal, diff, -jnp.inf))
        scores = jnp.einsum('bhtk,bhsk,bhtsk->bhts', qb, kb, decay)
        o_intra = jnp.einsum('bhts,bhsv->bhtv', scores, vb)
        o_inter = jnp.einsum('bhtk,bhkv->bhtv', qb * jnp.exp(b), state)
        b_last = b[:, :, -1:, :]
        k_dec = kb * jnp.exp(b_last - b)
        new_state = state * jnp.exp(b_last)[:, :, 0, :, None] + jnp.einsum('bhsk,bhsv->bhkv', k_dec, vb)
        return new_state, o_intra + o_inter

    state0 = jnp.zeros((B, H, Dk, Dv), jnp.float32)
    _, out = lax.scan(step, state0, (qc, kc, vc, gc))
    return jnp.transpose(out, (1, 0, 3, 2, 4)).reshape(B, S, H, Dv)


def hybrid_mixer(h, w_in, fox_f_bias, fox_q_gain, fox_k_gain, lower_bound, hgrn_out_gain, w_out):
    B, S, _ = h.shape
    proj = h @ w_in
    o1 = FOX_WIDTH
    o2 = 2 * FOX_WIDTH
    o3 = 3 * FOX_WIDTH
    o4 = o3 + FOX_HEADS
    o5 = o4 + HGRN_WIDTH
    o6 = o5 + HGRN_WIDTH
    o7 = o6 + HGRN_WIDTH
    fq = proj[..., :o1].reshape(B, S, FOX_HEADS, HEAD_DIM)
    fk = proj[..., o1:o2].reshape(B, S, FOX_HEADS, HEAD_DIM)
    fv = proj[..., o2:o3].reshape(B, S, FOX_HEADS, HEAD_DIM)
    f_logit = proj[..., o3:o4]
    hq = proj[..., o4:o5].reshape(B, S, HGRN_HEADS, HEAD_DIM)
    hf = proj[..., o5:o6].reshape(B, S, HGRN_HEADS, HEAD_DIM)
    hi = proj[..., o6:o7].reshape(B, S, HGRN_HEADS, HEAD_DIM)
    hg = proj[..., o7:].reshape(B, S, HGRN_HEADS, HEAD_DIM)

    fq = rms_norm(fq, fox_q_gain)
    fk = rms_norm(fk, fox_k_gain)
    fox_log_f = jax.nn.log_sigmoid(f_logit.astype(jnp.float32) + fox_f_bias.astype(jnp.float32))
    fox_out = fox_attention(fq, fk, fv, fox_log_f)

    lb = lower_bound.reshape(HGRN_HEADS, HEAD_DIM)
    forget = lb + (1.0 - lb) * jax.nn.sigmoid(hf.astype(jnp.float32))
    h_log_f = jnp.log(forget)
    h_key = 1.0 - forget
    h_query = jax.nn.silu(hq.astype(jnp.float32)) * (HEAD_DIM ** -0.5)
    h_rec = hgrn2_recurrence(h_query, h_key, hi.astype(jnp.float32), h_log_f)
    hgrn_out = rms_norm(h_rec, hgrn_out_gain) * jax.nn.silu(hg.astype(jnp.float32))

    merged = jnp.concatenate([fox_out.reshape(B, S, FOX_WIDTH).astype(h.dtype),
                              hgrn_out.reshape(B, S, HGRN_WIDTH).astype(h.dtype)], axis=-1)
    return merged @ w_out


def peer_ffn(h, w_query, sub_keys, u_experts, v_experts):
    B, S, D = h.shape
    T = B * S
    ht = h.reshape(T, D)
    q = (ht @ w_query).reshape(T, PEER_HEADS, 2, PEER_HALF)
    scores = jnp.einsum('thpd,hpnd->thpn', q, sub_keys, preferred_element_type=jnp.float32)
    top_s, top_i = lax.top_k(scores, PEER_TOPK)
    cand_s = top_s[:, :, 0, :, None] + top_s[:, :, 1, None, :]
    cand_i = top_i[:, :, 0, :, None] * PEER_KEYS + top_i[:, :, 1, None, :]
    cand_s = cand_s.reshape(T, PEER_HEADS, PEER_TOPK * PEER_TOPK)
    cand_i = cand_i.reshape(T, PEER_HEADS, PEER_TOPK * PEER_TOPK)
    best_s, best_pos = lax.top_k(cand_s, PEER_TOPK)
    expert_idx = jnp.take_along_axis(cand_i, best_pos, axis=-1)
    gates = jax.nn.softmax(best_s.astype(jnp.float32), axis=-1)

    nb = T // PEER_TOKEN_BLOCK
    xb_all = ht.reshape(nb, PEER_TOKEN_BLOCK, D)
    idx_all = expert_idx.reshape(nb, PEER_TOKEN_BLOCK, PEER_HEADS, PEER_TOPK)
    g_all = gates.reshape(nb, PEER_TOKEN_BLOCK, PEER_HEADS, PEER_TOPK)

    def one_block(args):
        xb, idx, g = args
        u = u_experts[idx]
        act = jax.nn.gelu(jnp.einsum('td,thkd->thk', xb, u, preferred_element_type=jnp.float32), approximate=False)
        v = v_experts[idx]
        return jnp.einsum('thk,thkd->td', (g * act).astype(v.dtype), v)

    out = lax.map(one_block, (xb_all, idx_all, g_all))
    return out.reshape(B, S, D)


def setup_inputs(seed: int = 0) -> dict:
    key = jax.random.key(seed)
    ks = jax.random.split(key, 17)
    D = D_MODEL
    L = DEPTH
    f32 = jnp.float32

    def nrm(k, shape, scale):
        return jax.random.normal(k, shape, f32) * scale

    return {
        "x": nrm(ks[0], (BATCH, SEQ, D), 1.0),
        "c": nrm(ks[1], (BATCH, D), 1.0),
        "ada_w": nrm(ks[2], (L, D, N_MOD * D), 0.5 * D ** -0.5),
        "ada_b": nrm(ks[3], (L, N_MOD * D), 0.02),
        "norm1_gain": 1.0 + nrm(ks[4], (L, D), 0.02),
        "norm2_gain": 1.0 + nrm(ks[5], (L, D), 0.02),
        "w_in": nrm(ks[6], (L, D, IN_COLS), D ** -0.5),
        "fox_f_bias": 2.0 + nrm(ks[7], (L, FOX_HEADS), 0.5),
        "fox_q_gain": 1.0 + nrm(ks[8], (L, FOX_HEADS, HEAD_DIM), 0.02),
        "fox_k_gain": 1.0 + nrm(ks[9], (L, FOX_HEADS, HEAD_DIM), 0.02),
        "hgrn_lower_bounds": nrm(ks[10], (L + 1, HGRN_WIDTH), 0.1).at[-1].add(1.0),
        "hgrn_out_gain": 1.0 + nrm(ks[11], (L, HGRN_HEADS, HEAD_DIM), 0.02),
        "w_out": nrm(ks[12], (L, MIX_WIDTH, D), MIX_WIDTH ** -0.5),
        "peer_w_query": nrm(ks[13], (L, D, PEER_HEADS * PEER_QDIM), D ** -0.5),
        "peer_sub_keys": nrm(ks[14], (L, PEER_HEADS, 2, PEER_KEYS, PEER_HALF), PEER_HALF ** -0.5),
        "peer_u": nrm(ks[15], (L, PEER_EXPERTS, D), D ** -0.5),
        "peer_v": nrm(ks[16], (L, PEER_EXPERTS, D), PEER_HEADS ** -0.5),
    }


def reference(x, c, ada_w, ada_b, norm1_gain, norm2_gain, w_in, fox_f_bias, fox_q_gain, fox_k_gain,
              hgrn_lower_bounds, hgrn_out_gain, w_out, peer_w_query, peer_sub_keys, peer_u, peer_v):
    lower_bounds = jnp.cumsum(jax.nn.softmax(hgrn_lower_bounds.astype(jnp.float32), axis=0), axis=0)
    c_act = jax.nn.silu(c)
    for layer in range(DEPTH):
        mod = c_act @ ada_w[layer] + ada_b[layer]
        shift1, scale1, gate1, shift2, scale2, gate2 = jnp.split(mod[:, None, :], N_MOD, axis=-1)
        h = rms_norm(x, norm1_gain[layer]) * (1.0 + scale1) + shift1
        mix = hybrid_mixer(h, w_in[layer], fox_f_bias[layer], fox_q_gain[layer], fox_k_gain[layer],
                           lower_bounds[layer], hgrn_out_gain[layer], w_out[layer])
        x = x + gate1 * mix
        h = rms_norm(x, norm2_gain[layer]) * (1.0 + scale2) + shift2
        x = x + gate2 * peer_ffn(h, peer_w_query[layer], peer_sub_keys[layer], peer_u[layer], peer_v[layer])
    return x
```

```python
import functools

import jax
import jax.numpy as jnp
from jax import lax
from jax.experimental import pallas as pl
from jax.experimental.pallas import tpu as pltpu

F32 = jnp.float32
BF16 = jnp.bfloat16

HEAD_DIM = 128
PEER_KEYS = 128
PEER_TOPK = 16
HGRN_CHUNK = 128
EPS = 1e-6
MASKED = -1e30
SQRT_HALF = 0.7071067811865476
V7X_VMEM_LIMIT_BYTES = 52 * 1024 * 1024

_NT = (((1,), (1,)), ((), ()))


def _params(semantics):
    return pltpu.CompilerParams(dimension_semantics=semantics, vmem_limit_bytes=V7X_VMEM_LIMIT_BYTES)


def _tile(n, t):
    if n <= t:
        return n
    t -= t % HEAD_DIM
    while n % t:
        t -= HEAD_DIM
    assert t > 0, n
    return t


def _split3(x):
    hi = x.astype(BF16)
    r1 = x - hi.astype(F32)
    mid = r1.astype(BF16)
    lo = (r1 - mid.astype(F32)).astype(BF16)
    return hi, mid, lo


def _ada_body(c_ref, w_ref, b_ref, o_ref, acc_ref):
    k = pl.program_id(1)

    @pl.when(k == 0)
    def _():
        acc_ref[...] = jnp.zeros_like(acc_ref)

    c = c_ref[...]
    c_act = c * jax.nn.sigmoid(c)
    acc_ref[...] += jnp.sum(w_ref[...] * c_act, axis=0, keepdims=True)

    @pl.when(k == pl.num_programs(1) - 1)
    def _():
        o_ref[...] = acc_ref[...] + b_ref[...]


def _ada_mod(c_col, w, b_row):
    d, n = w.shape
    tk, tn = _tile(d, 512), _tile(n, 2048)
    return pl.pallas_call(
        _ada_body,
        grid=(n // tn, d // tk),
        in_specs=[pl.BlockSpec((tk, 1), lambda j, k: (k, 0)),
                  pl.BlockSpec((tk, tn), lambda j, k: (k, j)),
                  pl.BlockSpec((1, tn), lambda j, k: (0, j))],
        out_specs=pl.BlockSpec((1, tn), lambda j, k: (0, j)),
        out_shape=jax.ShapeDtypeStruct((1, n), F32),
        scratch_shapes=[pltpu.VMEM((1, tn), F32)],
        compiler_params=_params(("parallel", "arbitrary")),
        name="ada_mod",
    )(c_col, w, b_row)


def _norm_mod_body(x_ref, g_ref, sc_ref, sh_ref, o_ref):
    x = x_ref[...]
    y = x * lax.rsqrt(jnp.mean(x * x, axis=-1, keepdims=True) + EPS)
    o_ref[...] = (y * g_ref[...] * (1.0 + sc_ref[...]) + sh_ref[...]).astype(o_ref.dtype)


def _norm_mod(x, gain, scale, shift):
    s, d = x.shape
    tm = _tile(s, 256)
    row = pl.BlockSpec((1, d), lambda i: (0, 0))
    return pl.pallas_call(
        _norm_mod_body,
        grid=(s // tm,),
        in_specs=[pl.BlockSpec((tm, d), lambda i: (i, 0)), row, row, row],
        out_specs=pl.BlockSpec((tm, d), lambda i: (i, 0)),
        out_shape=jax.ShapeDtypeStruct((s, d), BF16),
        compiler_params=_params(("parallel",)),
        name="norm_mod",
    )(x, gain, scale, shift)


def _fox_proj_body(h_ref, w_ref, g_ref, o_ref, *, n_norm_blocks):
    j = pl.program_id(1)
    y = jnp.dot(h_ref[...], w_ref[...], preferred_element_type=F32)

    @pl.when(j < n_norm_blocks)
    def _():
        g = g_ref[...]
        for hh in range(y.shape[1] // HEAD_DIM):
            sl = slice(hh * HEAD_DIM, (hh + 1) * HEAD_DIM)
            yh = y[:, sl]
            r = lax.rsqrt(jnp.mean(yh * yh, axis=-1, keepdims=True) + EPS)
            o_ref[:, sl] = (yh * r * g[:, sl]).astype(o_ref.dtype)

    @pl.when(j >= n_norm_blocks)
    def _():
        o_ref[...] = y.astype(o_ref.dtype)


def _fox_proj(h, w, gains, fox_width):
    s, d = h.shape
    n = w.shape[1]
    tm, tn = _tile(s, 1024), _tile(fox_width, 512)
    return pl.pallas_call(
        functools.partial(_fox_proj_body, n_norm_blocks=2 * fox_width // tn),
        grid=(s // tm, n // tn),
        in_specs=[pl.BlockSpec((tm, d), lambda i, j: (i, 0)),
                  pl.BlockSpec((d, tn), lambda i, j: (0, j)),
                  pl.BlockSpec((1, tn), lambda i, j: (0, j))],
        out_specs=pl.BlockSpec((tm, tn), lambda i, j: (i, j)),
        out_shape=jax.ShapeDtypeStruct((s, n), BF16),
        compiler_params=_params(("parallel", "arbitrary")),
        name="fox_proj",
    )(h, w, gains)


def _matmul_body(a_ref, b_ref, o_ref):
    o_ref[...] = jnp.dot(a_ref[...], b_ref[...], preferred_element_type=F32).astype(o_ref.dtype)


def _matmul(a, b, out_dtype):
    s, k = a.shape
    n = b.shape[1]
    tm, tn = _tile(s, 1024), _tile(n, 512)
    return pl.pallas_call(
        _matmul_body,
        grid=(s // tm, n // tn),
        in_specs=[pl.BlockSpec((tm, k), lambda i, j: (i, 0)),
                  pl.BlockSpec((k, tn), lambda i, j: (0, j))],
        out_specs=pl.BlockSpec((tm, tn), lambda i, j: (i, j)),
        out_shape=jax.ShapeDtypeStruct((s, n), out_dtype),
        compiler_params=_params(("parallel", "arbitrary")),
        name="hgrn_proj",
    )(a, b)


def _fox_gate_body(w_ref, h_ref, b_ref, o_ref, carry_ref):
    i = pl.program_id(0)

    @pl.when(i == 0)
    def _():
        carry_ref[...] = jnp.zeros_like(carry_ref)

    logit = lax.dot_general(w_ref[...], h_ref[...], _NT, preferred_element_type=F32) + b_ref[...]
    log_f = jnp.minimum(logit, 0.0) - jnp.log1p(jnp.exp(-jnp.abs(logit)))
    ts = log_f.shape[1]
    upper = (lax.broadcasted_iota(jnp.int32, (ts, ts), 0)
             <= lax.broadcasted_iota(jnp.int32, (ts, ts), 1)).astype(BF16)
    csum = sum(jnp.dot(part, upper, preferred_element_type=F32) for part in _split3(log_f))
    o_ref[...] = csum + carry_ref[...]
    carry_ref[...] += csum[:, ts - 1:ts]


def _fox_gate(w_t, h, bias_col):
    fh, d = w_t.shape
    s = h.shape[0]
    ts = _tile(s, 512)
    return pl.pallas_call(
        _fox_gate_body,
        grid=(s // ts,),
        in_specs=[pl.BlockSpec((fh, d), lambda i: (0, 0)),
                  pl.BlockSpec((ts, d), lambda i: (i, 0)),
                  pl.BlockSpec((fh, 1), lambda i: (0, 0))],
        out_specs=pl.BlockSpec((fh, ts), lambda i: (0, i)),
        out_shape=jax.ShapeDtypeStruct((fh, s), F32),
        scratch_shapes=[pltpu.VMEM((fh, 1), F32)],
        compiler_params=_params(("arbitrary",)),
        name="fox_gate",
    )(w_t, h, bias_col)


def _fox_attn_body(q_ref, k_ref, v_ref, c_ref, o_ref, m_ref, l_ref, acc_ref, *, tq, tk):
    i = pl.program_id(1)
    q = q_ref[...]
    q_start = pl.multiple_of(i * tq, tq)
    c_first = c_ref[:, pl.ds(q_start, HEAD_DIM)][:, 0:1]
    m_ref[...] = jnp.full_like(m_ref, MASKED)
    l_ref[...] = jnp.zeros_like(l_ref)
    acc_ref[...] = jnp.zeros_like(acc_ref)

    def block(j, masked):
        k_start = pl.multiple_of(j * tk, tk)
        k = k_ref[pl.ds(k_start, tk), :]
        v = v_ref[pl.ds(k_start, tk), :]
        s = lax.dot_general(q, k, _NT, preferred_element_type=F32)
        s = s + (c_first - c_ref[:, pl.ds(k_start, tk)])
        if masked:
            q_pos = q_start + lax.broadcasted_iota(jnp.int32, (tq, tk), 0)
            k_pos = k_start + lax.broadcasted_iota(jnp.int32, (tq, tk), 1)
            s = jnp.where(k_pos <= q_pos, s, MASKED)
        m_old = m_ref[...]
        m_new = jnp.maximum(m_old, jnp.max(s, axis=-1, keepdims=True))
        alpha = jnp.exp(m_old - m_new)
        p = jnp.exp(s - m_new)
        l_ref[...] = alpha * l_ref[...] + jnp.sum(p, axis=-1, keepdims=True)
        acc_ref[...] = alpha * acc_ref[...] + jnp.dot(p.astype(BF16), v, preferred_element_type=F32)
        m_ref[...] = m_new

    n_full = (i * tq) // tk

    def full_block(j, carry):
        block(j, masked=False)
        return carry

    lax.fori_loop(0, n_full, full_block, 0)
    block(n_full, masked=True)
    o_ref[...] = (acc_ref[...] / l_ref[...]).astype(o_ref.dtype)


def _fox_attention(qkv, cum_f, n_heads):
    s = qkv.shape[0]
    tq = _tile(s, 512)
    tk = _tile(s, 512)
    assert tk % tq == 0
    return pl.pallas_call(
        functools.partial(_fox_attn_body, tq=tq, tk=tk),
        grid=(n_heads, s // tq),
        in_specs=[pl.BlockSpec((tq, HEAD_DIM), lambda h, i: (i, h)),
                  pl.BlockSpec((s, HEAD_DIM), lambda h, i: (0, n_heads + h)),
                  pl.BlockSpec((s, HEAD_DIM), lambda h, i: (0, 2 * n_heads + h)),
                  pl.BlockSpec((None, 1, s), lambda h, i: (h, 0, 0))],
        out_specs=pl.BlockSpec((tq, HEAD_DIM), lambda h, i: (i, h)),
        out_shape=jax.ShapeDtypeStruct((s, n_heads * HEAD_DIM), BF16),
        scratch_shapes=[pltpu.VMEM((tq, 1), F32), pltpu.VMEM((tq, 1), F32),
                        pltpu.VMEM((tq, HEAD_DIM), F32)],
        compiler_params=_params(("parallel", "arbitrary")),
        name="fox_attention",
    )(qkv, qkv, qkv, cum_f)


def _hgrn_body(q_ref, f_ref, i_ref, g_ref, lb_ref, gain_ref, o_ref, state_ref, *, layer):
    @pl.when(pl.program_id(1) == 0)
    def _():
        state_ref[...] = jnp.zeros_like(state_ref)

    table = lb_ref[...]
    e = jnp.exp(table - jnp.max(table, axis=0, keepdims=True))
    lb = jnp.sum(e[:layer + 1], axis=0, keepdims=True) / jnp.sum(e, axis=0, keepdims=True)

    forget = lb + (1.0 - lb) * jax.nn.sigmoid(f_ref[...])
    log_f = jnp.log(forget)
    key = 1.0 - forget
    hq = q_ref[...]
    query = hq * jax.nn.sigmoid(hq) * (HEAD_DIM ** -0.5)
    val = i_ref[...]
    val_b = val.astype(BF16)
    c = log_f.shape[0]

    row = lax.broadcasted_iota(jnp.int32, (c, c), 0)
    col = lax.broadcasted_iota(jnp.int32, (c, c), 1)
    lower = (col <= row).astype(BF16)
    b = sum(jnp.dot(lower, part, preferred_element_type=F32) for part in _split3(log_f))
    b_last = b[c - 1:c, :]

    state_t = state_ref[...]
    out = lax.dot_general((query * jnp.exp(b)).astype(BF16), state_t.astype(BF16), _NT,
                          preferred_element_type=F32)

    t_idx = lax.broadcasted_iota(jnp.int32, (c, HEAD_DIM), 0)
    sep = row ^ col
    scores = jnp.where(sep == 0, lax.dot_general(query.astype(BF16), key.astype(BF16), _NT,
                                                 preferred_element_type=F32), 0.0)
    half = c // 2
    while half >= 1:
        blk = 2 * half
        if half >= 4:
            ref_rows = jnp.broadcast_to(b.reshape(c // blk, blk, HEAD_DIM)[:, half - 1:half, :],
                                        (c // blk, blk, HEAD_DIM)).reshape(c, HEAD_DIM)
        elif half == 2:
            pos = t_idx & 3
            ref_rows = jnp.where(pos == 0, pltpu.roll(b, c - 1, 0),
                                 jnp.where(pos == 2, pltpu.roll(b, 1, 0),
                                           jnp.where(pos == 3, pltpu.roll(b, 2, 0), b)))
        else:
            ref_rows = jnp.where((t_idx & 1) == 1, pltpu.roll(b, 1, 0), b)
        in_upper = (t_idx & half) != 0
        x = jnp.where(in_upper, query, key) * jnp.exp(-jnp.abs(b - ref_rows))
        a_side = jnp.where(in_upper, x, 0.0).astype(BF16)
        b_side = jnp.where(in_upper, 0.0, x).astype(BF16)
        level = lax.dot_general(a_side, b_side, _NT, preferred_element_type=F32)
        scores = scores + (level if blk == c else jnp.where(sep < blk, level, 0.0))
        half //= 2
    out = out + jnp.dot(scores.astype(BF16), val_b, preferred_element_type=F32)

    key_dec = (key * jnp.exp(b_last - b)).astype(BF16)
    state_ref[...] = state_t * jnp.exp(b_last) + jnp.dot(val.T.astype(BF16), key_dec,
                                                         preferred_element_type=F32)

    normed = out * lax.rsqrt(jnp.mean(out * out, axis=-1, keepdims=True) + EPS) * gain_ref[...]
    hg = g_ref[...]
    o_ref[...] = (normed * (hg * jax.nn.sigmoid(hg))).astype(o_ref.dtype)


def _hgrn(proj, lb_table, out_gain, n_heads, layer):
    s = proj.shape[0]
    c = _tile(s, HGRN_CHUNK)
    n_slots = lb_table.shape[0]

    def col(block):
        return pl.BlockSpec((c, HEAD_DIM), lambda h, t: (t, block * n_heads + h))

    return pl.pallas_call(
        functools.partial(_hgrn_body, layer=layer),
        grid=(n_heads, s // c),
        in_specs=[col(0), col(1), col(2), col(3),
                  pl.BlockSpec((n_slots, HEAD_DIM), lambda h, t: (0, h)),
                  pl.BlockSpec((1, HEAD_DIM), lambda h, t: (0, h))],
        out_specs=pl.BlockSpec((c, HEAD_DIM), lambda h, t: (t, h)),
        out_shape=jax.ShapeDtypeStruct((s, n_heads * HEAD_DIM), BF16),
        scratch_shapes=[pltpu.VMEM((HEAD_DIM, HEAD_DIM), F32)],
        compiler_params=_params(("parallel", "arbitrary")),
        name="hgrn2",
    )(proj, proj, proj, proj, lb_table, out_gain)


def _out_proj_body(a_ref, b_ref, wa_ref, wb_ref, x_ref, g_ref, o_ref):
    mix = (jnp.dot(a_ref[...], wa_ref[...], preferred_element_type=F32)
           + jnp.dot(b_ref[...], wb_ref[...], preferred_element_type=F32))
    o_ref[...] = x_ref[...] + g_ref[...] * mix


def _out_proj(fox, hgrn, w_fox, w_hgrn, x, gate):
    s, d = x.shape
    tm, tn = _tile(s, 1024), _tile(d, 512)
    ka, kb = fox.shape[1], hgrn.shape[1]
    return pl.pallas_call(
        _out_proj_body,
        grid=(s // tm, d // tn),
        in_specs=[pl.BlockSpec((tm, ka), lambda i, j: (i, 0)),
                  pl.BlockSpec((tm, kb), lambda i, j: (i, 0)),
                  pl.BlockSpec((ka, tn), lambda i, j: (0, j)),
                  pl.BlockSpec((kb, tn), lambda i, j: (0, j)),
                  pl.BlockSpec((tm, tn), lambda i, j: (i, j)),
                  pl.BlockSpec((1, tn), lambda i, j: (0, j))],
        out_specs=pl.BlockSpec((tm, tn), lambda i, j: (i, j)),
        out_shape=jax.ShapeDtypeStruct((s, d), F32),
        compiler_params=_params(("parallel", "arbitrary")),
        name="out_proj",
    )(fox, hgrn, w_fox, w_hgrn, x, gate)


def _top_rows(vals, k):
    n = vals.shape[0]
    idx = lax.broadcasted_iota(jnp.int32, vals.shape, 0)
    rows = []
    for _ in range(k):
        m = jnp.max(vals, axis=0, keepdims=True)
        rows.append(m)
        first = jnp.min(jnp.where(vals == m, idx, n), axis=0, keepdims=True)
        vals = jnp.where(idx == first, -jnp.inf, vals)
    return rows


def _peer_route_body(h_ref, wq_ref, sk_ref, thr_ref, e0_ref, s1_ref, e1_ref, a_ref, b_ref):
    q = jnp.dot(h_ref[...], wq_ref[...], preferred_element_type=F32).astype(BF16)
    s0 = lax.dot_general(sk_ref[0], q[:, :PEER_KEYS], _NT, preferred_element_type=F32)
    s1 = lax.dot_general(sk_ref[1], q[:, PEER_KEYS:], _NT, preferred_element_type=F32)

    n_top = PEER_TOPK + 1
    a_ref[...] = jnp.full_like(a_ref, -jnp.inf)
    b_ref[...] = jnp.full_like(b_ref, -jnp.inf)
    for r, (ra, rb) in enumerate(zip(_top_rows(s0, n_top), _top_rows(s1, n_top))):
        a_ref[r:r + 1, :] = ra
        b_ref[r:r + 1, :] = rb
    a24, b24 = a_ref[...], b_ref[...]
    a8, b8 = a24[:8], b24[:8]

    r24 = lax.broadcasted_iota(jnp.int32, a24.shape, 0)
    r8 = lax.broadcasted_iota(jnp.int32, a8.shape, 0)
    ninf = -jnp.inf

    def grp(x, rows, lo, hi):
        return jnp.where((rows >= lo) & (rows <= hi), x, ninf)

    cand = jnp.concatenate([
        grp(a24[0:1] + b24, r24, 0, 16),
        grp(a24 + b24[0:1], r24, 1, 16),
        grp(a8[1:2] + b8, r8, 1, 7),
        grp(a8 + b8[1:2], r8, 2, 7),
        grp(a8[2:3] + b8, r8, 2, 4),
        grp(a8 + b8[2:3], r8, 3, 4),
        grp(a8[3:4] + b8, r8, 3, 3),
    ], axis=0)
    top = _top_rows(cand, n_top)
    tau = 0.5 * (top[PEER_TOPK - 1] + top[PEER_TOPK])
    z = sum(jnp.exp(t - top[0]) for t in top[:PEER_TOPK])

    thr_ref[...] = tau - s0
    e0_ref[...] = jnp.exp(s0 - a24[0:1]) / z
    s1_ref[...] = s1
    e1_ref[...] = jnp.exp(s1 - b24[0:1])


def _peer_route(h, wq, sub_keys):
    s, d = h.shape
    ph = sub_keys.shape[0]
    tt = _tile(s, 256)
    out_spec = pl.BlockSpec((None, PEER_KEYS, tt), lambda t, p: (p, 0, t))
    out_sds = jax.ShapeDtypeStruct((ph, PEER_KEYS, s), F32)
    return pl.pallas_call(
        _peer_route_body,
        grid=(s // tt, ph),
        in_specs=[pl.BlockSpec((tt, d), lambda t, p: (t, 0)),
                  pl.BlockSpec((d, 2 * PEER_KEYS), lambda t, p: (0, p)),
                  pl.BlockSpec((None, 2, PEER_KEYS, PEER_KEYS), lambda t, p: (p, 0, 0, 0))],
        out_specs=[out_spec] * 4,
        out_shape=[out_sds] * 4,
        scratch_shapes=[pltpu.VMEM((24, tt), F32), pltpu.VMEM((24, tt), F32)],
        compiler_params=_params(("parallel", "arbitrary")),
        name="peer_route",
    )(h, wq, sub_keys)


def _peer_act_body(u_ref, h_ref, thr_ref, e0_ref, s1_ref, e1_ref, w_ref):
    z = lax.dot_general(u_ref[...], h_ref[...], _NT, preferred_element_type=F32)
    n_heads = s1_ref.shape[0]
    for ii in range(z.shape[0] // PEER_KEYS):
        zi = z[ii * PEER_KEYS:(ii + 1) * PEER_KEYS, :]
        act = 0.5 * zi * (1.0 + lax.erf(zi * SQRT_HALF))
        gate = jnp.zeros_like(zi)
        for hd in range(n_heads):
            sel = s1_ref[hd] >= thr_ref[hd, ii:ii + 1, :]
            gate = gate + jnp.where(sel, e1_ref[hd] * e0_ref[hd, ii:ii + 1, :], 0.0)
        w_ref[:, ii * PEER_KEYS:(ii + 1) * PEER_KEYS] = (act * gate).T.astype(w_ref.dtype)


def _peer_act(u, h, thr, e0, s1, e1):
    n_exp, d = u.shape
    s = h.shape[0]
    ph = thr.shape[0]
    te = 8 * PEER_KEYS
    tt = _tile(s, 512)
    assert n_exp % te == 0
    per_i = pl.BlockSpec((ph, 8, tt), lambda t, e: (0, e, t))
    per_j = pl.BlockSpec((ph, PEER_KEYS, tt), lambda t, e: (0, 0, t))
    return pl.pallas_call(
        _peer_act_body,
        grid=(s // tt, n_exp // te),
        in_specs=[pl.BlockSpec((te, d), lambda t, e: (e, 0)),
                  pl.BlockSpec((tt, d), lambda t, e: (t, 0)),
                  per_i, per_i, per_j, per_j],
        out_specs=pl.BlockSpec((tt, te), lambda t, e: (t, e)),
        out_shape=jax.ShapeDtypeStruct((s, n_exp), BF16),
        compiler_params=_params(("parallel", "arbitrary")),
        name="peer_act",
    )(u, h, thr, e0, s1, e1)


def _peer_down_body(w_ref, v_ref, x_ref, g_ref, o_ref, acc_ref):
    k = pl.program_id(2)

    @pl.when(k == 0)
    def _():
        acc_ref[...] = jnp.zeros_like(acc_ref)

    acc_ref[...] += jnp.dot(w_ref[...], v_ref[...], preferred_element_type=F32)

    @pl.when(k == pl.num_programs(2) - 1)
    def _():
        o_ref[...] = x_ref[...] + g_ref[...] * acc_ref[...]


def _peer_down(w, v, x, gate):
    s, n_exp = w.shape
    d = v.shape[1]
    tm, tn, tk = _tile(s, 1024), _tile(d, 1024), _tile(n_exp, 1024)
    return pl.pallas_call(
        _peer_down_body,
        grid=(s // tm, d // tn, n_exp // tk),
        in_specs=[pl.BlockSpec((tm, tk), lambda i, j, k: (i, k)),
                  pl.BlockSpec((tk, tn), lambda i, j, k: (k, j)),
                  pl.BlockSpec((tm, tn), lambda i, j, k: (i, j)),
                  pl.BlockSpec((1, tn), lambda i, j, k: (0, j))],
        out_specs=pl.BlockSpec((tm, tn), lambda i, j, k: (i, j)),
        out_shape=jax.ShapeDtypeStruct((s, d), F32),
        scratch_shapes=[pltpu.VMEM((tm, tn), F32)],
        compiler_params=_params(("parallel", "parallel", "arbitrary")),
        name="peer_down",
    )(w, v, x, gate)


def kernel(x, c, ada_w, ada_b, norm1_gain, norm2_gain, w_in, fox_f_bias, fox_q_gain, fox_k_gain,
           hgrn_lower_bounds, hgrn_out_gain, w_out, peer_w_query, peer_sub_keys, peer_u, peer_v):
    batch, seq, d = x.shape
    assert batch == 1, "one sequence per call"
    depth = ada_w.shape[0]
    fox_heads = fox_f_bias.shape[-1]
    hgrn_heads = hgrn_out_gain.shape[1]
    fox_width, hgrn_width = fox_heads * HEAD_DIM, hgrn_heads * HEAD_DIM
    assert fox_q_gain.shape[-1] == HEAD_DIM and hgrn_out_gain.shape[-1] == HEAD_DIM
    assert w_in.shape[-1] == 3 * fox_width + fox_heads + 4 * hgrn_width
    assert peer_sub_keys.shape[2:] == (2, PEER_KEYS, PEER_KEYS) and peer_u.shape[1] == PEER_KEYS ** 2
    o3 = 3 * fox_width
    o4 = o3 + fox_heads

    xs = x.reshape(seq, d)
    c_col = c.reshape(d, 1)
    for layer in range(depth):
        mod = _ada_mod(c_col, ada_w[layer], ada_b[layer].reshape(1, -1))
        shift1, scale1, gate1, shift2, scale2, gate2 = (mod[:, n * d:(n + 1) * d] for n in range(6))

        h = _norm_mod(xs, norm1_gain[layer].reshape(1, d), scale1, shift1)
        w_l = w_in[layer]
        qk_gains = jnp.concatenate([fox_q_gain[layer].reshape(1, fox_width) * (HEAD_DIM ** -0.5),
                                    fox_k_gain[layer].reshape(1, fox_width),
                                    jnp.ones((1, fox_width), F32)], axis=1)
        qkv = _fox_proj(h, w_l[:, :o3].astype(BF16), qk_gains, fox_width)
        cum_f = _fox_gate(w_l[:, o3:o4].T.astype(BF16), h, fox_f_bias[layer].reshape(fox_heads, 1))
        fox_out = _fox_attention(qkv, cum_f.reshape(fox_heads, 1, seq), fox_heads)

        hproj = _matmul(h, w_l[:, o4:].astype(BF16), F32)
        hgrn_out = _hgrn(hproj, hgrn_lower_bounds, hgrn_out_gain[layer].reshape(1, hgrn_width),
                         hgrn_heads, layer)

        w_o = w_out[layer].astype(BF16)
        xs = _out_proj(fox_out, hgrn_out, w_o[:fox_width], w_o[fox_width:], xs, gate1)

        h = _norm_mod(xs, norm2_gain[layer].reshape(1, d), scale2, shift2)
        thr, e0, s1, e1 = _peer_route(h, peer_w_query[layer].astype(BF16),
                                      peer_sub_keys[layer].astype(BF16))
        w_act = _peer_act(peer_u[layer].astype(BF16), h, thr, e0, s1, e1)
        xs = _peer_down(w_act, peer_v[layer].astype(BF16), xs, gate2)
    return xs.reshape(batch, seq, d)
```

```python
import functools

import jax
import jax.numpy as jnp
from jax import lax
from jax.experimental import pallas as pl
from jax.experimental.pallas import tpu as pltpu

F32 = jnp.float32
BF16 = jnp.bfloat16

HEAD_DIM = 128
PEER_KEYS = 128
PEER_TOPK = 16
HGRN_CHUNK = 128
EPS = 1e-6
MASKED = -1e30
SQRT_HALF = 0.7071067811865476
LOG2_E = 1.4426950408889634
ATTN_LANE_GROUP = 256
V7X_VMEM_LIMIT_BYTES = 52 * 1024 * 1024

_NT = (((1,), (1,)), ((), ()))


def _params(semantics):
    return pltpu.CompilerParams(dimension_semantics=semantics, vmem_limit_bytes=V7X_VMEM_LIMIT_BYTES)


def _tile(n, t):
    if n <= t:
        return n
    t -= t % HEAD_DIM
    while n % t:
        t -= HEAD_DIM
    assert t > 0, n
    return t


def _split3(x):
    hi = x.astype(BF16)
    r1 = x - hi.astype(F32)
    mid = r1.astype(BF16)
    lo = (r1 - mid.astype(F32)).astype(BF16)
    return hi, mid, lo


def _ada_body(c_ref, w_ref, b_ref, o_ref, acc_ref):
    k = pl.program_id(1)

    @pl.when(k == 0)
    def _():
        acc_ref[...] = jnp.zeros_like(acc_ref)

    c = c_ref[...]
    c_act = c * jax.nn.sigmoid(c)
    acc_ref[...] += jnp.sum(w_ref[...] * c_act, axis=0, keepdims=True)

    @pl.when(k == pl.num_programs(1) - 1)
    def _():
        o_ref[...] = acc_ref[...] + b_ref[...]


def _ada_mod(c_col, w, b_row):
    d, n = w.shape
    tk, tn = _tile(d, 512), _tile(n, 2048)
    return pl.pallas_call(
        _ada_body,
        grid=(n // tn, d // tk),
        in_specs=[pl.BlockSpec((tk, 1), lambda j, k: (k, 0)),
                  pl.BlockSpec((tk, tn), lambda j, k: (k, j)),
                  pl.BlockSpec((1, tn), lambda j, k: (0, j))],
        out_specs=pl.BlockSpec((1, tn), lambda j, k: (0, j)),
        out_shape=jax.ShapeDtypeStruct((1, n), F32),
        scratch_shapes=[pltpu.VMEM((1, tn), F32)],
        compiler_params=_params(("parallel", "arbitrary")),
        name="ada_mod",
    )(c_col, w, b_row)


def _norm_mod_body(x_ref, g_ref, sc_ref, sh_ref, o_ref, *maybe_ot_ref):
    x = x_ref[...]
    y = x * lax.rsqrt(jnp.mean(x * x, axis=-1, keepdims=True) + EPS)
    h = y * g_ref[...] * (1.0 + sc_ref[...]) + sh_ref[...]
    o_ref[...] = h.astype(o_ref.dtype)
    for ot_ref in maybe_ot_ref:
        ot_ref[...] = h.T.astype(ot_ref.dtype)


def _norm_mod(x, gain, scale, shift, with_transposed=False):
    s, d = x.shape
    tm = _tile(s, 256)
    row = pl.BlockSpec((1, d), lambda i: (0, 0))
    out_specs = [pl.BlockSpec((tm, d), lambda i: (i, 0))]
    out_shape = [jax.ShapeDtypeStruct((s, d), BF16)]
    if with_transposed:
        out_specs.append(pl.BlockSpec((d, tm), lambda i: (0, i)))
        out_shape.append(jax.ShapeDtypeStruct((d, s), BF16))
    outs = pl.pallas_call(
        _norm_mod_body,
        grid=(s // tm,),
        in_specs=[pl.BlockSpec((tm, d), lambda i: (i, 0)), row, row, row],
        out_specs=out_specs,
        out_shape=out_shape,
        compiler_params=_params(("parallel",)),
        name="norm_mod",
    )(x, gain, scale, shift)
    return outs if with_transposed else outs[0]


def _fox_proj_body(h_ref, w_ref, g_ref, o_ref):
    y = jnp.dot(h_ref[...], w_ref[...], preferred_element_type=F32)
    g = g_ref[...]
    for hh in range(y.shape[1] // HEAD_DIM):
        sl = slice(hh * HEAD_DIM, (hh + 1) * HEAD_DIM)
        yh = y[:, sl]
        r = lax.rsqrt(jnp.mean(yh * yh, axis=-1, keepdims=True) + EPS)
        o_ref[:, sl] = (yh * r * g[:, sl]).astype(o_ref.dtype)


def _fox_proj(h, w, gains):
    s, d = h.shape
    n = w.shape[1]
    tm, tn = _tile(s, 1024), _tile(n, 512)
    return pl.pallas_call(
        _fox_proj_body,
        grid=(s // tm, n // tn),
        in_specs=[pl.BlockSpec((tm, d), lambda i, j: (i, 0)),
                  pl.BlockSpec((d, tn), lambda i, j: (0, j)),
                  pl.BlockSpec((1, tn), lambda i, j: (0, j))],
        out_specs=pl.BlockSpec((tm, tn), lambda i, j: (i, j)),
        out_shape=jax.ShapeDtypeStruct((s, n), BF16),
        compiler_params=_params(("parallel", "arbitrary")),
        name="fox_proj",
    )(h, w, gains)


def _matmul_body(a_ref, b_ref, o_ref):
    o_ref[...] = jnp.dot(a_ref[...], b_ref[...], preferred_element_type=F32).astype(o_ref.dtype)


def _matmul(a, b, out_dtype):
    s, k = a.shape
    n = b.shape[1]
    tm, tn = _tile(s, 1024), _tile(n, 512)
    return pl.pallas_call(
        _matmul_body,
        grid=(s // tm, n // tn),
        in_specs=[pl.BlockSpec((tm, k), lambda i, j: (i, 0)),
                  pl.BlockSpec((k, tn), lambda i, j: (0, j))],
        out_specs=pl.BlockSpec((tm, tn), lambda i, j: (i, j)),
        out_shape=jax.ShapeDtypeStruct((s, n), out_dtype),
        compiler_params=_params(("parallel", "arbitrary")),
        name="hgrn_proj",
    )(a, b)


def _matmul_nt_body(a_ref, b_ref, o_ref):
    o_ref[...] = lax.dot_general(a_ref[...], b_ref[...], _NT,
                                 preferred_element_type=F32).astype(o_ref.dtype)


def _matmul_nt(a, b, out_dtype):
    m, k = a.shape
    n = b.shape[0]
    tm, tn = _tile(m, 512), _tile(n, 1024)
    return pl.pallas_call(
        _matmul_nt_body,
        grid=(n // tn, m // tm),
        in_specs=[pl.BlockSpec((tm, k), lambda j, i: (i, 0)),
                  pl.BlockSpec((tn, k), lambda j, i: (j, 0))],
        out_specs=pl.BlockSpec((tm, tn), lambda j, i: (i, j)),
        out_shape=jax.ShapeDtypeStruct((m, n), out_dtype),
        compiler_params=_params(("parallel", "arbitrary")),
        name="fox_v_proj",
    )(a, b)


def _fox_gate_body(h_ref, w_ref, b_ref, o_ref, carry_ref):
    i = pl.program_id(0)

    @pl.when(i == 0)
    def _():
        carry_ref[...] = jnp.zeros_like(carry_ref)

    logit = jnp.dot(h_ref[...], w_ref[...], preferred_element_type=F32) + b_ref[...]
    log_f = jnp.minimum(logit, 0.0) - jnp.log1p(jnp.exp(-jnp.abs(logit)))
    ts = log_f.shape[0]
    lower = (lax.broadcasted_iota(jnp.int32, (ts, ts), 1)
             <= lax.broadcasted_iota(jnp.int32, (ts, ts), 0)).astype(BF16)
    csum = sum(jnp.dot(lower, part, preferred_element_type=F32) for part in _split3(log_f))
    csum = csum + carry_ref[...]
    carry_ref[...] = csum[ts - 1:ts, :]
    scaled = csum * LOG2_E
    for hd in range(o_ref.shape[0]):
        o_ref[hd] = jnp.broadcast_to(scaled[:, hd:hd + 1], (ts, HEAD_DIM))


def _fox_gate(h, w_pad, bias_pad, n_heads):
    s, d = h.shape
    ts = _tile(s, 512)
    return pl.pallas_call(
        _fox_gate_body,
        grid=(s // ts,),
        in_specs=[pl.BlockSpec((ts, d), lambda i: (i, 0)),
                  pl.BlockSpec((d, HEAD_DIM), lambda i: (0, 0)),
                  pl.BlockSpec((1, HEAD_DIM), lambda i: (0, 0))],
        out_specs=pl.BlockSpec((n_heads, ts, HEAD_DIM), lambda i: (0, i, 0)),
        out_shape=jax.ShapeDtypeStruct((n_heads, s, HEAD_DIM), F32),
        scratch_shapes=[pltpu.VMEM((1, HEAD_DIM), F32)],
        compiler_params=_params(("arbitrary",)),
        name="fox_gate",
    )(h, w_pad, bias_pad)


def _fox_attn_body(q_ref, k_ref, vt_ref, cb_ref, o_ref, sa_ref, sb_ref, m_ref, l_ref, acc_ref, *, t):
    i = pl.program_id(1)
    q = q_ref[...]
    q_start = pl.multiple_of(i * t, t)
    c_first = cb_ref[pl.ds(q_start, 8), :][0:1]
    m_ref[...] = jnp.full_like(m_ref, MASKED)
    l_ref[...] = jnp.zeros_like(l_ref)
    acc_ref[...] = jnp.zeros_like(acc_ref)
    group = min(t, ATTN_LANE_GROUP)

    def scores(j):
        k_start = pl.multiple_of(j * t, t)
        return lax.dot_general(k_ref[pl.ds(k_start, t), :], q, _NT, preferred_element_type=F32)

    def consume(j, sc_ref, masked):
        k_start = pl.multiple_of(j * t, t)
        bias = c_first - cb_ref[pl.ds(k_start, t), :]
        bias = jnp.concatenate([bias] * (group // HEAD_DIM), axis=1)
        v_t = vt_ref[:, pl.ds(k_start, t)]
        for g in range(t // group):
            lanes = slice(g * group, (g + 1) * group)
            s = sc_ref[:, lanes] + bias
            if masked:
                k_pos = lax.broadcasted_iota(jnp.int32, (t, group), 0)
                q_pos = lax.broadcasted_iota(jnp.int32, (t, group), 1) + g * group
                s = jnp.where(k_pos <= q_pos, s, MASKED)
            m_old = m_ref[:, lanes]
            m_new = jnp.maximum(m_old, jnp.max(s, axis=0, keepdims=True))
            alpha = jnp.exp2(m_old - m_new)
            p = jnp.exp2(s - m_new)
            l_ref[:, lanes] = alpha * l_ref[:, lanes] + jnp.sum(p, axis=0, keepdims=True)
            acc_ref[:, lanes] = alpha * acc_ref[:, lanes] + jnp.dot(
                v_t, p.astype(BF16), preferred_element_type=F32)
            m_ref[:, lanes] = m_new

    sa_ref[...] = scores(0)

    def pair(jj, carry):
        j = 2 * jj
        sb_ref[...] = scores(j + 1)
        consume(j, sa_ref, masked=False)
        sa_ref[...] = scores(j + 2)
        consume(j + 1, sb_ref, masked=False)
        return carry

    lax.fori_loop(0, lax.shift_right_logical(i, 1), pair, 0)
    odd = (i & 1) == 1

    @pl.when(jnp.logical_not(odd))
    def _():
        consume(i, sa_ref, masked=True)

    @pl.when(odd)
    def _():
        sb_ref[...] = scores(i)
        consume(i - 1, sa_ref, masked=False)
        consume(i, sb_ref, masked=True)

    o_ref[...] = (acc_ref[...] / l_ref[...]).T.astype(o_ref.dtype)


def _fox_attention(qk, v_t, cum_f, n_heads):
    s = qk.shape[0]
    t = _tile(s, 512)
    return pl.pallas_call(
        functools.partial(_fox_attn_body, t=t),
        grid=(n_heads, s // t),
        in_specs=[pl.BlockSpec((t, HEAD_DIM), lambda h, i: (i, h)),
                  pl.BlockSpec((s, HEAD_DIM), lambda h, i: (0, n_heads + h)),
                  pl.BlockSpec((HEAD_DIM, s), lambda h, i: (h, 0)),
                  pl.BlockSpec((None, s, HEAD_DIM), lambda h, i: (h, 0, 0))],
        out_specs=pl.BlockSpec((t, HEAD_DIM), lambda h, i: (i, h)),
        out_shape=jax.ShapeDtypeStruct((s, n_heads * HEAD_DIM), BF16),
        scratch_shapes=[pltpu.VMEM((t, t), F32), pltpu.VMEM((t, t), F32),
                        pltpu.VMEM((1, t), F32), pltpu.VMEM((1, t), F32),
                        pltpu.VMEM((HEAD_DIM, t), F32)],
        compiler_params=_params(("parallel", "arbitrary")),
        name="fox_attention",
    )(qk, qk, v_t, cum_f)


def _hgrn_body(q_ref, f_ref, i_ref, g_ref, lb_ref, gain_ref, o_ref, state_ref, *, layer):
    @pl.when(pl.program_id(1) == 0)
    def _():
        state_ref[...] = jnp.zeros_like(state_ref)

    table = lb_ref[...]
    e = jnp.exp(table - jnp.max(table, axis=0, keepdims=True))
    lb = jnp.sum(e[:layer + 1], axis=0, keepdims=True) / jnp.sum(e, axis=0, keepdims=True)

    forget = lb + (1.0 - lb) * jax.nn.sigmoid(f_ref[...])
    log_f = jnp.log(forget)
    key = 1.0 - forget
    hq = q_ref[...]
    query = hq * jax.nn.sigmoid(hq) * (HEAD_DIM ** -0.5)
    val = i_ref[...]
    val_b = val.astype(BF16)
    c = log_f.shape[0]

    row = lax.broadcasted_iota(jnp.int32, (c, c), 0)
    col = lax.broadcasted_iota(jnp.int32, (c, c), 1)
    lower = (col <= row).astype(BF16)
    b = sum(jnp.dot(lower, part, preferred_element_type=F32) for part in _split3(log_f))
    b_last = b[c - 1:c, :]

    state_t = state_ref[...]
    out = lax.dot_general((query * jnp.exp(b)).astype(BF16), state_t.astype(BF16), _NT,
                          preferred_element_type=F32)

    t_idx = lax.broadcasted_iota(jnp.int32, (c, HEAD_DIM), 0)
    sep = row ^ col
    scores = jnp.where(sep == 0, lax.dot_general(query.astype(BF16), key.astype(BF16), _NT,
                                                 preferred_element_type=F32), 0.0)
    half = c // 2
    while half >= 1:
        blk = 2 * half
        if half >= 4:
            ref_rows = jnp.broadcast_to(b.reshape(c // blk, blk, HEAD_DIM)[:, half - 1:half, :],
                                        (c // blk, blk, HEAD_DIM)).reshape(c, HEAD_DIM)
        elif half == 2:
            pos = t_idx & 3
            ref_rows = jnp.where(pos == 0, pltpu.roll(b, c - 1, 0),
                                 jnp.where(pos == 2, pltpu.roll(b, 1, 0),
                                           jnp.where(pos == 3, pltpu.roll(b, 2, 0), b)))
        else:
            ref_rows = jnp.where((t_idx & 1) == 1, pltpu.roll(b, 1, 0), b)
        in_upper = (t_idx & half) != 0
        x = jnp.where(in_upper, query, key) * jnp.exp(-jnp.abs(b - ref_rows))
        a_side = jnp.where(in_upper, x, 0.0).astype(BF16)
        b_side = jnp.where(in_upper, 0.0, x).astype(BF16)
        level = lax.dot_general(a_side, b_side, _NT, preferred_element_type=F32)
        scores = scores + (level if blk == c else jnp.where(sep < blk, level, 0.0))
        half //= 2
    out = out + jnp.dot(scores.astype(BF16), val_b, preferred_element_type=F32)

    key_dec = (key * jnp.exp(b_last - b)).astype(BF16)
    state_ref[...] = state_t * jnp.exp(b_last) + jnp.dot(val.T.astype(BF16), key_dec,
                                                         preferred_element_type=F32)

    normed = out * lax.rsqrt(jnp.mean(out * out, axis=-1, keepdims=True) + EPS) * gain_ref[...]
    hg = g_ref[...]
    o_ref[...] = (normed * (hg * jax.nn.sigmoid(hg))).astype(o_ref.dtype)


def _hgrn(proj, lb_table, out_gain, n_heads, layer):
    s = proj.shape[0]
    c = _tile(s, HGRN_CHUNK)
    n_slots = lb_table.shape[0]

    def col(block):
        return pl.BlockSpec((c, HEAD_DIM), lambda h, t: (t, block * n_heads + h))

    return pl.pallas_call(
        functools.partial(_hgrn_body, layer=layer),
        grid=(n_heads, s // c),
        in_specs=[col(0), col(1), col(2), col(3),
                  pl.BlockSpec((n_slots, HEAD_DIM), lambda h, t: (0, h)),
                  pl.BlockSpec((1, HEAD_DIM), lambda h, t: (0, h))],
        out_specs=pl.BlockSpec((c, HEAD_DIM), lambda h, t: (t, h)),
        out_shape=jax.ShapeDtypeStruct((s, n_heads * HEAD_DIM), BF16),
        scratch_shapes=[pltpu.VMEM((HEAD_DIM, HEAD_DIM), F32)],
        compiler_params=_params(("parallel", "arbitrary")),
        name="hgrn2",
    )(proj, proj, proj, proj, lb_table, out_gain)


def _out_proj_body(a_ref, b_ref, wa_ref, wb_ref, x_ref, g_ref, o_ref):
    mix = (jnp.dot(a_ref[...], wa_ref[...], preferred_element_type=F32)
           + jnp.dot(b_ref[...], wb_ref[...], preferred_element_type=F32))
    o_ref[...] = x_ref[...] + g_ref[...] * mix


def _out_proj(fox, hgrn, w_fox, w_hgrn, x, gate):
    s, d = x.shape
    tm, tn = _tile(s, 1024), _tile(d, 512)
    ka, kb = fox.shape[1], hgrn.shape[1]
    return pl.pallas_call(
        _out_proj_body,
        grid=(s // tm, d // tn),
        in_specs=[pl.BlockSpec((tm, ka), lambda i, j: (i, 0)),
                  pl.BlockSpec((tm, kb), lambda i, j: (i, 0)),
                  pl.BlockSpec((ka, tn), lambda i, j: (0, j)),
                  pl.BlockSpec((kb, tn), lambda i, j: (0, j)),
                  pl.BlockSpec((tm, tn), lambda i, j: (i, j)),
                  pl.BlockSpec((1, tn), lambda i, j: (0, j))],
        out_specs=pl.BlockSpec((tm, tn), lambda i, j: (i, j)),
        out_shape=jax.ShapeDtypeStruct((s, d), F32),
        compiler_params=_params(("parallel", "arbitrary")),
        name="out_proj",
    )(fox, hgrn, w_fox, w_hgrn, x, gate)


def _top_rows(vals, k):
    n = vals.shape[0]
    idx = lax.broadcasted_iota(jnp.int32, vals.shape, 0)
    rows = []
    for _ in range(k):
        m = jnp.max(vals, axis=0, keepdims=True)
        rows.append(m)
        first = jnp.min(jnp.where(vals == m, idx, n), axis=0, keepdims=True)
        vals = jnp.where(idx == first, -jnp.inf, vals)
    return rows


def _peer_route_body(h_ref, wq_ref, sk_ref, thr_ref, e0_ref, s1_ref, e1_ref, a_ref, b_ref):
    q = jnp.dot(h_ref[...], wq_ref[...], preferred_element_type=F32).astype(BF16)
    s0 = lax.dot_general(sk_ref[0], q[:, :PEER_KEYS], _NT, preferred_element_type=F32)
    s1 = lax.dot_general(sk_ref[1], q[:, PEER_KEYS:], _NT, preferred_element_type=F32)

    n_top = PEER_TOPK + 1
    a_ref[...] = jnp.full_like(a_ref, -jnp.inf)
    b_ref[...] = jnp.full_like(b_ref, -jnp.inf)
    for r, (ra, rb) in enumerate(zip(_top_rows(s0, n_top), _top_rows(s1, n_top))):
        a_ref[r:r + 1, :] = ra
        b_ref[r:r + 1, :] = rb
    a24, b24 = a_ref[...], b_ref[...]
    a8, b8 = a24[:8], b24[:8]

    r24 = lax.broadcasted_iota(jnp.int32, a24.shape, 0)
    r8 = lax.broadcasted_iota(jnp.int32, a8.shape, 0)
    ninf = -jnp.inf

    def grp(x, rows, lo, hi):
        return jnp.where((rows >= lo) & (rows <= hi), x, ninf)

    cand = jnp.concatenate([
        grp(a24[0:1] + b24, r24, 0, 16),
        grp(a24 + b24[0:1], r24, 1, 16),
        grp(a8[1:2] + b8, r8, 1, 7),
        grp(a8 + b8[1:2], r8, 2, 7),
        grp(a8[2:3] + b8, r8, 2, 4),
        grp(a8 + b8[2:3], r8, 3, 4),
        grp(a8[3:4] + b8, r8, 3, 3),
    ], axis=0)
    top = _top_rows(cand, n_top)
    tau = 0.5 * (top[PEER_TOPK - 1] + top[PEER_TOPK])
    z = sum(jnp.exp(t - top[0]) for t in top[:PEER_TOPK])

    thr_ref[...] = tau - s0
    e0_ref[...] = jnp.exp(s0 - a24[0:1]) / z
    s1_ref[...] = s1
    e1_ref[...] = jnp.exp(s1 - b24[0:1])


def _peer_route(h, wq, sub_keys):
    s, d = h.shape
    ph = sub_keys.shape[0]
    tt = _tile(s, 256)
    out_spec = pl.BlockSpec((None, PEER_KEYS, tt), lambda t, p: (p, 0, t))
    out_sds = jax.ShapeDtypeStruct((ph, PEER_KEYS, s), F32)
    return pl.pallas_call(
        _peer_route_body,
        grid=(s // tt, ph),
        in_specs=[pl.BlockSpec((tt, d), lambda t, p: (t, 0)),
                  pl.BlockSpec((d, 2 * PEER_KEYS), lambda t, p: (0, p)),
                  pl.BlockSpec((None, 2, PEER_KEYS, PEER_KEYS), lambda t, p: (p, 0, 0, 0))],
        out_specs=[out_spec] * 4,
        out_shape=[out_sds] * 4,
        scratch_shapes=[pltpu.VMEM((24, tt), F32), pltpu.VMEM((24, tt), F32)],
        compiler_params=_params(("parallel", "arbitrary")),
        name="peer_route",
    )(h, wq, sub_keys)


def _peer_act_body(u_ref, ht_ref, thr_ref, e0_ref, s1_ref, e1_ref, w_ref, *, rows):
    n_heads = s1_ref.shape[0]
    te = u_ref.shape[0]
    for ch in range(te // rows):
        z = jnp.dot(u_ref[ch * rows:(ch + 1) * rows, :], ht_ref[...], preferred_element_type=F32)
        for ci in range(rows // PEER_KEYS):
            ii = ch * (rows // PEER_KEYS) + ci
            zi = z[ci * PEER_KEYS:(ci + 1) * PEER_KEYS, :]
            act = 0.5 * zi * (1.0 + lax.erf(zi * SQRT_HALF))
            gate = jnp.zeros_like(zi)
            for hd in range(n_heads):
                sel = s1_ref[hd] >= thr_ref[hd, ii:ii + 1, :]
                gate = gate + jnp.where(sel, e1_ref[hd] * e0_ref[hd, ii:ii + 1, :], 0.0)
            w_ref[:, ii * PEER_KEYS:(ii + 1) * PEER_KEYS] = (act * gate).T.astype(w_ref.dtype)


def _peer_act(u, h_t, thr, e0, s1, e1):
    n_exp, d = u.shape
    s = h_t.shape[1]
    ph = thr.shape[0]
    te = 8 * PEER_KEYS
    tt = _tile(s, 512)
    assert n_exp % te == 0
    per_i = pl.BlockSpec((ph, 8, tt), lambda t, e: (0, e, t))
    per_j = pl.BlockSpec((ph, PEER_KEYS, tt), lambda t, e: (0, 0, t))
    return pl.pallas_call(
        functools.partial(_peer_act_body, rows=te // 2),
        grid=(s // tt, n_exp // te),
        in_specs=[pl.BlockSpec((te, d), lambda t, e: (e, 0)),
                  pl.BlockSpec((d, tt), lambda t, e: (0, t)),
                  per_i, per_i, per_j, per_j],
        out_specs=pl.BlockSpec((tt, te), lambda t, e: (t, e)),
        out_shape=jax.ShapeDtypeStruct((s, n_exp), BF16),
        compiler_params=_params(("parallel", "arbitrary")),
        name="peer_act",
    )(u, h_t, thr, e0, s1, e1)


def _peer_down_body(w_ref, v_ref, x_ref, g_ref, o_ref, acc_ref):
    k = pl.program_id(2)

    @pl.when(k == 0)
    def _():
        acc_ref[...] = jnp.zeros_like(acc_ref)

    acc_ref[...] += jnp.dot(w_ref[...], v_ref[...], preferred_element_type=F32)

    @pl.when(k == pl.num_programs(2) - 1)
    def _():
        o_ref[...] = x_ref[...] + g_ref[...] * acc_ref[...]


def _peer_down(w, v, x, gate):
    s, n_exp = w.shape
    d = v.shape[1]
    tm, tn, tk = _tile(s, 1024), _tile(d, 1024), _tile(n_exp, 1024)
    return pl.pallas_call(
        _peer_down_body,
        grid=(s // tm, d // tn, n_exp // tk),
        in_specs=[pl.BlockSpec((tm, tk), lambda i, j, k: (i, k)),
                  pl.BlockSpec((tk, tn), lambda i, j, k: (k, j)),
                  pl.BlockSpec((tm, tn), lambda i, j, k: (i, j)),
                  pl.BlockSpec((1, tn), lambda i, j, k: (0, j))],
        out_specs=pl.BlockSpec((tm, tn), lambda i, j, k: (i, j)),
        out_shape=jax.ShapeDtypeStruct((s, d), F32),
        scratch_shapes=[pltpu.VMEM((tm, tn), F32)],
        compiler_params=_params(("parallel", "parallel", "arbitrary")),
        name="peer_down",
    )(w, v, x, gate)


def kernel(x, c, ada_w, ada_b, norm1_gain, norm2_gain, w_in, fox_f_bias, fox_q_gain, fox_k_gain,
           hgrn_lower_bounds, hgrn_out_gain, w_out, peer_w_query, peer_sub_keys, peer_u, peer_v):
    batch, seq, d = x.shape
    assert batch == 1, "one sequence per call"
    depth = ada_w.shape[0]
    fox_heads = fox_f_bias.shape[-1]
    hgrn_heads = hgrn_out_gain.shape[1]
    fox_width, hgrn_width = fox_heads * HEAD_DIM, hgrn_heads * HEAD_DIM
    assert fox_q_gain.shape[-1] == HEAD_DIM and hgrn_out_gain.shape[-1] == HEAD_DIM
    assert w_in.shape[-1] == 3 * fox_width + fox_heads + 4 * hgrn_width
    assert peer_sub_keys.shape[2:] == (2, PEER_KEYS, PEER_KEYS) and peer_u.shape[1] == PEER_KEYS ** 2
    o3 = 3 * fox_width
    o4 = o3 + fox_heads

    xs = x.reshape(seq, d)
    c_col = c.reshape(d, 1)
    for layer in range(depth):
        mod = _ada_mod(c_col, ada_w[layer], ada_b[layer].reshape(1, -1))
        shift1, scale1, gate1, shift2, scale2, gate2 = (mod[:, n * d:(n + 1) * d] for n in range(6))

        h = _norm_mod(xs, norm1_gain[layer].reshape(1, d), scale1, shift1)
        w_l = w_in[layer]
        qk_gains = jnp.concatenate([fox_q_gain[layer].reshape(1, fox_width) * (HEAD_DIM ** -0.5 * LOG2_E),
                                    fox_k_gain[layer].reshape(1, fox_width)], axis=1)
        qk = _fox_proj(h, w_l[:, :2 * fox_width].astype(BF16), qk_gains)
        v_t = _matmul_nt(w_l[:, 2 * fox_width:o3].T.astype(BF16), h, BF16)
        pad = HEAD_DIM - fox_heads
        cum_f = _fox_gate(h, jnp.pad(w_l[:, o3:o4], ((0, 0), (0, pad))).astype(BF16),
                          jnp.pad(fox_f_bias[layer].reshape(1, fox_heads), ((0, 0), (0, pad))), fox_heads)
        fox_out = _fox_attention(qk, v_t, cum_f, fox_heads)

        hproj = _matmul(h, w_l[:, o4:].astype(BF16), F32)
        hgrn_out = _hgrn(hproj, hgrn_lower_bounds, hgrn_out_gain[layer].reshape(1, hgrn_width),
                         hgrn_heads, layer)

        w_o = w_out[layer].astype(BF16)
        xs = _out_proj(fox_out, hgrn_out, w_o[:fox_width], w_o[fox_width:], xs, gate1)

        h, h_t = _norm_mod(xs, norm2_gain[layer].reshape(1, d), scale2, shift2, with_transposed=True)
        thr, e0, s1, e1 = _peer_route(h, peer_w_query[layer].astype(BF16),
                                      peer_sub_keys[layer].astype(BF16))
        w_act = _peer_act(peer_u[layer].astype(BF16), h_t, thr, e0, s1, e1)
        xs = _peer_down(w_act, peer_v[layer].astype(BF16), xs, gate2)
    return xs.reshape(batch, seq, d)
```

```python
import functools

import jax
import jax.numpy as jnp
from jax import lax
from jax.experimental import pallas as pl
from jax.experimental.pallas import tpu as pltpu

F32 = jnp.float32
BF16 = jnp.bfloat16

HEAD_DIM = 128
PEER_KEYS = 128
PEER_TOPK = 16
HGRN_CHUNK = 128
HGRN_HEAD_GROUP = 4
EPS = 1e-6
MASKED = -1e30
SQRT_HALF = 0.7071067811865476
LOG2_E = 1.4426950408889634
ATTN_LANE_GROUP = 256
V7X_VMEM_LIMIT_BYTES = 52 * 1024 * 1024

_NT = (((1,), (1,)), ((), ()))


def _params(semantics):
    return pltpu.CompilerParams(dimension_semantics=semantics, vmem_limit_bytes=V7X_VMEM_LIMIT_BYTES)


def _tile(n, t):
    if n <= t:
        return n
    t -= t % HEAD_DIM
    while n % t:
        t -= HEAD_DIM
    assert t > 0, n
    return t


def _split3(x):
    hi = x.astype(BF16)
    r1 = x - hi.astype(F32)
    mid = r1.astype(BF16)
    lo = (r1 - mid.astype(F32)).astype(BF16)
    return hi, mid, lo


def _ada_body(c_ref, w_ref, b_ref, o_ref, acc_ref):
    k = pl.program_id(1)

    @pl.when(k == 0)
    def _():
        acc_ref[...] = jnp.zeros_like(acc_ref)

    c = c_ref[...]
    c_act = c * jax.nn.sigmoid(c)
    acc_ref[...] += jnp.sum(w_ref[...] * c_act, axis=0, keepdims=True)

    @pl.when(k == pl.num_programs(1) - 1)
    def _():
        o_ref[...] = acc_ref[...] + b_ref[...]


def _ada_mod(c_col, w, b_row):
    d, n = w.shape
    tk, tn = _tile(d, 512), _tile(n, 2048)
    return pl.pallas_call(
        _ada_body,
        grid=(n // tn, d // tk),
        in_specs=[pl.BlockSpec((tk, 1), lambda j, k: (k, 0)),
                  pl.BlockSpec((tk, tn), lambda j, k: (k, j)),
                  pl.BlockSpec((1, tn), lambda j, k: (0, j))],
        out_specs=pl.BlockSpec((1, tn), lambda j, k: (0, j)),
        out_shape=jax.ShapeDtypeStruct((1, n), F32),
        scratch_shapes=[pltpu.VMEM((1, tn), F32)],
        compiler_params=_params(("parallel", "arbitrary")),
        name="ada_mod",
    )(c_col, w, b_row)


def _norm_mod_body(x_ref, g_ref, sc_ref, sh_ref, o_ref, *maybe_ot_ref):
    x = x_ref[...]
    y = x * lax.rsqrt(jnp.mean(x * x, axis=-1, keepdims=True) + EPS)
    h = y * g_ref[...] * (1.0 + sc_ref[...]) + sh_ref[...]
    o_ref[...] = h.astype(o_ref.dtype)
    for ot_ref in maybe_ot_ref:
        ot_ref[...] = h.T.astype(ot_ref.dtype)


def _norm_mod(x, gain, scale, shift, with_transposed=False):
    s, d = x.shape
    tm = _tile(s, 256)
    row = pl.BlockSpec((1, d), lambda i: (0, 0))
    out_specs = [pl.BlockSpec((tm, d), lambda i: (i, 0))]
    out_shape = [jax.ShapeDtypeStruct((s, d), BF16)]
    if with_transposed:
        out_specs.append(pl.BlockSpec((d, tm), lambda i: (0, i)))
        out_shape.append(jax.ShapeDtypeStruct((d, s), BF16))
    outs = pl.pallas_call(
        _norm_mod_body,
        grid=(s // tm,),
        in_specs=[pl.BlockSpec((tm, d), lambda i: (i, 0)), row, row, row],
        out_specs=out_specs,
        out_shape=out_shape,
        compiler_params=_params(("parallel",)),
        name="norm_mod",
    )(x, gain, scale, shift)
    return outs if with_transposed else outs[0]


def _fox_proj_body(h_ref, w_ref, g_ref, o_ref):
    y = jnp.dot(h_ref[...], w_ref[...].astype(BF16), preferred_element_type=F32)
    g = g_ref[...]
    for hh in range(y.shape[1] // HEAD_DIM):
        sl = slice(hh * HEAD_DIM, (hh + 1) * HEAD_DIM)
        yh = y[:, sl]
        r = lax.rsqrt(jnp.mean(yh * yh, axis=-1, keepdims=True) + EPS)
        o_ref[:, sl] = (yh * r * g[:, sl]).astype(o_ref.dtype)


def _fox_proj(h, w_in, gains):
    s, d = h.shape
    n = gains.shape[1]
    tm, tn = _tile(s, 1024), _tile(n, 512)
    return pl.pallas_call(
        _fox_proj_body,
        grid=(s // tm, n // tn),
        in_specs=[pl.BlockSpec((tm, d), lambda i, j: (i, 0)),
                  pl.BlockSpec((d, tn), lambda i, j: (0, j)),
                  pl.BlockSpec((1, tn), lambda i, j: (0, j))],
        out_specs=pl.BlockSpec((tm, tn), lambda i, j: (i, j)),
        out_shape=jax.ShapeDtypeStruct((s, n), BF16),
        compiler_params=_params(("parallel", "arbitrary")),
        name="fox_proj",
    )(h, w_in, gains)


def _fox_v_proj_body(h_ref, w_ref, o_ref, y_ref):
    y_ref[...] = jnp.dot(h_ref[...], w_ref[...].astype(BF16), preferred_element_type=F32)
    o_ref[...] = y_ref[...].T.astype(o_ref.dtype)


def _fox_v_proj(h, w_in, first_col, n):
    s, d = h.shape
    tm, tn = _tile(s, 1024), _tile(n, 512)
    assert first_col % tn == 0
    first_block = first_col // tn
    return pl.pallas_call(
        _fox_v_proj_body,
        grid=(s // tm, n // tn),
        in_specs=[pl.BlockSpec((tm, d), lambda i, j: (i, 0)),
                  pl.BlockSpec((d, tn), lambda i, j: (0, first_block + j))],
        out_specs=pl.BlockSpec((tn, tm), lambda i, j: (j, i)),
        out_shape=jax.ShapeDtypeStruct((n, s), BF16),
        scratch_shapes=[pltpu.VMEM((tm, tn), F32)],
        compiler_params=_params(("parallel", "arbitrary")),
        name="fox_v_proj",
    )(h, w_in)


def _matmul_body(a_ref, b_ref, o_ref):
    o_ref[...] = jnp.dot(a_ref[...], b_ref[...], preferred_element_type=F32).astype(o_ref.dtype)


def _matmul(a, b, out_dtype):
    s, k = a.shape
    n = b.shape[1]
    tm, tn = _tile(s, 1024), _tile(n, 512)
    return pl.pallas_call(
        _matmul_body,
        grid=(s // tm, n // tn),
        in_specs=[pl.BlockSpec((tm, k), lambda i, j: (i, 0)),
                  pl.BlockSpec((k, tn), lambda i, j: (0, j))],
        out_specs=pl.BlockSpec((tm, tn), lambda i, j: (i, j)),
        out_shape=jax.ShapeDtypeStruct((s, n), out_dtype),
        compiler_params=_params(("parallel", "arbitrary")),
        name="hgrn_proj",
    )(a, b)


def _fox_gate_body(h_ref, w_ref, b_ref, o_ref, carry_ref):
    i = pl.program_id(0)

    @pl.when(i == 0)
    def _():
        carry_ref[...] = jnp.zeros_like(carry_ref)

    logit = jnp.dot(h_ref[...], w_ref[...], preferred_element_type=F32) + b_ref[...]
    log_f = jnp.minimum(logit, 0.0) - jnp.log1p(jnp.exp(-jnp.abs(logit)))
    ts = log_f.shape[0]
    lower = (lax.broadcasted_iota(jnp.int32, (ts, ts), 1)
             <= lax.broadcasted_iota(jnp.int32, (ts, ts), 0)).astype(BF16)
    csum = sum(jnp.dot(lower, part, preferred_element_type=F32) for part in _split3(log_f))
    csum = csum + carry_ref[...]
    carry_ref[...] = csum[ts - 1:ts, :]
    scaled = csum * LOG2_E
    for hd in range(o_ref.shape[0]):
        o_ref[hd] = jnp.broadcast_to(scaled[:, hd:hd + 1], (ts, HEAD_DIM))


def _fox_gate(h, w_pad, bias_pad, n_heads):
    s, d = h.shape
    ts = _tile(s, 512)
    return pl.pallas_call(
        _fox_gate_body,
        grid=(s // ts,),
        in_specs=[pl.BlockSpec((ts, d), lambda i: (i, 0)),
                  pl.BlockSpec((d, HEAD_DIM), lambda i: (0, 0)),
                  pl.BlockSpec((1, HEAD_DIM), lambda i: (0, 0))],
        out_specs=pl.BlockSpec((n_heads, ts, HEAD_DIM), lambda i: (0, i, 0)),
        out_shape=jax.ShapeDtypeStruct((n_heads, s, HEAD_DIM), F32),
        scratch_shapes=[pltpu.VMEM((1, HEAD_DIM), F32)],
        compiler_params=_params(("arbitrary",)),
        name="fox_gate",
    )(h, w_pad, bias_pad)


def _fox_attn_body(q_ref, k_ref, vt_ref, cb_ref, o_ref, sa_ref, sb_ref, m_ref, l_ref, acc_ref, *, t):
    i = pl.program_id(1)
    q = q_ref[...]
    q_start = pl.multiple_of(i * t, t)
    c_first = cb_ref[pl.ds(q_start, 8), :][0:1]
    m_ref[...] = jnp.full_like(m_ref, MASKED)
    l_ref[...] = jnp.zeros_like(l_ref)
    acc_ref[...] = jnp.zeros_like(acc_ref)
    group = min(t, ATTN_LANE_GROUP)

    def scores(j):
        k_start = pl.multiple_of(j * t, t)
        return lax.dot_general(k_ref[pl.ds(k_start, t), :], q, _NT, preferred_element_type=F32)

    def consume(j, sc_ref, masked):
        k_start = pl.multiple_of(j * t, t)
        bias = c_first - cb_ref[pl.ds(k_start, t), :]
        bias = jnp.concatenate([bias] * (group // HEAD_DIM), axis=1)
        v_t = vt_ref[:, pl.ds(k_start, t)]
        for g in range(t // group):
            lanes = slice(g * group, (g + 1) * group)
            s = sc_ref[:, lanes] + bias
            if masked:
                k_pos = lax.broadcasted_iota(jnp.int32, (t, group), 0)
                q_pos = lax.broadcasted_iota(jnp.int32, (t, group), 1) + g * group
                s = jnp.where(k_pos <= q_pos, s, MASKED)
            m_old = m_ref[:, lanes]
            m_new = jnp.maximum(m_old, jnp.max(s, axis=0, keepdims=True))
            alpha = jnp.exp2(m_old - m_new)
            p = jnp.exp2(s - m_new)
            l_ref[:, lanes] = alpha * l_ref[:, lanes] + jnp.sum(p, axis=0, keepdims=True)
            acc_ref[:, lanes] = alpha * acc_ref[:, lanes] + jnp.dot(
                v_t, p.astype(BF16), preferred_element_type=F32)
            m_ref[:, lanes] = m_new

    sa_ref[...] = scores(0)

    def pair(jj, carry):
        j = 2 * jj
        sb_ref[...] = scores(j + 1)
        consume(j, sa_ref, masked=False)
        sa_ref[...] = scores(j + 2)
        consume(j + 1, sb_ref, masked=False)
        return carry

    lax.fori_loop(0, lax.shift_right_logical(i, 1), pair, 0)
    odd = (i & 1) == 1

    @pl.when(jnp.logical_not(odd))
    def _():
        consume(i, sa_ref, masked=True)

    @pl.when(odd)
    def _():
        sb_ref[...] = scores(i)
        consume(i - 1, sa_ref, masked=False)
        consume(i, sb_ref, masked=True)

    o_ref[...] = (acc_ref[...] / l_ref[...]).T.astype(o_ref.dtype)


def _fox_attention(qk, v_t, cum_f, n_heads):
    s = qk.shape[0]
    t = _tile(s, 512)
    return pl.pallas_call(
        functools.partial(_fox_attn_body, t=t),
        grid=(n_heads, s // t),
        in_specs=[pl.BlockSpec((t, HEAD_DIM), lambda h, i: (i, h)),
                  pl.BlockSpec((s, HEAD_DIM), lambda h, i: (0, n_heads + h)),
                  pl.BlockSpec((HEAD_DIM, s), lambda h, i: (h, 0)),
                  pl.BlockSpec((None, s, HEAD_DIM), lambda h, i: (h, 0, 0))],
        out_specs=pl.BlockSpec((t, HEAD_DIM), lambda h, i: (i, h)),
        out_shape=jax.ShapeDtypeStruct((s, n_heads * HEAD_DIM), BF16),
        scratch_shapes=[pltpu.VMEM((t, t), F32), pltpu.VMEM((t, t), F32),
                        pltpu.VMEM((1, t), F32), pltpu.VMEM((1, t), F32),
                        pltpu.VMEM((HEAD_DIM, t), F32)],
        compiler_params=_params(("parallel", "arbitrary")),
        name="fox_attention",
    )(qk, qk, v_t, cum_f)


def _hgrn_head(hq, hf, val, hg, lb, gain, state_t):
    forget = lb + (1.0 - lb) * jax.nn.sigmoid(hf)
    log_f = jnp.log(forget)
    key = 1.0 - forget
    query = hq * jax.nn.sigmoid(hq) * (HEAD_DIM ** -0.5)
    val_b = val.astype(BF16)
    c = log_f.shape[0]

    row = lax.broadcasted_iota(jnp.int32, (c, c), 0)
    col = lax.broadcasted_iota(jnp.int32, (c, c), 1)
    lower = (col <= row).astype(BF16)
    b = sum(jnp.dot(lower, part, preferred_element_type=F32) for part in _split3(log_f))
    b_last = b[c - 1:c, :]

    out = lax.dot_general((query * jnp.exp(b)).astype(BF16), state_t.astype(BF16), _NT,
                          preferred_element_type=F32)

    t_idx = lax.broadcasted_iota(jnp.int32, (c, HEAD_DIM), 0)
    sep = row ^ col
    scores = jnp.where(sep == 0, lax.dot_general(query.astype(BF16), key.astype(BF16), _NT,
                                                 preferred_element_type=F32), 0.0)
    half = c // 2
    while half >= 1:
        blk = 2 * half
        if half >= 4:
            ref_rows = jnp.broadcast_to(b.reshape(c // blk, blk, HEAD_DIM)[:, half - 1:half, :],
                                        (c // blk, blk, HEAD_DIM)).reshape(c, HEAD_DIM)
        elif half == 2:
            pos = t_idx & 3
            ref_rows = jnp.where(pos == 0, pltpu.roll(b, c - 1, 0),
                                 jnp.where(pos == 2, pltpu.roll(b, 1, 0),
                                           jnp.where(pos == 3, pltpu.roll(b, 2, 0), b)))
        else:
            ref_rows = jnp.where((t_idx & 1) == 1, pltpu.roll(b, 1, 0), b)
        in_upper = (t_idx & half) != 0
        x = jnp.where(in_upper, query, key) * jnp.exp(-jnp.abs(b - ref_rows))
        a_side = jnp.where(in_upper, x, 0.0).astype(BF16)
        b_side = jnp.where(in_upper, 0.0, x).astype(BF16)
        level = lax.dot_general(a_side, b_side, _NT, preferred_element_type=F32)
        scores = scores + (level if blk == c else jnp.where(sep < blk, level, 0.0))
        half //= 2
    out = out + jnp.dot(scores.astype(BF16), val_b, preferred_element_type=F32)

    key_dec = (key * jnp.exp(b_last - b)).astype(BF16)
    new_state_t = state_t * jnp.exp(b_last) + jnp.dot(val.T.astype(BF16), key_dec,
                                                      preferred_element_type=F32)

    normed = out * lax.rsqrt(jnp.mean(out * out, axis=-1, keepdims=True) + EPS) * gain
    return normed * (hg * jax.nn.sigmoid(hg)), new_state_t


def _hgrn_body(q_ref, f_ref, i_ref, g_ref, lb_ref, gain_ref, o_ref, state_ref, *, layer):
    @pl.when(pl.program_id(1) == 0)
    def _():
        state_ref[...] = jnp.zeros_like(state_ref)

    for hh in range(state_ref.shape[0]):
        sl = slice(hh * HEAD_DIM, (hh + 1) * HEAD_DIM)
        table = lb_ref[:, sl]
        e = jnp.exp(table - jnp.max(table, axis=0, keepdims=True))
        lb = jnp.sum(e[:layer + 1], axis=0, keepdims=True) / jnp.sum(e, axis=0, keepdims=True)
        out, state_ref[hh] = _hgrn_head(q_ref[:, sl], f_ref[:, sl], i_ref[:, sl], g_ref[:, sl],
                                        lb, gain_ref[:, sl], state_ref[hh])
        o_ref[:, sl] = out.astype(o_ref.dtype)


def _hgrn(proj, lb_table, out_gain, n_heads, layer):
    s = proj.shape[0]
    c = _tile(s, HGRN_CHUNK)
    n_slots = lb_table.shape[0]
    group = HGRN_HEAD_GROUP if n_heads % HGRN_HEAD_GROUP == 0 else 1
    n_groups = n_heads // group
    width = group * HEAD_DIM

    def col(block):
        return pl.BlockSpec((c, width), lambda g, t: (t, block * n_groups + g))

    return pl.pallas_call(
        functools.partial(_hgrn_body, layer=layer),
        grid=(n_groups, s // c),
        in_specs=[col(0), col(1), col(2), col(3),
                  pl.BlockSpec((n_slots, width), lambda g, t: (0, g)),
                  pl.BlockSpec((1, width), lambda g, t: (0, g))],
        out_specs=pl.BlockSpec((c, width), lambda g, t: (t, g)),
        out_shape=jax.ShapeDtypeStruct((s, n_heads * HEAD_DIM), BF16),
        scratch_shapes=[pltpu.VMEM((group, HEAD_DIM, HEAD_DIM), F32)],
        compiler_params=_params(("parallel", "arbitrary")),
        name="hgrn2",
    )(proj, proj, proj, proj, lb_table, out_gain)


def _out_proj_body(a_ref, b_ref, wa_ref, wb_ref, x_ref, g_ref, o_ref):
    mix = (jnp.dot(a_ref[...], wa_ref[...], preferred_element_type=F32)
           + jnp.dot(b_ref[...], wb_ref[...], preferred_element_type=F32))
    o_ref[...] = x_ref[...] + g_ref[...] * mix


def _out_proj(fox, hgrn, w_fox, w_hgrn, x, gate):
    s, d = x.shape
    tm, tn = _tile(s, 1024), _tile(d, 512)
    ka, kb = fox.shape[1], hgrn.shape[1]
    return pl.pallas_call(
        _out_proj_body,
        grid=(s // tm, d // tn),
        in_specs=[pl.BlockSpec((tm, ka), lambda i, j: (i, 0)),
                  pl.BlockSpec((tm, kb), lambda i, j: (i, 0)),
                  pl.BlockSpec((ka, tn), lambda i, j: (0, j)),
                  pl.BlockSpec((kb, tn), lambda i, j: (0, j)),
                  pl.BlockSpec((tm, tn), lambda i, j: (i, j)),
                  pl.BlockSpec((1, tn), lambda i, j: (0, j))],
        out_specs=pl.BlockSpec((tm, tn), lambda i, j: (i, j)),
        out_shape=jax.ShapeDtypeStruct((s, d), F32),
        compiler_params=_params(("parallel", "arbitrary")),
        name="out_proj",
    )(fox, hgrn, w_fox, w_hgrn, x, gate)


PEER_NTOP = PEER_TOPK + 1
PEER_TOP_ROWS = 24
PEER_NCAND = 52


def _top_rows(arrays, k, exact_ties):
    vals = list(arrays)
    rows = [[] for _ in vals]
    for _ in range(k):
        for n, v in enumerate(vals):
            m = jnp.max(v, axis=0, keepdims=True)
            rows[n].append(m)
            hit = v == m
            if exact_ties:
                idx = lax.broadcasted_iota(jnp.int32, v.shape, 0)
                first = jnp.min(jnp.where(hit, idx, v.shape[0]), axis=0, keepdims=True)
                hit = idx == first
            vals[n] = jnp.where(hit, -jnp.inf, v)
    return rows, vals


def _n_finite(v):
    return jnp.sum(jnp.where(v > -jnp.inf, 1.0, 0.0), axis=0, keepdims=True)


def _peer_thresholds(s0s, s1s, a_ref, b_ref, exact_ties):
    nh = len(s0s)
    rows, left = _top_rows(s0s + s1s, PEER_NTOP, exact_ties)
    excess = sum(jnp.abs(_n_finite(v) - (PEER_KEYS - PEER_NTOP)) for v in left)

    r24 = lax.broadcasted_iota(jnp.int32, a_ref.shape[1:], 0)
    r8 = lax.broadcasted_iota(jnp.int32, (8, a_ref.shape[2]), 0)
    ninf = -jnp.inf

    def grp(x, r, lo, hi):
        return jnp.where((r >= lo) & (r <= hi), x, ninf)

    cands = []
    for hd in range(nh):
        a_ref[hd] = jnp.full(a_ref.shape[1:], ninf, F32)
        b_ref[hd] = jnp.full(b_ref.shape[1:], ninf, F32)
        for r in range(PEER_NTOP):
            a_ref[hd, r:r + 1, :] = rows[hd][r]
            b_ref[hd, r:r + 1, :] = rows[nh + hd][r]
        a24, b24 = a_ref[hd], b_ref[hd]
        a8, b8 = a24[:8], b24[:8]
        cands.append(jnp.concatenate([
            grp(a24[0:1] + b24, r24, 0, 16),
            grp(a24 + b24[0:1], r24, 1, 16),
            grp(a8[1:2] + b8, r8, 1, 7),
            grp(a8 + b8[1:2], r8, 2, 7),
            grp(a8[2:3] + b8, r8, 2, 4),
            grp(a8 + b8[2:3], r8, 3, 4),
            grp(a8[3:4] + b8, r8, 3, 3),
        ], axis=0))
    tops, left = _top_rows(cands, PEER_NTOP, exact_ties)
    excess = excess + sum(jnp.abs(_n_finite(v) - (PEER_NCAND - PEER_NTOP)) for v in left)

    outs = []
    for hd in range(nh):
        top = tops[hd]
        tau = 0.5 * (top[PEER_TOPK - 1] + top[PEER_TOPK])
        z = sum(jnp.exp(t - top[0]) for t in top[:PEER_TOPK])
        outs.append((tau - s0s[hd],
                     jnp.exp(s0s[hd] - rows[hd][0]) / z,
                     jnp.exp(s1s[hd] - rows[nh + hd][0])))
    return outs, excess


def _peer_route_body(h_ref, wq_ref, sk_ref, thr_ref, e0_ref, s1_ref, e1_ref, a_ref, b_ref):
    nh = sk_ref.shape[0]
    q = jnp.dot(h_ref[...], wq_ref[...], preferred_element_type=F32).astype(BF16)
    s0s, s1s = [], []
    for hd in range(nh):
        c0 = 2 * hd * PEER_KEYS
        s0s.append(lax.dot_general(sk_ref[hd, 0], q[:, c0:c0 + PEER_KEYS], _NT,
                                   preferred_element_type=F32))
        s1s.append(lax.dot_general(sk_ref[hd, 1], q[:, c0 + PEER_KEYS:c0 + 2 * PEER_KEYS], _NT,
                                   preferred_element_type=F32))

    def emit(outs):
        for hd, (thr, e0, e1) in enumerate(outs):
            thr_ref[hd] = thr
            e0_ref[hd] = e0
            s1_ref[hd] = s1s[hd]
            e1_ref[hd] = e1

    outs, excess = _peer_thresholds(s0s, s1s, a_ref, b_ref, exact_ties=False)
    emit(outs)

    @pl.when(jnp.max(excess) > 0.0)
    def _():
        emit(_peer_thresholds(s0s, s1s, a_ref, b_ref, exact_ties=True)[0])


def _peer_route(h, wq, sub_keys):
    s, d = h.shape
    ph = sub_keys.shape[0]
    tt = _tile(s, 256)
    nh = 2 if ph % 2 == 0 else 1
    out_spec = pl.BlockSpec((nh, PEER_KEYS, tt), lambda t, p: (p, 0, t))
    out_sds = jax.ShapeDtypeStruct((ph, PEER_KEYS, s), F32)
    return pl.pallas_call(
        _peer_route_body,
        grid=(s // tt, ph // nh),
        in_specs=[pl.BlockSpec((tt, d), lambda t, p: (t, 0)),
                  pl.BlockSpec((d, nh * 2 * PEER_KEYS), lambda t, p: (0, p)),
                  pl.BlockSpec((nh, 2, PEER_KEYS, PEER_KEYS), lambda t, p: (p, 0, 0, 0))],
        out_specs=[out_spec] * 4,
        out_shape=[out_sds] * 4,
        scratch_shapes=[pltpu.VMEM((nh, PEER_TOP_ROWS, tt), F32), pltpu.VMEM((nh, PEER_TOP_ROWS, tt), F32)],
        compiler_params=_params(("parallel", "arbitrary")),
        name="peer_route",
    )(h, wq, sub_keys)


def _peer_act_body(u_ref, ht_ref, thr_ref, e0_ref, s1_ref, e1_ref, w_ref, *, rows):
    n_heads = s1_ref.shape[0]
    te = u_ref.shape[0]
    for ch in range(te // rows):
        z = jnp.dot(u_ref[ch * rows:(ch + 1) * rows, :], ht_ref[...], preferred_element_type=F32)
        for ci in range(rows // PEER_KEYS):
            ii = ch * (rows // PEER_KEYS) + ci
            zi = z[ci * PEER_KEYS:(ci + 1) * PEER_KEYS, :]
            act = 0.5 * zi * (1.0 + lax.erf(zi * SQRT_HALF))
            gate = jnp.zeros_like(zi)
            for hd in range(n_heads):
                sel = s1_ref[hd] >= thr_ref[hd, ii:ii + 1, :]
                gate = gate + jnp.where(sel, e1_ref[hd] * e0_ref[hd, ii:ii + 1, :], 0.0)
            w_ref[:, ii * PEER_KEYS:(ii + 1) * PEER_KEYS] = (act * gate).T.astype(w_ref.dtype)


def _peer_act(u, h_t, thr, e0, s1, e1):
    n_exp, d = u.shape
    s = h_t.shape[1]
    ph = thr.shape[0]
    te = 8 * PEER_KEYS
    tt = _tile(s, 512)
    assert n_exp % te == 0
    per_i = pl.BlockSpec((ph, 8, tt), lambda t, e: (0, e, t))
    per_j = pl.BlockSpec((ph, PEER_KEYS, tt), lambda t, e: (0, 0, t))
    return pl.pallas_call(
        functools.partial(_peer_act_body, rows=te // 2),
        grid=(s // tt, n_exp // te),
        in_specs=[pl.BlockSpec((te, d), lambda t, e: (e, 0)),
                  pl.BlockSpec((d, tt), lambda t, e: (0, t)),
                  per_i, per_i, per_j, per_j],
        out_specs=pl.BlockSpec((tt, te), lambda t, e: (t, e)),
        out_shape=jax.ShapeDtypeStruct((s, n_exp), BF16),
        compiler_params=_params(("parallel", "arbitrary")),
        name="peer_act",
    )(u, h_t, thr, e0, s1, e1)


def _peer_down_body(w_ref, v_ref, x_ref, g_ref, o_ref, acc_ref):
    k = pl.program_id(2)

    @pl.when(k == 0)
    def _():
        acc_ref[...] = jnp.zeros_like(acc_ref)

    acc_ref[...] += jnp.dot(w_ref[...], v_ref[...], preferred_element_type=F32)

    @pl.when(k == pl.num_programs(2) - 1)
    def _():
        o_ref[...] = x_ref[...] + g_ref[...] * acc_ref[...]


def _peer_down(w, v, x, gate):
    s, n_exp = w.shape
    d = v.shape[1]
    tm, tn, tk = _tile(s, 1024), _tile(d, 1024), _tile(n_exp, 2048)
    return pl.pallas_call(
        _peer_down_body,
        grid=(s // tm, d // tn, n_exp // tk),
        in_specs=[pl.BlockSpec((tm, tk), lambda i, j, k: (i, k)),
                  pl.BlockSpec((tk, tn), lambda i, j, k: (k, j)),
                  pl.BlockSpec((tm, tn), lambda i, j, k: (i, j)),
                  pl.BlockSpec((1, tn), lambda i, j, k: (0, j))],
        out_specs=pl.BlockSpec((tm, tn), lambda i, j, k: (i, j)),
        out_shape=jax.ShapeDtypeStruct((s, d), F32),
        scratch_shapes=[pltpu.VMEM((tm, tn), F32)],
        compiler_params=_params(("parallel", "parallel", "arbitrary")),
        name="peer_down",
    )(w, v, x, gate)


def kernel(x, c, ada_w, ada_b, norm1_gain, norm2_gain, w_in, fox_f_bias, fox_q_gain, fox_k_gain,
           hgrn_lower_bounds, hgrn_out_gain, w_out, peer_w_query, peer_sub_keys, peer_u, peer_v):
    batch, seq, d = x.shape
    assert batch == 1, "one sequence per call"
    depth = ada_w.shape[0]
    fox_heads = fox_f_bias.shape[-1]
    hgrn_heads = hgrn_out_gain.shape[1]
    fox_width, hgrn_width = fox_heads * HEAD_DIM, hgrn_heads * HEAD_DIM
    assert fox_q_gain.shape[-1] == HEAD_DIM and hgrn_out_gain.shape[-1] == HEAD_DIM
    assert w_in.shape[-1] == 3 * fox_width + fox_heads + 4 * hgrn_width
    assert peer_sub_keys.shape[2:] == (2, PEER_KEYS, PEER_KEYS) and peer_u.shape[1] == PEER_KEYS ** 2
    o3 = 3 * fox_width
    o4 = o3 + fox_heads

    xs = x.reshape(seq, d)
    c_col = c.reshape(d, 1)
    for layer in range(depth):
        mod = _ada_mod(c_col, ada_w[layer], ada_b[layer].reshape(1, -1))
        shift1, scale1, gate1, shift2, scale2, gate2 = (mod[:, n * d:(n + 1) * d] for n in range(6))

        h = _norm_mod(xs, norm1_gain[layer].reshape(1, d), scale1, shift1)
        w_l = w_in[layer]
        qk_gains = jnp.concatenate([fox_q_gain[layer].reshape(1, fox_width) * (HEAD_DIM ** -0.5 * LOG2_E),
                                    fox_k_gain[layer].reshape(1, fox_width)], axis=1)
        qk = _fox_proj(h, w_l, qk_gains)
        v_t = _fox_v_proj(h, w_l, 2 * fox_width, fox_width)
        pad = HEAD_DIM - fox_heads
        cum_f = _fox_gate(h, jnp.pad(w_l[:, o3:o4], ((0, 0), (0, pad))).astype(BF16),
                          jnp.pad(fox_f_bias[layer].reshape(1, fox_heads), ((0, 0), (0, pad))), fox_heads)
        fox_out = _fox_attention(qk, v_t, cum_f, fox_heads)

        hproj = _matmul(h, w_l[:, o4:].astype(BF16), F32)
        hgrn_out = _hgrn(hproj, hgrn_lower_bounds, hgrn_out_gain[layer].reshape(1, hgrn_width),
                         hgrn_heads, layer)

        w_o = w_out[layer].astype(BF16)
        xs = _out_proj(fox_out, hgrn_out, w_o[:fox_width], w_o[fox_width:], xs, gate1)

        h, h_t = _norm_mod(xs, norm2_gain[layer].reshape(1, d), scale2, shift2, with_transposed=True)
        thr, e0, s1, e1 = _peer_route(h, peer_w_query[layer].astype(BF16),
                                      peer_sub_keys[layer].astype(BF16))
        w_act = _peer_act(peer_u[layer].astype(BF16), h_t, thr, e0, s1, e1)
        xs = _peer_down(w_act, peer_v[layer].astype(BF16), xs, gate2)
    return xs.reshape(batch, seq, d)
```

```python
import functools

import jax
import jax.numpy as jnp
from jax import lax
from jax.experimental import pallas as pl
from jax.experimental.pallas import tpu as pltpu

F32 = jnp.float32
BF16 = jnp.bfloat16

HEAD_DIM = 128
PEER_KEYS = 128
PEER_TOPK = 16
HGRN_CHUNK = 128
HGRN_HEAD_GROUP = 4
EPS = 1e-6
MASKED = -1e30
SQRT_HALF = 0.7071067811865476
LOG2_E = 1.4426950408889634
ATTN_LANE_GROUP = 256
V7X_VMEM_LIMIT_BYTES = 52 * 1024 * 1024

_NT = (((1,), (1,)), ((), ()))


def _params(semantics):
    return pltpu.CompilerParams(dimension_semantics=semantics, vmem_limit_bytes=V7X_VMEM_LIMIT_BYTES)


def _tile(n, t):
    if n <= t:
        return n
    t -= t % HEAD_DIM
    while n % t:
        t -= HEAD_DIM
    assert t > 0, n
    return t


def _split3(x):
    hi = x.astype(BF16)
    r1 = x - hi.astype(F32)
    mid = r1.astype(BF16)
    lo = (r1 - mid.astype(F32)).astype(BF16)
    return hi, mid, lo


def _ada_body(c_ref, w_ref, b_ref, o_ref, acc_ref):
    k = pl.program_id(1)

    @pl.when(k == 0)
    def _():
        acc_ref[...] = jnp.zeros_like(acc_ref)

    c = c_ref[...]
    c_act = c * jax.nn.sigmoid(c)
    acc_ref[...] += jnp.sum(w_ref[...] * c_act, axis=0, keepdims=True)

    @pl.when(k == pl.num_programs(1) - 1)
    def _():
        o_ref[...] = acc_ref[...] + b_ref[...]


def _ada_mod(c_col, w, b_row):
    d, n = w.shape
    tk, tn = _tile(d, 512), _tile(n, 2048)
    return pl.pallas_call(
        _ada_body,
        grid=(n // tn, d // tk),
        in_specs=[pl.BlockSpec((tk, 1), lambda j, k: (k, 0)),
                  pl.BlockSpec((tk, tn), lambda j, k: (k, j)),
                  pl.BlockSpec((1, tn), lambda j, k: (0, j))],
        out_specs=pl.BlockSpec((1, tn), lambda j, k: (0, j)),
        out_shape=jax.ShapeDtypeStruct((1, n), F32),
        scratch_shapes=[pltpu.VMEM((1, tn), F32)],
        compiler_params=_params(("parallel", "arbitrary")),
        name="ada_mod",
    )(c_col, w, b_row)


def _norm_mod_body(x_ref, g_ref, sc_ref, sh_ref, o_ref, *maybe_ot_ref):
    x = x_ref[...]
    y = x * lax.rsqrt(jnp.mean(x * x, axis=-1, keepdims=True) + EPS)
    h = y * g_ref[...] * (1.0 + sc_ref[...]) + sh_ref[...]
    o_ref[...] = h.astype(o_ref.dtype)
    for ot_ref in maybe_ot_ref:
        ot_ref[...] = h.T.astype(ot_ref.dtype)


def _norm_mod(x, gain, scale, shift, with_transposed=False):
    s, d = x.shape
    tm = _tile(s, 256)
    row = pl.BlockSpec((1, d), lambda i: (0, 0))
    out_specs = [pl.BlockSpec((tm, d), lambda i: (i, 0))]
    out_shape = [jax.ShapeDtypeStruct((s, d), BF16)]
    if with_transposed:
        out_specs.append(pl.BlockSpec((d, tm), lambda i: (0, i)))
        out_shape.append(jax.ShapeDtypeStruct((d, s), BF16))
    outs = pl.pallas_call(
        _norm_mod_body,
        grid=(s // tm,),
        in_specs=[pl.BlockSpec((tm, d), lambda i: (i, 0)), row, row, row],
        out_specs=out_specs,
        out_shape=out_shape,
        compiler_params=_params(("parallel",)),
        name="norm_mod",
    )(x, gain, scale, shift)
    return outs if with_transposed else outs[0]


def _fox_proj_body(h_ref, w_ref, g_ref, o_ref):
    y = jnp.dot(h_ref[...], w_ref[...].astype(BF16), preferred_element_type=F32)
    g = g_ref[...]
    for hh in range(y.shape[1] // HEAD_DIM):
        sl = slice(hh * HEAD_DIM, (hh + 1) * HEAD_DIM)
        yh = y[:, sl]
        r = lax.rsqrt(jnp.mean(yh * yh, axis=-1, keepdims=True) + EPS)
        o_ref[:, sl] = (yh * r * g[:, sl]).astype(o_ref.dtype)


def _fox_proj(h, w_in, layer, gains):
    s, d = h.shape
    n = gains.shape[1]
    tm, tn = _tile(s, 1024), _tile(n, 512)
    return pl.pallas_call(
        _fox_proj_body,
        grid=(s // tm, n // tn),
        in_specs=[pl.BlockSpec((tm, d), lambda i, j: (i, 0)),
                  pl.BlockSpec((None, d, tn), lambda i, j: (layer, 0, j)),
                  pl.BlockSpec((1, tn), lambda i, j: (0, j))],
        out_specs=pl.BlockSpec((tm, tn), lambda i, j: (i, j)),
        out_shape=jax.ShapeDtypeStruct((s, n), BF16),
        compiler_params=_params(("parallel", "arbitrary")),
        name="fox_proj",
    )(h, w_in, gains)


def _fox_v_proj_body(h_ref, w_ref, o_ref, y_ref):
    y_ref[...] = jnp.dot(h_ref[...], w_ref[...].astype(BF16), preferred_element_type=F32)
    o_ref[...] = y_ref[...].T.astype(o_ref.dtype)


def _fox_v_proj(h, w_in, layer, first_col, n):
    s, d = h.shape
    tm, tn = _tile(s, 1024), _tile(n, 512)
    assert first_col % tn == 0
    first_block = first_col // tn
    return pl.pallas_call(
        _fox_v_proj_body,
        grid=(s // tm, n // tn),
        in_specs=[pl.BlockSpec((tm, d), lambda i, j: (i, 0)),
                  pl.BlockSpec((None, d, tn), lambda i, j: (layer, 0, first_block + j))],
        out_specs=pl.BlockSpec((tn, tm), lambda i, j: (j, i)),
        out_shape=jax.ShapeDtypeStruct((n, s), BF16),
        scratch_shapes=[pltpu.VMEM((tm, tn), F32)],
        compiler_params=_params(("parallel", "arbitrary")),
        name="fox_v_proj",
    )(h, w_in)


def _matmul_body(a_ref, b_ref, o_ref):
    o_ref[...] = jnp.dot(a_ref[...], b_ref[...], preferred_element_type=F32).astype(o_ref.dtype)


def _matmul(a, b, out_dtype):
    s, k = a.shape
    n = b.shape[1]
    tm, tn = _tile(s, 1024), _tile(n, 512)
    return pl.pallas_call(
        _matmul_body,
        grid=(s // tm, n // tn),
        in_specs=[pl.BlockSpec((tm, k), lambda i, j: (i, 0)),
                  pl.BlockSpec((k, tn), lambda i, j: (0, j))],
        out_specs=pl.BlockSpec((tm, tn), lambda i, j: (i, j)),
        out_shape=jax.ShapeDtypeStruct((s, n), out_dtype),
        compiler_params=_params(("parallel", "arbitrary")),
        name="hgrn_proj",
    )(a, b)


def _fox_gate_body(h_ref, w_ref, b_ref, o_ref, carry_ref):
    i = pl.program_id(0)

    @pl.when(i == 0)
    def _():
        carry_ref[...] = jnp.zeros_like(carry_ref)

    logit = jnp.dot(h_ref[...], w_ref[...], preferred_element_type=F32) + b_ref[...]
    log_f = jnp.minimum(logit, 0.0) - jnp.log1p(jnp.exp(-jnp.abs(logit)))
    ts = log_f.shape[0]
    lower = (lax.broadcasted_iota(jnp.int32, (ts, ts), 1)
             <= lax.broadcasted_iota(jnp.int32, (ts, ts), 0)).astype(BF16)
    csum = sum(jnp.dot(lower, part, preferred_element_type=F32) for part in _split3(log_f))
    csum = csum + carry_ref[...]
    carry_ref[...] = csum[ts - 1:ts, :]
    scaled = csum * LOG2_E
    for hd in range(o_ref.shape[0]):
        o_ref[hd] = jnp.broadcast_to(scaled[:, hd:hd + 1], (ts, HEAD_DIM))


def _fox_gate(h, w_pad, bias_pad, n_heads):
    s, d = h.shape
    ts = _tile(s, 512)
    return pl.pallas_call(
        _fox_gate_body,
        grid=(s // ts,),
        in_specs=[pl.BlockSpec((ts, d), lambda i: (i, 0)),
                  pl.BlockSpec((d, HEAD_DIM), lambda i: (0, 0)),
                  pl.BlockSpec((1, HEAD_DIM), lambda i: (0, 0))],
        out_specs=pl.BlockSpec((n_heads, ts, HEAD_DIM), lambda i: (0, i, 0)),
        out_shape=jax.ShapeDtypeStruct((n_heads, s, HEAD_DIM), F32),
        scratch_shapes=[pltpu.VMEM((1, HEAD_DIM), F32)],
        compiler_params=_params(("arbitrary",)),
        name="fox_gate",
    )(h, w_pad, bias_pad)


def _fox_attn_body(q_ref, k_ref, vt_ref, cb_ref, o_ref, sa_ref, sb_ref, m_ref, l_ref, acc_ref, *, t):
    i = pl.program_id(1)
    q = q_ref[...]
    q_start = pl.multiple_of(i * t, t)
    c_first = cb_ref[pl.ds(q_start, 8), :][0:1]
    m_ref[...] = jnp.full_like(m_ref, MASKED)
    l_ref[...] = jnp.zeros_like(l_ref)
    acc_ref[...] = jnp.zeros_like(acc_ref)
    group = min(t, ATTN_LANE_GROUP)

    def scores(j):
        k_start = pl.multiple_of(j * t, t)
        return lax.dot_general(k_ref[pl.ds(k_start, t), :], q, _NT, preferred_element_type=F32)

    def consume(j, sc_ref, masked):
        k_start = pl.multiple_of(j * t, t)
        bias = c_first - cb_ref[pl.ds(k_start, t), :]
        bias = jnp.concatenate([bias] * (group // HEAD_DIM), axis=1)
        v_t = vt_ref[:, pl.ds(k_start, t)]
        for g in range(t // group):
            lanes = slice(g * group, (g + 1) * group)
            s = sc_ref[:, lanes] + bias
            if masked:
                k_pos = lax.broadcasted_iota(jnp.int32, (t, group), 0)
                q_pos = lax.broadcasted_iota(jnp.int32, (t, group), 1) + g * group
                s = jnp.where(k_pos <= q_pos, s, MASKED)
            m_old = m_ref[:, lanes]
            m_new = jnp.maximum(m_old, jnp.max(s, axis=0, keepdims=True))
            alpha = jnp.exp2(m_old - m_new)
            p = jnp.exp2(s - m_new)
            l_ref[:, lanes] = alpha * l_ref[:, lanes] + jnp.sum(p, axis=0, keepdims=True)
            acc_ref[:, lanes] = alpha * acc_ref[:, lanes] + jnp.dot(
                v_t, p.astype(BF16), preferred_element_type=F32)
            m_ref[:, lanes] = m_new

    sa_ref[...] = scores(0)

    def pair(jj, carry):
        j = 2 * jj
        sb_ref[...] = scores(j + 1)
        consume(j, sa_ref, masked=False)
        sa_ref[...] = scores(j + 2)
        consume(j + 1, sb_ref, masked=False)
        return carry

    lax.fori_loop(0, lax.shift_right_logical(i, 1), pair, 0)
    odd = (i & 1) == 1

    @pl.when(jnp.logical_not(odd))
    def _():
        consume(i, sa_ref, masked=True)

    @pl.when(odd)
    def _():
        sb_ref[...] = scores(i)
        consume(i - 1, sa_ref, masked=False)
        consume(i, sb_ref, masked=True)

    o_ref[...] = (acc_ref[...] / l_ref[...]).T.astype(o_ref.dtype)


def _fox_attention(qk, v_t, cum_f, n_heads):
    s = qk.shape[0]
    t = _tile(s, 512)
    return pl.pallas_call(
        functools.partial(_fox_attn_body, t=t),
        grid=(n_heads, s // t),
        in_specs=[pl.BlockSpec((t, HEAD_DIM), lambda h, i: (i, h)),
                  pl.BlockSpec((s, HEAD_DIM), lambda h, i: (0, n_heads + h)),
                  pl.BlockSpec((HEAD_DIM, s), lambda h, i: (h, 0)),
                  pl.BlockSpec((None, s, HEAD_DIM), lambda h, i: (h, 0, 0))],
        out_specs=pl.BlockSpec((t, HEAD_DIM), lambda h, i: (i, h)),
        out_shape=jax.ShapeDtypeStruct((s, n_heads * HEAD_DIM), BF16),
        scratch_shapes=[pltpu.VMEM((t, t), F32), pltpu.VMEM((t, t), F32),
                        pltpu.VMEM((1, t), F32), pltpu.VMEM((1, t), F32),
                        pltpu.VMEM((HEAD_DIM, t), F32)],
        compiler_params=_params(("parallel", "arbitrary")),
        name="fox_attention",
    )(qk, qk, v_t, cum_f)


def _hgrn_head(hq, hf, val, hg, lb, gain, state_t):
    forget = lb + (1.0 - lb) * jax.nn.sigmoid(hf)
    log_f = jnp.log(forget)
    key = 1.0 - forget
    query = hq * jax.nn.sigmoid(hq) * (HEAD_DIM ** -0.5)
    val_b = val.astype(BF16)
    c = log_f.shape[0]

    row = lax.broadcasted_iota(jnp.int32, (c, c), 0)
    col = lax.broadcasted_iota(jnp.int32, (c, c), 1)
    lower = (col <= row).astype(BF16)
    b = sum(jnp.dot(lower, part, preferred_element_type=F32) for part in _split3(log_f))
    b_last = b[c - 1:c, :]

    out = lax.dot_general((query * jnp.exp(b)).astype(BF16), state_t.astype(BF16), _NT,
                          preferred_element_type=F32)

    t_idx = lax.broadcasted_iota(jnp.int32, (c, HEAD_DIM), 0)
    sep = row ^ col
    scores = jnp.where(sep == 0, lax.dot_general(query.astype(BF16), key.astype(BF16), _NT,
                                                 preferred_element_type=F32), 0.0)
    half = c // 2
    while half >= 1:
        blk = 2 * half
        if half >= 4:
            ref_rows = jnp.broadcast_to(b.reshape(c // blk, blk, HEAD_DIM)[:, half - 1:half, :],
                                        (c // blk, blk, HEAD_DIM)).reshape(c, HEAD_DIM)
        elif half == 2:
            pos = t_idx & 3
            ref_rows = jnp.where(pos == 0, pltpu.roll(b, c - 1, 0),
                                 jnp.where(pos == 2, pltpu.roll(b, 1, 0),
                                           jnp.where(pos == 3, pltpu.roll(b, 2, 0), b)))
        else:
            ref_rows = jnp.where((t_idx & 1) == 1, pltpu.roll(b, 1, 0), b)
        in_upper = (t_idx & half) != 0
        x = jnp.where(in_upper, query, key) * jnp.exp(-jnp.abs(b - ref_rows))
        a_side = jnp.where(in_upper, x, 0.0).astype(BF16)
        b_side = jnp.where(in_upper, 0.0, x).astype(BF16)
        level = lax.dot_general(a_side, b_side, _NT, preferred_element_type=F32)
        scores = scores + (level if blk == c else jnp.where(sep < blk, level, 0.0))
        half //= 2
    out = out + jnp.dot(scores.astype(BF16), val_b, preferred_element_type=F32)

    key_dec = (key * jnp.exp(b_last - b)).astype(BF16)
    new_state_t = state_t * jnp.exp(b_last) + jnp.dot(val.T.astype(BF16), key_dec,
                                                      preferred_element_type=F32)

    normed = out * lax.rsqrt(jnp.mean(out * out, axis=-1, keepdims=True) + EPS) * gain
    return normed * (hg * jax.nn.sigmoid(hg)), new_state_t


def _hgrn_body(q_ref, f_ref, i_ref, g_ref, lb_ref, gain_ref, o_ref, state_ref, *, layer):
    @pl.when(pl.program_id(1) == 0)
    def _():
        state_ref[...] = jnp.zeros_like(state_ref)

    for hh in range(state_ref.shape[0]):
        sl = slice(hh * HEAD_DIM, (hh + 1) * HEAD_DIM)
        table = lb_ref[:, sl]
        e = jnp.exp(table - jnp.max(table, axis=0, keepdims=True))
        lb = jnp.sum(e[:layer + 1], axis=0, keepdims=True) / jnp.sum(e, axis=0, keepdims=True)
        out, state_ref[hh] = _hgrn_head(q_ref[:, sl], f_ref[:, sl], i_ref[:, sl], g_ref[:, sl],
                                        lb, gain_ref[:, sl], state_ref[hh])
        o_ref[:, sl] = out.astype(o_ref.dtype)


def _hgrn(proj, lb_table, out_gain, n_heads, layer):
    s = proj.shape[0]
    c = _tile(s, HGRN_CHUNK)
    n_slots = lb_table.shape[0]
    group = HGRN_HEAD_GROUP if n_heads % HGRN_HEAD_GROUP == 0 else 1
    n_groups = n_heads // group
    width = group * HEAD_DIM

    def col(block):
        return pl.BlockSpec((c, width), lambda g, t: (t, block * n_groups + g))

    return pl.pallas_call(
        functools.partial(_hgrn_body, layer=layer),
        grid=(n_groups, s // c),
        in_specs=[col(0), col(1), col(2), col(3),
                  pl.BlockSpec((n_slots, width), lambda g, t: (0, g)),
                  pl.BlockSpec((1, width), lambda g, t: (0, g))],
        out_specs=pl.BlockSpec((c, width), lambda g, t: (t, g)),
        out_shape=jax.ShapeDtypeStruct((s, n_heads * HEAD_DIM), BF16),
        scratch_shapes=[pltpu.VMEM((group, HEAD_DIM, HEAD_DIM), F32)],
        compiler_params=_params(("parallel", "arbitrary")),
        name="hgrn2",
    )(proj, proj, proj, proj, lb_table, out_gain)


def _out_proj_body(a_ref, b_ref, wa_ref, wb_ref, x_ref, g_ref, o_ref):
    mix = (jnp.dot(a_ref[...], wa_ref[...], preferred_element_type=F32)
           + jnp.dot(b_ref[...], wb_ref[...], preferred_element_type=F32))
    o_ref[...] = x_ref[...] + g_ref[...] * mix


def _out_proj(fox, hgrn, w_fox, w_hgrn, x, gate):
    s, d = x.shape
    tm, tn = _tile(s, 1024), _tile(d, 512)
    ka, kb = fox.shape[1], hgrn.shape[1]
    return pl.pallas_call(
        _out_proj_body,
        grid=(s // tm, d // tn),
        in_specs=[pl.BlockSpec((tm, ka), lambda i, j: (i, 0)),
                  pl.BlockSpec((tm, kb), lambda i, j: (i, 0)),
                  pl.BlockSpec((ka, tn), lambda i, j: (0, j)),
                  pl.BlockSpec((kb, tn), lambda i, j: (0, j)),
                  pl.BlockSpec((tm, tn), lambda i, j: (i, j)),
                  pl.BlockSpec((1, tn), lambda i, j: (0, j))],
        out_specs=pl.BlockSpec((tm, tn), lambda i, j: (i, j)),
        out_shape=jax.ShapeDtypeStruct((s, d), F32),
        compiler_params=_params(("parallel", "arbitrary")),
        name="out_proj",
    )(fox, hgrn, w_fox, w_hgrn, x, gate)


PEER_NTOP = PEER_TOPK + 1
PEER_TOP_ROWS = 24
PEER_NCAND = 52


def _top_rows(arrays, k, exact_ties, ranked=()):
    vals = list(arrays)
    rows = [[] for _ in vals]
    ranks = {n: jnp.full(vals[n].shape, float(vals[n].shape[0]), F32) for n in ranked}
    for r in range(k):
        for n, v in enumerate(vals):
            m = jnp.max(v, axis=0, keepdims=True)
            rows[n].append(m)
            hit = v == m
            if exact_ties:
                idx = lax.broadcasted_iota(jnp.int32, v.shape, 0)
                first = jnp.min(jnp.where(hit, idx, v.shape[0]), axis=0, keepdims=True)
                hit = idx == first
            vals[n] = jnp.where(hit, -jnp.inf, v)
            if n in ranks:
                ranks[n] = jnp.where(hit, float(r), ranks[n])
    return rows, vals, ranks


def _n_finite(v):
    return jnp.sum(jnp.where(v > -jnp.inf, 1.0, 0.0), axis=0, keepdims=True)


def _peer_thresholds(s0s, s1s, a_ref, b_ref, exact_ties):
    nh = len(s0s)
    rows, left, ranks = _top_rows(s0s + s1s, PEER_NTOP, exact_ties, ranked=range(nh, 2 * nh))
    excess = sum(jnp.abs(_n_finite(v) - (PEER_KEYS - PEER_NTOP)) for v in left)

    r24 = lax.broadcasted_iota(jnp.int32, a_ref.shape[1:], 0)
    r8 = lax.broadcasted_iota(jnp.int32, (8, a_ref.shape[2]), 0)
    ninf = -jnp.inf

    def grp(x, r, lo, hi):
        return jnp.where((r >= lo) & (r <= hi), x, ninf)

    cands = []
    for hd in range(nh):
        a_ref[hd] = jnp.full(a_ref.shape[1:], ninf, F32)
        b_ref[hd] = jnp.full(b_ref.shape[1:], ninf, F32)
        for r in range(PEER_NTOP):
            a_ref[hd, r:r + 1, :] = rows[hd][r]
            b_ref[hd, r:r + 1, :] = rows[nh + hd][r]
        a24, b24 = a_ref[hd], b_ref[hd]
        a8, b8 = a24[:8], b24[:8]
        cands.append(jnp.concatenate([
            grp(a24[0:1] + b24, r24, 0, 16),
            grp(a24 + b24[0:1], r24, 1, 16),
            grp(a8[1:2] + b8, r8, 1, 7),
            grp(a8 + b8[1:2], r8, 2, 7),
            grp(a8[2:3] + b8, r8, 2, 4),
            grp(a8 + b8[2:3], r8, 3, 4),
            grp(a8[3:4] + b8, r8, 3, 3),
        ], axis=0))
    tops, left, _ = _top_rows(cands, PEER_NTOP, exact_ties)
    excess = excess + sum(jnp.abs(_n_finite(v) - (PEER_NCAND - PEER_NTOP)) for v in left)

    outs = []
    for hd in range(nh):
        top = tops[hd]
        tau = 0.5 * (top[PEER_TOPK - 1] + top[PEER_TOPK])
        z = sum(jnp.exp(t - top[0]) for t in top[:PEER_TOPK])
        thr = tau - s0s[hd]
        cnt = sum(jnp.where(b_r >= thr, 1.0, 0.0) for b_r in rows[nh + hd])
        outs.append((cnt,
                     jnp.exp(s0s[hd] - rows[hd][0]) / z,
                     ranks[nh + hd],
                     jnp.exp(s1s[hd] - rows[nh + hd][0])))
    return outs, excess


def _peer_route_body(h_ref, wq_ref, sk_ref, cnt_ref, e0_ref, rank_ref, e1_ref, a_ref, b_ref):
    nh = sk_ref.shape[0]
    q = jnp.dot(h_ref[...], wq_ref[...], preferred_element_type=F32).astype(BF16)
    s0s, s1s = [], []
    for hd in range(nh):
        c0 = 2 * hd * PEER_KEYS
        s0s.append(lax.dot_general(sk_ref[hd, 0], q[:, c0:c0 + PEER_KEYS], _NT,
                                   preferred_element_type=F32))
        s1s.append(lax.dot_general(sk_ref[hd, 1], q[:, c0 + PEER_KEYS:c0 + 2 * PEER_KEYS], _NT,
                                   preferred_element_type=F32))

    def emit(outs):
        for hd, (cnt, e0, rank1, e1) in enumerate(outs):
            cnt_ref[hd] = cnt
            e0_ref[hd] = e0
            rank_ref[hd] = rank1.astype(rank_ref.dtype)
            e1_ref[hd] = e1.astype(e1_ref.dtype)

    outs, excess = _peer_thresholds(s0s, s1s, a_ref, b_ref, exact_ties=False)
    emit(outs)

    @pl.when(jnp.max(excess) > 0.0)
    def _():
        emit(_peer_thresholds(s0s, s1s, a_ref, b_ref, exact_ties=True)[0])


def _peer_route(h, wq, sub_keys):
    s, d = h.shape
    ph = sub_keys.shape[0]
    tt = _tile(s, 256)
    nh = 2 if ph % 2 == 0 else 1
    out_spec = pl.BlockSpec((nh, PEER_KEYS, tt), lambda t, p: (p, 0, t))
    out_sds = [jax.ShapeDtypeStruct((ph, PEER_KEYS, s), dt) for dt in (F32, F32, BF16, BF16)]
    return pl.pallas_call(
        _peer_route_body,
        grid=(s // tt, ph // nh),
        in_specs=[pl.BlockSpec((tt, d), lambda t, p: (t, 0)),
                  pl.BlockSpec((d, nh * 2 * PEER_KEYS), lambda t, p: (0, p)),
                  pl.BlockSpec((nh, 2, PEER_KEYS, PEER_KEYS), lambda t, p: (p, 0, 0, 0))],
        out_specs=[out_spec] * 4,
        out_shape=out_sds,
        scratch_shapes=[pltpu.VMEM((nh, PEER_TOP_ROWS, tt), F32), pltpu.VMEM((nh, PEER_TOP_ROWS, tt), F32)],
        compiler_params=_params(("parallel", "arbitrary")),
        name="peer_route",
    )(h, wq, sub_keys)


def _peer_act_body(u_ref, ht_ref, cnt_ref, e0_ref, rank_ref, e1_ref, w_ref, *, rows):
    n_heads = rank_ref.shape[0]
    te = u_ref.shape[0]
    for ch in range(te // rows):
        z = jnp.dot(u_ref[ch * rows:(ch + 1) * rows, :], ht_ref[...], preferred_element_type=F32)
        for ci in range(rows // PEER_KEYS):
            ii = ch * (rows // PEER_KEYS) + ci
            zi = z[ci * PEER_KEYS:(ci + 1) * PEER_KEYS, :]
            act = 0.5 * zi * (1.0 + lax.erf(zi * SQRT_HALF))
            gate = jnp.zeros(zi.shape, BF16)
            for hd in range(n_heads):
                sel = rank_ref[hd] < cnt_ref[hd, ii:ii + 1, :].astype(BF16)
                gate = gate + jnp.where(sel, e1_ref[hd] * e0_ref[hd, ii:ii + 1, :].astype(BF16),
                                        jnp.zeros((), BF16))
            w_ref[:, ii * PEER_KEYS:(ii + 1) * PEER_KEYS] = (act * gate.astype(F32)).T.astype(w_ref.dtype)


def _peer_act(u, h_t, cnt, e0, rank, e1):
    n_exp, d = u.shape
    s = h_t.shape[1]
    ph = cnt.shape[0]
    te = 8 * PEER_KEYS
    tt = _tile(s, 512)
    assert n_exp % te == 0
    per_i = pl.BlockSpec((ph, 8, tt), lambda t, e: (0, e, t))
    per_j = pl.BlockSpec((ph, PEER_KEYS, tt), lambda t, e: (0, 0, t))
    return pl.pallas_call(
        functools.partial(_peer_act_body, rows=te // 2),
        grid=(s // tt, n_exp // te),
        in_specs=[pl.BlockSpec((te, d), lambda t, e: (e, 0)),
                  pl.BlockSpec((d, tt), lambda t, e: (0, t)),
                  per_i, per_i, per_j, per_j],
        out_specs=pl.BlockSpec((tt, te), lambda t, e: (t, e)),
        out_shape=jax.ShapeDtypeStruct((s, n_exp), BF16),
        compiler_params=_params(("parallel", "arbitrary")),
        name="peer_act",
    )(u, h_t, cnt, e0, rank, e1)


def _peer_down_body(w_ref, v_ref, x_ref, g_ref, o_ref, acc_ref):
    k = pl.program_id(2)

    @pl.when(k == 0)
    def _():
        acc_ref[...] = jnp.zeros_like(acc_ref)

    acc_ref[...] += jnp.dot(w_ref[...], v_ref[...], preferred_element_type=F32)

    @pl.when(k == pl.num_programs(2) - 1)
    def _():
        o_ref[...] = x_ref[...] + g_ref[...] * acc_ref[...]


def _peer_down(w, v, x, gate):
    s, n_exp = w.shape
    d = v.shape[1]
    tm, tn, tk = _tile(s, 1024), _tile(d, 1024), _tile(n_exp, 2048)
    return pl.pallas_call(
        _peer_down_body,
        grid=(s // tm, d // tn, n_exp // tk),
        in_specs=[pl.BlockSpec((tm, tk), lambda i, j, k: (i, k)),
                  pl.BlockSpec((tk, tn), lambda i, j, k: (k, j)),
                  pl.BlockSpec((tm, tn), lambda i, j, k: (i, j)),
                  pl.BlockSpec((1, tn), lambda i, j, k: (0, j))],
        out_specs=pl.BlockSpec((tm, tn), lambda i, j, k: (i, j)),
        out_shape=jax.ShapeDtypeStruct((s, d), F32),
        scratch_shapes=[pltpu.VMEM((tm, tn), F32)],
        compiler_params=_params(("parallel", "parallel", "arbitrary")),
        name="peer_down",
    )(w, v, x, gate)


def kernel(x, c, ada_w, ada_b, norm1_gain, norm2_gain, w_in, fox_f_bias, fox_q_gain, fox_k_gain,
           hgrn_lower_bounds, hgrn_out_gain, w_out, peer_w_query, peer_sub_keys, peer_u, peer_v):
    batch, seq, d = x.shape
    assert batch == 1, "one sequence per call"
    depth = ada_w.shape[0]
    fox_heads = fox_f_bias.shape[-1]
    hgrn_heads = hgrn_out_gain.shape[1]
    fox_width, hgrn_width = fox_heads * HEAD_DIM, hgrn_heads * HEAD_DIM
    assert fox_q_gain.shape[-1] == HEAD_DIM and hgrn_out_gain.shape[-1] == HEAD_DIM
    assert w_in.shape[-1] == 3 * fox_width + fox_heads + 4 * hgrn_width
    assert peer_sub_keys.shape[2:] == (2, PEER_KEYS, PEER_KEYS) and peer_u.shape[1] == PEER_KEYS ** 2
    o3 = 3 * fox_width
    o4 = o3 + fox_heads

    xs = x.reshape(seq, d)
    c_col = c.reshape(d, 1)
    for layer in range(depth):
        mod = _ada_mod(c_col, ada_w[layer], ada_b[layer].reshape(1, -1))
        shift1, scale1, gate1, shift2, scale2, gate2 = (mod[:, n * d:(n + 1) * d] for n in range(6))

        h = _norm_mod(xs, norm1_gain[layer].reshape(1, d), scale1, shift1)
        w_l = w_in[layer]
        qk_gains = jnp.concatenate([fox_q_gain[layer].reshape(1, fox_width) * (HEAD_DIM ** -0.5 * LOG2_E),
                                    fox_k_gain[layer].reshape(1, fox_width)], axis=1)
        qk = _fox_proj(h, w_in, layer, qk_gains)
        v_t = _fox_v_proj(h, w_in, layer, 2 * fox_width, fox_width)
        pad = HEAD_DIM - fox_heads
        cum_f = _fox_gate(h, jnp.pad(w_l[:, o3:o4], ((0, 0), (0, pad))).astype(BF16),
                          jnp.pad(fox_f_bias[layer].reshape(1, fox_heads), ((0, 0), (0, pad))), fox_heads)
        fox_out = _fox_attention(qk, v_t, cum_f, fox_heads)

        hproj = _matmul(h, w_l[:, o4:].astype(BF16), F32)
        hgrn_out = _hgrn(hproj, hgrn_lower_bounds, hgrn_out_gain[layer].reshape(1, hgrn_width),
                         hgrn_heads, layer)

        w_o = w_out[layer].astype(BF16)
        xs = _out_proj(fox_out, hgrn_out, w_o[:fox_width], w_o[fox_width:], xs, gate1)

        h, h_t = _norm_mod(xs, norm2_gain[layer].reshape(1, d), scale2, shift2, with_transposed=True)
        cnt, e0, rank1, e1 = _peer_route(h, peer_w_query[layer].astype(BF16),
                                         peer_sub_keys[layer].astype(BF16))
        w_act = _peer_act(peer_u[layer].astype(BF16), h_t, cnt, e0, rank1, e1)
        xs = _peer_down(w_act, peer_v[layer].astype(BF16), xs, gate2)
    return xs.reshape(batch, seq, d)
```

```python
import functools

import jax
import jax.numpy as jnp
from jax import lax
from jax.experimental import pallas as pl
from jax.experimental.pallas import tpu as pltpu

F32 = jnp.float32
BF16 = jnp.bfloat16

HEAD_DIM = 128
PEER_KEYS = 128
PEER_TOPK = 16
HGRN_CHUNK = 128
HGRN_HEAD_GROUP = 4
EPS = 1e-6
MASKED = -1e30
SQRT_HALF = 0.7071067811865476
LOG2_E = 1.4426950408889634
ATTN_LANE_GROUP = 256
V7X_VMEM_LIMIT_BYTES = 52 * 1024 * 1024

_NT = (((1,), (1,)), ((), ()))


def _params(semantics):
    return pltpu.CompilerParams(dimension_semantics=semantics, vmem_limit_bytes=V7X_VMEM_LIMIT_BYTES)


def _tile(n, t):
    if n <= t:
        return n
    t -= t % HEAD_DIM
    while n % t:
        t -= HEAD_DIM
    assert t > 0, n
    return t


def _split3(x):
    hi = x.astype(BF16)
    r1 = x - hi.astype(F32)
    mid = r1.astype(BF16)
    lo = (r1 - mid.astype(F32)).astype(BF16)
    return hi, mid, lo


def _ada_body(c_ref, w_ref, b_ref, o_ref, acc_ref):
    k = pl.program_id(1)

    @pl.when(k == 0)
    def _():
        acc_ref[...] = jnp.zeros_like(acc_ref)

    c = c_ref[...]
    c_act = c * jax.nn.sigmoid(c)
    acc_ref[...] += jnp.sum(w_ref[...] * c_act, axis=0, keepdims=True)

    @pl.when(k == pl.num_programs(1) - 1)
    def _():
        o_ref[...] = acc_ref[...] + b_ref[...]


def _ada_mod(c_col, w, b_row):
    d, n = w.shape
    tk, tn = _tile(d, 512), _tile(n, 2048)
    return pl.pallas_call(
        _ada_body,
        grid=(n // tn, d // tk),
        in_specs=[pl.BlockSpec((tk, 1), lambda j, k: (k, 0)),
                  pl.BlockSpec((tk, tn), lambda j, k: (k, j)),
                  pl.BlockSpec((1, tn), lambda j, k: (0, j))],
        out_specs=pl.BlockSpec((1, tn), lambda j, k: (0, j)),
        out_shape=jax.ShapeDtypeStruct((1, n), F32),
        scratch_shapes=[pltpu.VMEM((1, tn), F32)],
        compiler_params=_params(("parallel", "arbitrary")),
        name="ada_mod",
    )(c_col, w, b_row)


def _norm_mod_body(x_ref, g_ref, sc_ref, sh_ref, o_ref, *maybe_ot_ref):
    x = x_ref[...]
    y = x * lax.rsqrt(jnp.mean(x * x, axis=-1, keepdims=True) + EPS)
    h = y * g_ref[...] * (1.0 + sc_ref[...]) + sh_ref[...]
    o_ref[...] = h.astype(o_ref.dtype)
    for ot_ref in maybe_ot_ref:
        ot_ref[...] = h.T.astype(ot_ref.dtype)


def _norm_mod(x, gain, scale, shift, with_transposed=False):
    s, d = x.shape
    tm = _tile(s, 256)
    row = pl.BlockSpec((1, d), lambda i: (0, 0))
    out_specs = [pl.BlockSpec((tm, d), lambda i: (i, 0))]
    out_shape = [jax.ShapeDtypeStruct((s, d), BF16)]
    if with_transposed:
        out_specs.append(pl.BlockSpec((d, tm), lambda i: (0, i)))
        out_shape.append(jax.ShapeDtypeStruct((d, s), BF16))
    outs = pl.pallas_call(
        _norm_mod_body,
        grid=(s // tm,),
        in_specs=[pl.BlockSpec((tm, d), lambda i: (i, 0)), row, row, row],
        out_specs=out_specs,
        out_shape=out_shape,
        compiler_params=_params(("parallel",)),
        name="norm_mod",
    )(x, gain, scale, shift)
    return outs if with_transposed else outs[0]


def _fox_proj_body(h_ref, w_ref, g_ref, o_ref):
    y = lax.dot_general(h_ref[...], w_ref[0].astype(BF16), _NT, preferred_element_type=F32)
    g = g_ref[...]
    for hh in range(y.shape[1] // HEAD_DIM):
        sl = slice(hh * HEAD_DIM, (hh + 1) * HEAD_DIM)
        yh = y[:, sl]
        r = lax.rsqrt(jnp.mean(yh * yh, axis=-1, keepdims=True) + EPS)
        o_ref[:, sl] = (yh * r * g[:, sl]).astype(o_ref.dtype)


def _w_rows_spec(layer, first_row, tn, d):
    assert first_row % 8 == 0 and tn % 8 == 0
    return pl.BlockSpec((pl.Element(1), pl.Element(tn), pl.Element(d)),
                        lambda i, j: (layer, pl.multiple_of(first_row + j * tn, 8), 0))


def _fox_proj(h, w_t, layer, gains):
    s, d = h.shape
    n = gains.shape[1]
    tm, tn = _tile(s, 1024), _tile(n, 512)
    return pl.pallas_call(
        _fox_proj_body,
        grid=(s // tm, n // tn),
        in_specs=[pl.BlockSpec((tm, d), lambda i, j: (i, 0)),
                  _w_rows_spec(layer, 0, tn, d),
                  pl.BlockSpec((1, tn), lambda i, j: (0, j))],
        out_specs=pl.BlockSpec((tm, tn), lambda i, j: (i, j)),
        out_shape=jax.ShapeDtypeStruct((s, n), BF16),
        compiler_params=_params(("parallel", "arbitrary")),
        name="fox_proj",
    )(h, w_t, gains)


def _fox_v_proj_body(h_ref, w_ref, o_ref):
    o_ref[...] = lax.dot_general(w_ref[0].astype(BF16), h_ref[...], _NT,
                                 preferred_element_type=F32).astype(o_ref.dtype)


def _fox_v_proj(h, w_t, layer, first_row, n):
    s, d = h.shape
    tm, tn = _tile(s, 1024), _tile(n, 512)
    return pl.pallas_call(
        _fox_v_proj_body,
        grid=(s // tm, n // tn),
        in_specs=[pl.BlockSpec((tm, d), lambda i, j: (i, 0)),
                  _w_rows_spec(layer, first_row, tn, d)],
        out_specs=pl.BlockSpec((tn, tm), lambda i, j: (j, i)),
        out_shape=jax.ShapeDtypeStruct((n, s), BF16),
        compiler_params=_params(("parallel", "arbitrary")),
        name="fox_v_proj",
    )(h, w_t)


def _hgrn_proj_body(h_ref, w_ref, o_ref):
    o_ref[...] = lax.dot_general(h_ref[...], w_ref[0].astype(BF16), _NT,
                                 preferred_element_type=F32).astype(o_ref.dtype)


def _hgrn_proj(h, w_t, layer, first_row, n, out_dtype):
    s, d = h.shape
    tm, tn = _tile(s, 1024), _tile(n, 512)
    return pl.pallas_call(
        _hgrn_proj_body,
        grid=(s // tm, n // tn),
        in_specs=[pl.BlockSpec((tm, d), lambda i, j: (i, 0)),
                  _w_rows_spec(layer, first_row, tn, d)],
        out_specs=pl.BlockSpec((tm, tn), lambda i, j: (i, j)),
        out_shape=jax.ShapeDtypeStruct((s, n), out_dtype),
        compiler_params=_params(("parallel", "arbitrary")),
        name="hgrn_proj",
    )(h, w_t)


def _fox_gate_body(h_ref, w_ref, b_ref, o_ref, carry_ref):
    i = pl.program_id(0)

    @pl.when(i == 0)
    def _():
        carry_ref[...] = jnp.zeros_like(carry_ref)

    logit = jnp.dot(h_ref[...], w_ref[...], preferred_element_type=F32) + b_ref[...]
    log_f = jnp.minimum(logit, 0.0) - jnp.log1p(jnp.exp(-jnp.abs(logit)))
    ts = log_f.shape[0]
    lower = (lax.broadcasted_iota(jnp.int32, (ts, ts), 1)
             <= lax.broadcasted_iota(jnp.int32, (ts, ts), 0)).astype(BF16)
    csum = sum(jnp.dot(lower, part, preferred_element_type=F32) for part in _split3(log_f))
    csum = csum + carry_ref[...]
    carry_ref[...] = csum[ts - 1:ts, :]
    scaled = csum * LOG2_E
    for hd in range(o_ref.shape[0]):
        o_ref[hd] = jnp.broadcast_to(scaled[:, hd:hd + 1], (ts, HEAD_DIM))


def _fox_gate(h, w_pad, bias_pad, n_heads):
    s, d = h.shape
    ts = _tile(s, 512)
    return pl.pallas_call(
        _fox_gate_body,
        grid=(s // ts,),
        in_specs=[pl.BlockSpec((ts, d), lambda i: (i, 0)),
                  pl.BlockSpec((d, HEAD_DIM), lambda i: (0, 0)),
                  pl.BlockSpec((1, HEAD_DIM), lambda i: (0, 0))],
        out_specs=pl.BlockSpec((n_heads, ts, HEAD_DIM), lambda i: (0, i, 0)),
        out_shape=jax.ShapeDtypeStruct((n_heads, s, HEAD_DIM), F32),
        scratch_shapes=[pltpu.VMEM((1, HEAD_DIM), F32)],
        compiler_params=_params(("arbitrary",)),
        name="fox_gate",
    )(h, w_pad, bias_pad)


def _fox_attn_body(q_ref, k_ref, vt_ref, cb_ref, o_ref, sa_ref, sb_ref, m_ref, l_ref, acc_ref, *, t):
    i = pl.program_id(1)
    q = q_ref[...]
    q_start = pl.multiple_of(i * t, t)
    c_first = cb_ref[pl.ds(q_start, 8), :][0:1]
    m_ref[...] = jnp.full_like(m_ref, MASKED)
    l_ref[...] = jnp.zeros_like(l_ref)
    acc_ref[...] = jnp.zeros_like(acc_ref)
    group = min(t, ATTN_LANE_GROUP)

    def scores(j):
        k_start = pl.multiple_of(j * t, t)
        return lax.dot_general(k_ref[pl.ds(k_start, t), :], q, _NT, preferred_element_type=F32)

    def consume(j, sc_ref, masked):
        k_start = pl.multiple_of(j * t, t)
        bias = c_first - cb_ref[pl.ds(k_start, t), :]
        bias = jnp.concatenate([bias] * (group // HEAD_DIM), axis=1)
        v_t = vt_ref[:, pl.ds(k_start, t)]
        for g in range(t // group):
            lanes = slice(g * group, (g + 1) * group)
            s = sc_ref[:, lanes] + bias
            if masked:
                k_pos = lax.broadcasted_iota(jnp.int32, (t, group), 0)
                q_pos = lax.broadcasted_iota(jnp.int32, (t, group), 1) + g * group
                s = jnp.where(k_pos <= q_pos, s, MASKED)
            m_old = m_ref[:, lanes]
            m_new = jnp.maximum(m_old, jnp.max(s, axis=0, keepdims=True))
            alpha = jnp.exp2(m_old - m_new)
            p = jnp.exp2(s - m_new)
            l_ref[:, lanes] = alpha * l_ref[:, lanes] + jnp.sum(p, axis=0, keepdims=True)
            acc_ref[:, lanes] = alpha * acc_ref[:, lanes] + jnp.dot(
                v_t, p.astype(BF16), preferred_element_type=F32)
            m_ref[:, lanes] = m_new

    sa_ref[...] = scores(0)

    def pair(jj, carry):
        j = 2 * jj
        sb_ref[...] = scores(j + 1)
        consume(j, sa_ref, masked=False)
        sa_ref[...] = scores(j + 2)
        consume(j + 1, sb_ref, masked=False)
        return carry

    lax.fori_loop(0, lax.shift_right_logical(i, 1), pair, 0)
    odd = (i & 1) == 1

    @pl.when(jnp.logical_not(odd))
    def _():
        consume(i, sa_ref, masked=True)

    @pl.when(odd)
    def _():
        sb_ref[...] = scores(i)
        consume(i - 1, sa_ref, masked=False)
        consume(i, sb_ref, masked=True)

    o_ref[...] = (acc_ref[...] / l_ref[...]).T.astype(o_ref.dtype)


def _fox_attention(qk, v_t, cum_f, n_heads):
    s = qk.shape[0]
    t = _tile(s, 512)
    return pl.pallas_call(
        functools.partial(_fox_attn_body, t=t),
        grid=(n_heads, s // t),
        in_specs=[pl.BlockSpec((t, HEAD_DIM), lambda h, i: (i, h)),
                  pl.BlockSpec((s, HEAD_DIM), lambda h, i: (0, n_heads + h)),
                  pl.BlockSpec((HEAD_DIM, s), lambda h, i: (h, 0)),
                  pl.BlockSpec((None, s, HEAD_DIM), lambda h, i: (h, 0, 0))],
        out_specs=pl.BlockSpec((t, HEAD_DIM), lambda h, i: (i, h)),
        out_shape=jax.ShapeDtypeStruct((s, n_heads * HEAD_DIM), BF16),
        scratch_shapes=[pltpu.VMEM((t, t), F32), pltpu.VMEM((t, t), F32),
                        pltpu.VMEM((1, t), F32), pltpu.VMEM((1, t), F32),
                        pltpu.VMEM((HEAD_DIM, t), F32)],
        compiler_params=_params(("parallel", "arbitrary")),
        name="fox_attention",
    )(qk, qk, v_t, cum_f)


def _hgrn_head(hq, hf, val, hg, lb, gain, state_t):
    forget = lb + (1.0 - lb) * jax.nn.sigmoid(hf)
    log_f = jnp.log(forget)
    key = 1.0 - forget
    query = hq * jax.nn.sigmoid(hq) * (HEAD_DIM ** -0.5)
    val_b = val.astype(BF16)
    c = log_f.shape[0]

    row = lax.broadcasted_iota(jnp.int32, (c, c), 0)
    col = lax.broadcasted_iota(jnp.int32, (c, c), 1)
    lower = (col <= row).astype(BF16)
    b = sum(jnp.dot(lower, part, preferred_element_type=F32) for part in _split3(log_f))
    b_last = b[c - 1:c, :]

    out = lax.dot_general((query * jnp.exp(b)).astype(BF16), state_t.astype(BF16), _NT,
                          preferred_element_type=F32)

    t_idx = lax.broadcasted_iota(jnp.int32, (c, HEAD_DIM), 0)
    sep = row ^ col
    scores = jnp.where(sep == 0, lax.dot_general(query.astype(BF16), key.astype(BF16), _NT,
                                                 preferred_element_type=F32), 0.0)
    half = c // 2
    while half >= 1:
        blk = 2 * half
        if half >= 4:
            ref_rows = jnp.broadcast_to(b.reshape(c // blk, blk, HEAD_DIM)[:, half - 1:half, :],
                                        (c // blk, blk, HEAD_DIM)).reshape(c, HEAD_DIM)
        elif half == 2:
            pos = t_idx & 3
            ref_rows = jnp.where(pos == 0, pltpu.roll(b, c - 1, 0),
                                 jnp.where(pos == 2, pltpu.roll(b, 1, 0),
                                           jnp.where(pos == 3, pltpu.roll(b, 2, 0), b)))
        else:
            ref_rows = jnp.where((t_idx & 1) == 1, pltpu.roll(b, 1, 0), b)
        in_upper = (t_idx & half) != 0
        x = jnp.where(in_upper, query, key) * jnp.exp(-jnp.abs(b - ref_rows))
        a_side = jnp.where(in_upper, x, 0.0).astype(BF16)
        b_side = jnp.where(in_upper, 0.0, x).astype(BF16)
        level = lax.dot_general(a_side, b_side, _NT, preferred_element_type=F32)
        scores = scores + (level if blk == c else jnp.where(sep < blk, level, 0.0))
        half //= 2
    out = out + jnp.dot(scores.astype(BF16), val_b, preferred_element_type=F32)

    key_dec = (key * jnp.exp(b_last - b)).astype(BF16)
    new_state_t = state_t * jnp.exp(b_last) + jnp.dot(val.T.astype(BF16), key_dec,
                                                      preferred_element_type=F32)

    normed = out * lax.rsqrt(jnp.mean(out * out, axis=-1, keepdims=True) + EPS) * gain
    return normed * (hg * jax.nn.sigmoid(hg)), new_state_t


def _hgrn_body(q_ref, f_ref, i_ref, g_ref, lb_ref, gain_ref, o_ref, state_ref, *, layer):
    @pl.when(pl.program_id(1) == 0)
    def _():
        state_ref[...] = jnp.zeros_like(state_ref)

    for hh in range(state_ref.shape[0]):
        sl = slice(hh * HEAD_DIM, (hh + 1) * HEAD_DIM)
        table = lb_ref[:, sl]
        e = jnp.exp(table - jnp.max(table, axis=0, keepdims=True))
        lb = jnp.sum(e[:layer + 1], axis=0, keepdims=True) / jnp.sum(e, axis=0, keepdims=True)
        out, state_ref[hh] = _hgrn_head(q_ref[:, sl], f_ref[:, sl], i_ref[:, sl], g_ref[:, sl],
                                        lb, gain_ref[:, sl], state_ref[hh])
        o_ref[:, sl] = out.astype(o_ref.dtype)


def _hgrn(proj, lb_table, out_gain, n_heads, layer):
    s = proj.shape[0]
    c = _tile(s, HGRN_CHUNK)
    n_slots = lb_table.shape[0]
    group = HGRN_HEAD_GROUP if n_heads % HGRN_HEAD_GROUP == 0 else 1
    n_groups = n_heads // group
    width = group * HEAD_DIM

    def col(block):
        return pl.BlockSpec((c, width), lambda g, t: (t, block * n_groups + g))

    return pl.pallas_call(
        functools.partial(_hgrn_body, layer=layer),
        grid=(n_groups, s // c),
        in_specs=[col(0), col(1), col(2), col(3),
                  pl.BlockSpec((n_slots, width), lambda g, t: (0, g)),
                  pl.BlockSpec((1, width), lambda g, t: (0, g))],
        out_specs=pl.BlockSpec((c, width), lambda g, t: (t, g)),
        out_shape=jax.ShapeDtypeStruct((s, n_heads * HEAD_DIM), BF16),
        scratch_shapes=[pltpu.VMEM((group, HEAD_DIM, HEAD_DIM), F32)],
        compiler_params=_params(("parallel", "arbitrary")),
        name="hgrn2",
    )(proj, proj, proj, proj, lb_table, out_gain)


def _out_proj_body(a_ref, b_ref, wa_ref, wb_ref, x_ref, g_ref, o_ref):
    mix = (jnp.dot(a_ref[...], wa_ref[0].astype(BF16), preferred_element_type=F32)
           + jnp.dot(b_ref[...], wb_ref[0].astype(BF16), preferred_element_type=F32))
    o_ref[...] = x_ref[...] + g_ref[...] * mix


def _out_proj(fox, hgrn, w_out, layer, x, gate):
    s, d = x.shape
    tm, tn = _tile(s, 1024), _tile(d, 512)
    ka, kb = fox.shape[1], hgrn.shape[1]

    def w_rows(first_row, k):
        return pl.BlockSpec((pl.Element(1), pl.Element(k), pl.Element(tn)),
                            lambda i, j: (layer, first_row, pl.multiple_of(j * tn, HEAD_DIM)))

    return pl.pallas_call(
        _out_proj_body,
        grid=(s // tm, d // tn),
        in_specs=[pl.BlockSpec((tm, ka), lambda i, j: (i, 0)),
                  pl.BlockSpec((tm, kb), lambda i, j: (i, 0)),
                  w_rows(0, ka),
                  w_rows(ka, kb),
                  pl.BlockSpec((tm, tn), lambda i, j: (i, j)),
                  pl.BlockSpec((1, tn), lambda i, j: (0, j))],
        out_specs=pl.BlockSpec((tm, tn), lambda i, j: (i, j)),
        out_shape=jax.ShapeDtypeStruct((s, d), F32),
        compiler_params=_params(("parallel", "arbitrary")),
        name="out_proj",
    )(fox, hgrn, w_out, w_out, x, gate)


PEER_NTOP = PEER_TOPK + 1
PEER_TOP_ROWS = 24
PEER_NCAND = 52


def _top_rows(arrays, k, exact_ties, ranked=()):
    vals = list(arrays)
    rows = [[] for _ in vals]
    ranks = {n: jnp.full(vals[n].shape, float(vals[n].shape[0]), F32) for n in ranked}
    for r in range(k):
        for n, v in enumerate(vals):
            m = jnp.max(v, axis=0, keepdims=True)
            rows[n].append(m)
            hit = v == m
            if exact_ties:
                idx = lax.broadcasted_iota(jnp.int32, v.shape, 0)
                first = jnp.min(jnp.where(hit, idx, v.shape[0]), axis=0, keepdims=True)
                hit = idx == first
            vals[n] = jnp.where(hit, -jnp.inf, v)
            if n in ranks:
                ranks[n] = jnp.where(hit, float(r), ranks[n])
    return rows, vals, ranks


def _n_finite(v):
    return jnp.sum(jnp.where(v > -jnp.inf, 1.0, 0.0), axis=0, keepdims=True)


def _peer_thresholds(s0s, s1s, a_ref, b_ref, exact_ties):
    nh = len(s0s)
    rows, left, ranks = _top_rows(s0s + s1s, PEER_NTOP, exact_ties, ranked=range(nh, 2 * nh))
    excess = sum(jnp.abs(_n_finite(v) - (PEER_KEYS - PEER_NTOP)) for v in left)

    r24 = lax.broadcasted_iota(jnp.int32, a_ref.shape[1:], 0)
    r8 = lax.broadcasted_iota(jnp.int32, (8, a_ref.shape[2]), 0)
    ninf = -jnp.inf

    def grp(x, r, lo, hi):
        return jnp.where((r >= lo) & (r <= hi), x, ninf)

    cands = []
    for hd in range(nh):
        a_ref[hd] = jnp.full(a_ref.shape[1:], ninf, F32)
        b_ref[hd] = jnp.full(b_ref.shape[1:], ninf, F32)
        for r in range(PEER_NTOP):
            a_ref[hd, r:r + 1, :] = rows[hd][r]
            b_ref[hd, r:r + 1, :] = rows[nh + hd][r]
        a24, b24 = a_ref[hd], b_ref[hd]
        a8, b8 = a24[:8], b24[:8]
        cands.append(jnp.concatenate([
            grp(a24[0:1] + b24, r24, 0, 16),
            grp(a24 + b24[0:1], r24, 1, 16),
            grp(a8[1:2] + b8, r8, 1, 7),
            grp(a8 + b8[1:2], r8, 2, 7),
            grp(a8[2:3] + b8, r8, 2, 4),
            grp(a8 + b8[2:3], r8, 3, 4),
            grp(a8[3:4] + b8, r8, 3, 3),
        ], axis=0))
    tops, left, _ = _top_rows(cands, PEER_NTOP, exact_ties)
    excess = excess + sum(jnp.abs(_n_finite(v) - (PEER_NCAND - PEER_NTOP)) for v in left)

    outs = []
    for hd in range(nh):
        top = tops[hd]
        tau = 0.5 * (top[PEER_TOPK - 1] + top[PEER_TOPK])
        z = sum(jnp.exp(t - top[0]) for t in top[:PEER_TOPK])
        thr = tau - s0s[hd]
        cnt = sum(jnp.where(b_r >= thr, 1.0, 0.0) for b_r in rows[nh + hd])
        outs.append((cnt,
                     jnp.exp(s0s[hd] - rows[hd][0]) / z,
                     ranks[nh + hd],
                     jnp.exp(s1s[hd] - rows[nh + hd][0])))
    return outs, excess


def _peer_route_body(h_ref, wq_ref, sk_ref, cnt_ref, e0_ref, rank_ref, e1_ref, a_ref, b_ref):
    nh = sk_ref.shape[0]
    q = jnp.dot(h_ref[...], wq_ref[...].astype(BF16), preferred_element_type=F32).astype(BF16)
    s0s, s1s = [], []
    for hd in range(nh):
        c0 = 2 * hd * PEER_KEYS
        s0s.append(lax.dot_general(sk_ref[hd, 0], q[:, c0:c0 + PEER_KEYS], _NT,
                                   preferred_element_type=F32))
        s1s.append(lax.dot_general(sk_ref[hd, 1], q[:, c0 + PEER_KEYS:c0 + 2 * PEER_KEYS], _NT,
                                   preferred_element_type=F32))

    def emit(outs):
        for hd, (cnt, e0, rank1, e1) in enumerate(outs):
            cnt_ref[hd] = cnt
            e0_ref[hd] = e0
            rank_ref[hd] = rank1.astype(rank_ref.dtype)
            e1_ref[hd] = e1.astype(e1_ref.dtype)

    outs, excess = _peer_thresholds(s0s, s1s, a_ref, b_ref, exact_ties=False)
    emit(outs)

    @pl.when(jnp.max(excess) > 0.0)
    def _():
        emit(_peer_thresholds(s0s, s1s, a_ref, b_ref, exact_ties=True)[0])


def _peer_route(h, wq, layer, sub_keys):
    s, d = h.shape
    ph = sub_keys.shape[0]
    tt = _tile(s, 256)
    nh = 2 if ph % 2 == 0 else 1
    out_spec = pl.BlockSpec((nh, PEER_KEYS, tt), lambda p, t: (p, 0, t))
    out_sds = [jax.ShapeDtypeStruct((ph, PEER_KEYS, s), dt) for dt in (F32, F32, BF16, BF16)]
    return pl.pallas_call(
        _peer_route_body,
        grid=(ph // nh, s // tt),
        in_specs=[pl.BlockSpec((tt, d), lambda p, t: (t, 0)),
                  pl.BlockSpec((None, d, nh * 2 * PEER_KEYS), lambda p, t: (layer, 0, p)),
                  pl.BlockSpec((nh, 2, PEER_KEYS, PEER_KEYS), lambda p, t: (p, 0, 0, 0))],
        out_specs=[out_spec] * 4,
        out_shape=out_sds,
        scratch_shapes=[pltpu.VMEM((nh, PEER_TOP_ROWS, tt), F32), pltpu.VMEM((nh, PEER_TOP_ROWS, tt), F32)],
        compiler_params=_params(("parallel", "arbitrary")),
        name="peer_route",
    )(h, wq, sub_keys)


def _peer_act_body(u_ref, ht_ref, cnt_ref, e0_ref, rank_ref, e1_ref, w_ref, *, rows):
    n_heads = rank_ref.shape[0]
    te = u_ref.shape[0]
    for ch in range(te // rows):
        z = jnp.dot(u_ref[ch * rows:(ch + 1) * rows, :], ht_ref[...], preferred_element_type=F32)
        for ci in range(rows // PEER_KEYS):
            ii = ch * (rows // PEER_KEYS) + ci
            zi = z[ci * PEER_KEYS:(ci + 1) * PEER_KEYS, :]
            act = 0.5 * zi * (1.0 + lax.erf(zi * SQRT_HALF))
            gate = jnp.zeros(zi.shape, BF16)
            for hd in range(n_heads):
                sel = rank_ref[hd] < cnt_ref[hd, ii:ii + 1, :].astype(BF16)
                gate = gate + jnp.where(sel, e1_ref[hd] * e0_ref[hd, ii:ii + 1, :].astype(BF16),
                                        jnp.zeros((), BF16))
            w_ref[:, ii * PEER_KEYS:(ii + 1) * PEER_KEYS] = (act * gate.astype(F32)).T.astype(w_ref.dtype)


def _peer_act(u, h_t, cnt, e0, rank, e1):
    n_exp, d = u.shape
    s = h_t.shape[1]
    ph = cnt.shape[0]
    te = 8 * PEER_KEYS
    tt = _tile(s, 512)
    assert n_exp % te == 0
    per_i = pl.BlockSpec((ph, 8, tt), lambda t, e: (0, e, t))
    per_j = pl.BlockSpec((ph, PEER_KEYS, tt), lambda t, e: (0, 0, t))
    return pl.pallas_call(
        functools.partial(_peer_act_body, rows=te // 2),
        grid=(s // tt, n_exp // te),
        in_specs=[pl.BlockSpec((te, d), lambda t, e: (e, 0)),
                  pl.BlockSpec((d, tt), lambda t, e: (0, t)),
                  per_i, per_i, per_j, per_j],
        out_specs=pl.BlockSpec((tt, te), lambda t, e: (t, e)),
        out_shape=jax.ShapeDtypeStruct((s, n_exp), BF16),
        compiler_params=_params(("parallel", "arbitrary")),
        name="peer_act",
    )(u, h_t, cnt, e0, rank, e1)


def _peer_down_body(w_ref, v_ref, x_ref, g_ref, o_ref, acc_ref):
    k = pl.program_id(2)

    @pl.when(k == 0)
    def _():
        acc_ref[...] = jnp.zeros_like(acc_ref)

    acc_ref[...] += jnp.dot(w_ref[...], v_ref[...], preferred_element_type=F32)

    @pl.when(k == pl.num_programs(2) - 1)
    def _():
        o_ref[...] = x_ref[...] + g_ref[...] * acc_ref[...]


def _peer_down(w, v, x, gate):
    s, n_exp = w.shape
    d = v.shape[1]
    tm, tn, tk = _tile(s, 1024), _tile(d, 1024), _tile(n_exp, 2048)
    return pl.pallas_call(
        _peer_down_body,
        grid=(s // tm, d // tn, n_exp // tk),
        in_specs=[pl.BlockSpec((tm, tk), lambda i, j, k: (i, k)),
                  pl.BlockSpec((tk, tn), lambda i, j, k: (k, j)),
                  pl.BlockSpec((tm, tn), lambda i, j, k: (i, j)),
                  pl.BlockSpec((1, tn), lambda i, j, k: (0, j))],
        out_specs=pl.BlockSpec((tm, tn), lambda i, j, k: (i, j)),
        out_shape=jax.ShapeDtypeStruct((s, d), F32),
        scratch_shapes=[pltpu.VMEM((tm, tn), F32)],
        compiler_params=_params(("parallel", "parallel", "arbitrary")),
        name="peer_down",
    )(w, v, x, gate)


def kernel(x, c, ada_w, ada_b, norm1_gain, norm2_gain, w_in, fox_f_bias, fox_q_gain, fox_k_gain,
           hgrn_lower_bounds, hgrn_out_gain, w_out, peer_w_query, peer_sub_keys, peer_u, peer_v):
    batch, seq, d = x.shape
    assert batch == 1, "one sequence per call"
    depth = ada_w.shape[0]
    fox_heads = fox_f_bias.shape[-1]
    hgrn_heads = hgrn_out_gain.shape[1]
    fox_width, hgrn_width = fox_heads * HEAD_DIM, hgrn_heads * HEAD_DIM
    assert fox_q_gain.shape[-1] == HEAD_DIM and hgrn_out_gain.shape[-1] == HEAD_DIM
    assert w_in.shape[-1] == 3 * fox_width + fox_heads + 4 * hgrn_width
    assert peer_sub_keys.shape[2:] == (2, PEER_KEYS, PEER_KEYS) and peer_u.shape[1] == PEER_KEYS ** 2
    o3 = 3 * fox_width
    o4 = o3 + fox_heads

    xs = x.reshape(seq, d)
    c_col = c.reshape(d, 1)
    w_t = jnp.swapaxes(w_in, 1, 2)
    for layer in range(depth):
        mod = _ada_mod(c_col, ada_w[layer], ada_b[layer].reshape(1, -1))
        shift1, scale1, gate1, shift2, scale2, gate2 = (mod[:, n * d:(n + 1) * d] for n in range(6))

        h = _norm_mod(xs, norm1_gain[layer].reshape(1, d), scale1, shift1)
        qk_gains = jnp.concatenate([fox_q_gain[layer].reshape(1, fox_width) * (HEAD_DIM ** -0.5 * LOG2_E),
                                    fox_k_gain[layer].reshape(1, fox_width)], axis=1)
        qk = _fox_proj(h, w_t, layer, qk_gains)
        v_t = _fox_v_proj(h, w_t, layer, 2 * fox_width, fox_width)
        pad = HEAD_DIM - fox_heads
        cum_f = _fox_gate(h, jnp.pad(w_t[layer, o3:o4].T, ((0, 0), (0, pad))).astype(BF16),
                          jnp.pad(fox_f_bias[layer].reshape(1, fox_heads), ((0, 0), (0, pad))), fox_heads)
        fox_out = _fox_attention(qk, v_t, cum_f, fox_heads)

        hproj = _hgrn_proj(h, w_t, layer, o4, 4 * hgrn_width, F32)
        hgrn_out = _hgrn(hproj, hgrn_lower_bounds, hgrn_out_gain[layer].reshape(1, hgrn_width),
                         hgrn_heads, layer)

        xs = _out_proj(fox_out, hgrn_out, w_out, layer, xs, gate1)

        h, h_t = _norm_mod(xs, norm2_gain[layer].reshape(1, d), scale2, shift2, with_transposed=True)
        cnt, e0, rank1, e1 = _peer_route(h, peer_w_query, layer, peer_sub_keys[layer].astype(BF16))
        w_act = _peer_act(peer_u[layer].astype(BF16), h_t, cnt, e0, rank1, e1)
        xs = _peer_down(w_act, peer_v[layer].astype(BF16), xs, gate2)
    return xs.reshape(batch, seq, d)
```

```python
import functools

import jax
import jax.numpy as jnp
from jax import lax
from jax.experimental import pallas as pl
from jax.experimental.pallas import tpu as pltpu

F32 = jnp.float32
BF16 = jnp.bfloat16

HEAD_DIM = 128
PEER_KEYS = 128
PEER_TOPK = 16
HGRN_CHUNK = 128
HGRN_HEAD_GROUP = 8
EPS = 1e-6
MASKED = -1e30
SQRT_HALF = 0.7071067811865476
LOG2_E = 1.4426950408889634
ATTN_LANE_GROUP = 256
ATTN_ONES_ROWS = 16
V7X_VMEM_LIMIT_BYTES = 52 * 1024 * 1024

_NT = (((1,), (1,)), ((), ()))


def _params(semantics):
    return pltpu.CompilerParams(dimension_semantics=semantics, vmem_limit_bytes=V7X_VMEM_LIMIT_BYTES)


def _tile(n, t):
    if n <= t:
        return n
    t -= t % HEAD_DIM
    while n % t:
        t -= HEAD_DIM
    assert t > 0, n
    return t


def _split3(x):
    hi = x.astype(BF16)
    r1 = x - hi.astype(F32)
    mid = r1.astype(BF16)
    lo = (r1 - mid.astype(F32)).astype(BF16)
    return hi, mid, lo


def _ada_body(c_ref, w_ref, b_ref, o_ref, acc_ref):
    k = pl.program_id(1)

    @pl.when(k == 0)
    def _():
        acc_ref[...] = jnp.zeros_like(acc_ref)

    c = c_ref[...]
    c_act = c * jax.nn.sigmoid(c)
    acc_ref[...] += jnp.sum(w_ref[...] * c_act, axis=0, keepdims=True)

    @pl.when(k == pl.num_programs(1) - 1)
    def _():
        o_ref[...] = acc_ref[...] + b_ref[...]


def _ada_mod(c_col, w, b_row):
    d, n = w.shape
    tk, tn = _tile(d, 512), _tile(n, 2048)
    return pl.pallas_call(
        _ada_body,
        grid=(n // tn, d // tk),
        in_specs=[pl.BlockSpec((tk, 1), lambda j, k: (k, 0)),
                  pl.BlockSpec((tk, tn), lambda j, k: (k, j)),
                  pl.BlockSpec((1, tn), lambda j, k: (0, j))],
        out_specs=pl.BlockSpec((1, tn), lambda j, k: (0, j)),
        out_shape=jax.ShapeDtypeStruct((1, n), F32),
        scratch_shapes=[pltpu.VMEM((1, tn), F32)],
        compiler_params=_params(("parallel", "arbitrary")),
        name="ada_mod",
    )(c_col, w, b_row)


def _norm_mod_body(x_ref, g_ref, sc_ref, sh_ref, o_ref, *maybe_ot_ref):
    x = x_ref[...]
    y = x * lax.rsqrt(jnp.mean(x * x, axis=-1, keepdims=True) + EPS)
    h = y * g_ref[...] * (1.0 + sc_ref[...]) + sh_ref[...]
    o_ref[...] = h.astype(o_ref.dtype)
    for ot_ref in maybe_ot_ref:
        ot_ref[...] = h.T.astype(ot_ref.dtype)


def _norm_mod(x, gain, scale, shift, with_transposed=False):
    s, d = x.shape
    tm = _tile(s, 256)
    row = pl.BlockSpec((1, d), lambda i: (0, 0))
    out_specs = [pl.BlockSpec((tm, d), lambda i: (i, 0))]
    out_shape = [jax.ShapeDtypeStruct((s, d), BF16)]
    if with_transposed:
        out_specs.append(pl.BlockSpec((d, tm), lambda i: (0, i)))
        out_shape.append(jax.ShapeDtypeStruct((d, s), BF16))
    outs = pl.pallas_call(
        _norm_mod_body,
        grid=(s // tm,),
        in_specs=[pl.BlockSpec((tm, d), lambda i: (i, 0)), row, row, row],
        out_specs=out_specs,
        out_shape=out_shape,
        compiler_params=_params(("parallel",)),
        name="norm_mod",
    )(x, gain, scale, shift)
    return outs if with_transposed else outs[0]


def _fox_proj_body(h_ref, w_ref, g_ref, o_ref):
    y = lax.dot_general(h_ref[...], w_ref[0].astype(BF16), _NT, preferred_element_type=F32)
    g = g_ref[...]
    for hh in range(y.shape[1] // HEAD_DIM):
        sl = slice(hh * HEAD_DIM, (hh + 1) * HEAD_DIM)
        yh = y[:, sl]
        r = lax.rsqrt(jnp.mean(yh * yh, axis=-1, keepdims=True) + EPS)
        o_ref[:, sl] = (yh * r * g[:, sl]).astype(o_ref.dtype)


def _w_rows_spec(layer, first_row, tn, d):
    assert first_row % 8 == 0 and tn % 8 == 0
    return pl.BlockSpec((pl.Element(1), pl.Element(tn), pl.Element(d)),
                        lambda i, j: (layer, pl.multiple_of(first_row + j * tn, 8), 0))


def _fox_proj(h, w_t, layer, gains):
    s, d = h.shape
    n = gains.shape[1]
    tm, tn = _tile(s, 1024), _tile(n, 512)
    return pl.pallas_call(
        _fox_proj_body,
        grid=(s // tm, n // tn),
        in_specs=[pl.BlockSpec((tm, d), lambda i, j: (i, 0)),
                  _w_rows_spec(layer, 0, tn, d),
                  pl.BlockSpec((1, tn), lambda i, j: (0, j))],
        out_specs=pl.BlockSpec((tm, tn), lambda i, j: (i, j)),
        out_shape=jax.ShapeDtypeStruct((s, n), BF16),
        compiler_params=_params(("parallel", "arbitrary")),
        name="fox_proj",
    )(h, w_t, gains)


def _fox_v_proj_body(h_ref, w_ref, o_ref):
    o_ref[...] = lax.dot_general(w_ref[0].astype(BF16), h_ref[...], _NT,
                                 preferred_element_type=F32).astype(o_ref.dtype)


def _fox_v_proj(h, w_t, layer, first_row, n):
    s, d = h.shape
    tm, tn = _tile(s, 1024), _tile(n, 512)
    return pl.pallas_call(
        _fox_v_proj_body,
        grid=(s // tm, n // tn),
        in_specs=[pl.BlockSpec((tm, d), lambda i, j: (i, 0)),
                  _w_rows_spec(layer, first_row, tn, d)],
        out_specs=pl.BlockSpec((tn, tm), lambda i, j: (j, i)),
        out_shape=jax.ShapeDtypeStruct((n, s), BF16),
        compiler_params=_params(("parallel", "arbitrary")),
        name="fox_v_proj",
    )(h, w_t)


def _hgrn_proj_body(h_ref, w_ref, o_ref):
    o_ref[...] = lax.dot_general(h_ref[...], w_ref[0].astype(BF16), _NT,
                                 preferred_element_type=F32).astype(o_ref.dtype)


def _hgrn_proj(h, w_t, layer, first_row, n, out_dtype):
    s, d = h.shape
    tm, tn = _tile(s, 1024), _tile(n, 512)
    return pl.pallas_call(
        _hgrn_proj_body,
        grid=(s // tm, n // tn),
        in_specs=[pl.BlockSpec((tm, d), lambda i, j: (i, 0)),
                  _w_rows_spec(layer, first_row, tn, d)],
        out_specs=pl.BlockSpec((tm, tn), lambda i, j: (i, j)),
        out_shape=jax.ShapeDtypeStruct((s, n), out_dtype),
        compiler_params=_params(("parallel", "arbitrary")),
        name="hgrn_proj",
    )(h, w_t)


def _fox_gate_body(h_ref, w_ref, b_ref, o_ref, carry_ref):
    i = pl.program_id(0)

    @pl.when(i == 0)
    def _():
        carry_ref[...] = jnp.zeros_like(carry_ref)

    logit = jnp.dot(h_ref[...], w_ref[...], preferred_element_type=F32) + b_ref[...]
    log_f = jnp.minimum(logit, 0.0) - jnp.log1p(jnp.exp(-jnp.abs(logit)))
    ts = log_f.shape[0]
    lower = (lax.broadcasted_iota(jnp.int32, (ts, ts), 1)
             <= lax.broadcasted_iota(jnp.int32, (ts, ts), 0)).astype(BF16)
    csum = sum(jnp.dot(lower, part, preferred_element_type=F32) for part in _split3(log_f))
    csum = csum + carry_ref[...]
    carry_ref[...] = csum[ts - 1:ts, :]
    scaled = csum * LOG2_E
    for hd in range(o_ref.shape[0]):
        o_ref[hd] = jnp.broadcast_to(scaled[:, hd:hd + 1], (ts, HEAD_DIM))


def _fox_gate(h, w_pad, bias_pad, n_heads):
    s, d = h.shape
    ts = _tile(s, 512)
    return pl.pallas_call(
        _fox_gate_body,
        grid=(s // ts,),
        in_specs=[pl.BlockSpec((ts, d), lambda i: (i, 0)),
                  pl.BlockSpec((d, HEAD_DIM), lambda i: (0, 0)),
                  pl.BlockSpec((1, HEAD_DIM), lambda i: (0, 0))],
        out_specs=pl.BlockSpec((n_heads, ts, HEAD_DIM), lambda i: (0, i, 0)),
        out_shape=jax.ShapeDtypeStruct((n_heads, s, HEAD_DIM), F32),
        scratch_shapes=[pltpu.VMEM((1, HEAD_DIM), F32)],
        compiler_params=_params(("arbitrary",)),
        name="fox_gate",
    )(h, w_pad, bias_pad)


def _fox_attn_body(q_ref, k_ref, vt_ref, cb_ref, o_ref, sa_ref, sb_ref, m_ref, acc_ref, *, t):
    i = pl.program_id(1)
    q = q_ref[...]
    q_start = pl.multiple_of(i * t, t)
    c_first = cb_ref[pl.ds(q_start, 8), :][0:1]
    m_ref[...] = jnp.full_like(m_ref, MASKED)
    acc_ref[...] = jnp.zeros_like(acc_ref)
    group = min(t, ATTN_LANE_GROUP)
    ones_rows = jnp.ones((ATTN_ONES_ROWS, t), BF16)

    def scores(j):
        k_start = pl.multiple_of(j * t, t)
        return lax.dot_general(k_ref[pl.ds(k_start, t), :], q, _NT, preferred_element_type=F32)

    def consume(j, sc_ref, masked):
        k_start = pl.multiple_of(j * t, t)
        bias = c_first - cb_ref[pl.ds(k_start, t), :]
        bias = jnp.concatenate([bias] * (group // HEAD_DIM), axis=1)
        v_t = jnp.concatenate([vt_ref[:, pl.ds(k_start, t)], ones_rows], axis=0)
        for g in range(t // group):
            lanes = slice(g * group, (g + 1) * group)
            s = sc_ref[:, lanes] + bias
            if masked:
                k_pos = lax.broadcasted_iota(jnp.int32, (t, group), 0)
                q_pos = lax.broadcasted_iota(jnp.int32, (t, group), 1) + g * group
                s = jnp.where(k_pos <= q_pos, s, MASKED)
            m_old = m_ref[:, lanes]
            m_new = jnp.maximum(m_old, jnp.max(s, axis=0, keepdims=True))
            alpha = jnp.exp2(m_old - m_new)
            p = jnp.exp2(s - m_new)
            acc_ref[:, lanes] = alpha * acc_ref[:, lanes] + jnp.dot(
                v_t, p.astype(BF16), preferred_element_type=F32)
            m_ref[:, lanes] = m_new

    sa_ref[...] = scores(0)

    def pair(jj, carry):
        j = 2 * jj
        sb_ref[...] = scores(j + 1)
        consume(j, sa_ref, masked=False)
        sa_ref[...] = scores(j + 2)
        consume(j + 1, sb_ref, masked=False)
        return carry

    lax.fori_loop(0, lax.shift_right_logical(i, 1), pair, 0)
    odd = (i & 1) == 1

    @pl.when(jnp.logical_not(odd))
    def _():
        consume(i, sa_ref, masked=True)

    @pl.when(odd)
    def _():
        sb_ref[...] = scores(i)
        consume(i - 1, sa_ref, masked=False)
        consume(i, sb_ref, masked=True)

    o_ref[...] = (acc_ref[:HEAD_DIM, :] / acc_ref[HEAD_DIM:HEAD_DIM + 1, :]).T.astype(o_ref.dtype)


def _fox_attention(qk, v_t, cum_f, n_heads):
    s = qk.shape[0]
    t = _tile(s, 512)
    return pl.pallas_call(
        functools.partial(_fox_attn_body, t=t),
        grid=(n_heads, s // t),
        in_specs=[pl.BlockSpec((t, HEAD_DIM), lambda h, i: (i, h)),
                  pl.BlockSpec((s, HEAD_DIM), lambda h, i: (0, n_heads + h)),
                  pl.BlockSpec((HEAD_DIM, s), lambda h, i: (h, 0)),
                  pl.BlockSpec((None, s, HEAD_DIM), lambda h, i: (h, 0, 0))],
        out_specs=pl.BlockSpec((t, HEAD_DIM), lambda h, i: (i, h)),
        out_shape=jax.ShapeDtypeStruct((s, n_heads * HEAD_DIM), BF16),
        scratch_shapes=[pltpu.VMEM((t, t), F32), pltpu.VMEM((t, t), F32),
                        pltpu.VMEM((1, t), F32),
                        pltpu.VMEM((HEAD_DIM + ATTN_ONES_ROWS, t), F32)],
        compiler_params=_params(("parallel", "arbitrary")),
        name="fox_attention",
    )(qk, qk, v_t, cum_f)


def _hgrn_head(hq, hf, val, hg, lb, gain, state_t):
    forget = lb + (1.0 - lb) * jax.nn.sigmoid(hf)
    log_f = jnp.log(forget)
    key = 1.0 - forget
    query = hq * jax.nn.sigmoid(hq) * (HEAD_DIM ** -0.5)
    val_b = val.astype(BF16)
    c = log_f.shape[0]

    row = lax.broadcasted_iota(jnp.int32, (c, c), 0)
    col = lax.broadcasted_iota(jnp.int32, (c, c), 1)
    lower = (col <= row).astype(BF16)
    b = sum(jnp.dot(lower, part, preferred_element_type=F32) for part in _split3(log_f))
    b_last = b[c - 1:c, :]

    out = lax.dot_general((query * jnp.exp(b)).astype(BF16), state_t.astype(BF16), _NT,
                          preferred_element_type=F32)

    t_idx = lax.broadcasted_iota(jnp.int32, (c, HEAD_DIM), 0)
    sep = row ^ col
    scores = jnp.where(sep == 0, lax.dot_general(query.astype(BF16), key.astype(BF16), _NT,
                                                 preferred_element_type=F32), 0.0)
    half = c // 2
    while half >= 1:
        blk = 2 * half
        if half >= 4:
            ref_rows = jnp.broadcast_to(b.reshape(c // blk, blk, HEAD_DIM)[:, half - 1:half, :],
                                        (c // blk, blk, HEAD_DIM)).reshape(c, HEAD_DIM)
        elif half == 2:
            pos = t_idx & 3
            ref_rows = jnp.where(pos == 0, pltpu.roll(b, c - 1, 0),
                                 jnp.where(pos == 2, pltpu.roll(b, 1, 0),
                                           jnp.where(pos == 3, pltpu.roll(b, 2, 0), b)))
        else:
            ref_rows = jnp.where((t_idx & 1) == 1, pltpu.roll(b, 1, 0), b)
        in_upper = (t_idx & half) != 0
        x = jnp.where(in_upper, query, key) * jnp.exp(-jnp.abs(b - ref_rows))
        a_side = jnp.where(in_upper, x, 0.0).astype(BF16)
        b_side = jnp.where(in_upper, 0.0, x).astype(BF16)
        level = lax.dot_general(a_side, b_side, _NT, preferred_element_type=F32)
        scores = scores + (level if blk == c else jnp.where(sep < blk, level, 0.0))
        half //= 2
    out = out + jnp.dot(scores.astype(BF16), val_b, preferred_element_type=F32)

    key_dec = (key * jnp.exp(b_last - b)).astype(BF16)
    new_state_t = state_t * jnp.exp(b_last) + jnp.dot(val.T.astype(BF16), key_dec,
                                                      preferred_element_type=F32)

    normed = out * lax.rsqrt(jnp.mean(out * out, axis=-1, keepdims=True) + EPS) * gain
    return normed * (hg * jax.nn.sigmoid(hg)), new_state_t


def _hgrn_body(q_ref, f_ref, i_ref, g_ref, lb_ref, gain_ref, o_ref, state_ref, *, layer):
    @pl.when(pl.program_id(1) == 0)
    def _():
        state_ref[...] = jnp.zeros_like(state_ref)

    for hh in range(state_ref.shape[0]):
        sl = slice(hh * HEAD_DIM, (hh + 1) * HEAD_DIM)
        table = lb_ref[:, sl]
        e = jnp.exp(table - jnp.max(table, axis=0, keepdims=True))
        lb = jnp.sum(e[:layer + 1], axis=0, keepdims=True) / jnp.sum(e, axis=0, keepdims=True)
        out, state_ref[hh] = _hgrn_head(q_ref[:, sl], f_ref[:, sl], i_ref[:, sl], g_ref[:, sl],
                                        lb, gain_ref[:, sl], state_ref[hh])
        o_ref[:, sl] = out.astype(o_ref.dtype)


def _hgrn(proj, lb_table, out_gain, n_heads, layer):
    s = proj.shape[0]
    c = _tile(s, HGRN_CHUNK)
    n_slots = lb_table.shape[0]
    group = HGRN_HEAD_GROUP if n_heads % HGRN_HEAD_GROUP == 0 else 1
    n_groups = n_heads // group
    width = group * HEAD_DIM

    def col(block):
        return pl.BlockSpec((c, width), lambda g, t: (t, block * n_groups + g))

    return pl.pallas_call(
        functools.partial(_hgrn_body, layer=layer),
        grid=(n_groups, s // c),
        in_specs=[col(0), col(1), col(2), col(3),
                  pl.BlockSpec((n_slots, width), lambda g, t: (0, g)),
                  pl.BlockSpec((1, width), lambda g, t: (0, g))],
        out_specs=pl.BlockSpec((c, width), lambda g, t: (t, g)),
        out_shape=jax.ShapeDtypeStruct((s, n_heads * HEAD_DIM), BF16),
        scratch_shapes=[pltpu.VMEM((group, HEAD_DIM, HEAD_DIM), F32)],
        compiler_params=_params(("parallel", "arbitrary")),
        name="hgrn2",
    )(proj, proj, proj, proj, lb_table, out_gain)


def _out_proj_body(a_ref, b_ref, wa_ref, wb_ref, x_ref, g_ref, o_ref):
    mix = (jnp.dot(a_ref[...], wa_ref[0].astype(BF16), preferred_element_type=F32)
           + jnp.dot(b_ref[...], wb_ref[0].astype(BF16), preferred_element_type=F32))
    o_ref[...] = x_ref[...] + g_ref[...] * mix


def _out_proj(fox, hgrn, w_out, layer, x, gate):
    s, d = x.shape
    tm, tn = _tile(s, 1024), _tile(d, 512)
    ka, kb = fox.shape[1], hgrn.shape[1]

    def w_rows(first_row, k):
        return pl.BlockSpec((pl.Element(1), pl.Element(k), pl.Element(tn)),
                            lambda i, j: (layer, first_row, pl.multiple_of(j * tn, HEAD_DIM)))

    return pl.pallas_call(
        _out_proj_body,
        grid=(s // tm, d // tn),
        in_specs=[pl.BlockSpec((tm, ka), lambda i, j: (i, 0)),
                  pl.BlockSpec((tm, kb), lambda i, j: (i, 0)),
                  w_rows(0, ka),
                  w_rows(ka, kb),
                  pl.BlockSpec((tm, tn), lambda i, j: (i, j)),
                  pl.BlockSpec((1, tn), lambda i, j: (0, j))],
        out_specs=pl.BlockSpec((tm, tn), lambda i, j: (i, j)),
        out_shape=jax.ShapeDtypeStruct((s, d), F32),
        compiler_params=_params(("parallel", "arbitrary")),
        name="out_proj",
    )(fox, hgrn, w_out, w_out, x, gate)


PEER_NTOP = PEER_TOPK + 1
PEER_TOP_ROWS = 24
PEER_NCAND = 52


def _top_rows(arrays, k, exact_ties, ranked=()):
    vals = list(arrays)
    rows = [[] for _ in vals]
    ranks = {n: jnp.full(vals[n].shape, float(vals[n].shape[0]), F32) for n in ranked}
    for r in range(k):
        for n, v in enumerate(vals):
            m = jnp.max(v, axis=0, keepdims=True)
            rows[n].append(m)
            hit = v == m
            if exact_ties:
                idx = lax.broadcasted_iota(jnp.int32, v.shape, 0)
                first = jnp.min(jnp.where(hit, idx, v.shape[0]), axis=0, keepdims=True)
                hit = idx == first
            vals[n] = jnp.where(hit, -jnp.inf, v)
            if n in ranks:
                ranks[n] = jnp.where(hit, float(r), ranks[n])
    return rows, vals, ranks


def _n_finite(v):
    return jnp.sum(jnp.where(v > -jnp.inf, 1.0, 0.0), axis=0, keepdims=True)


def _peer_thresholds(s0s, s1s, a_ref, b_ref, exact_ties):
    nh = len(s0s)
    rows, left, ranks = _top_rows(s0s + s1s, PEER_NTOP, exact_ties, ranked=range(nh, 2 * nh))
    excess = sum(jnp.abs(_n_finite(v) - (PEER_KEYS - PEER_NTOP)) for v in left)

    r24 = lax.broadcasted_iota(jnp.int32, a_ref.shape[1:], 0)
    r8 = lax.broadcasted_iota(jnp.int32, (8, a_ref.shape[2]), 0)
    ninf = -jnp.inf

    def grp(x, r, lo, hi):
        return jnp.where((r >= lo) & (r <= hi), x, ninf)

    cands = []
    for hd in range(nh):
        a_ref[hd] = jnp.full(a_ref.shape[1:], ninf, F32)
        b_ref[hd] = jnp.full(b_ref.shape[1:], ninf, F32)
        for r in range(PEER_NTOP):
            a_ref[hd, r:r + 1, :] = rows[hd][r]
            b_ref[hd, r:r + 1, :] = rows[nh + hd][r]
        a24, b24 = a_ref[hd], b_ref[hd]
        a8, b8 = a24[:8], b24[:8]
        cands.append(jnp.concatenate([
            grp(a24[0:1] + b24, r24, 0, 16),
            grp(a24 + b24[0:1], r24, 1, 16),
            grp(a8[1:2] + b8, r8, 1, 7),
            grp(a8 + b8[1:2], r8, 2, 7),
            grp(a8[2:3] + b8, r8, 2, 4),
            grp(a8 + b8[2:3], r8, 3, 4),
            grp(a8[3:4] + b8, r8, 3, 3),
        ], axis=0))
    tops, left, _ = _top_rows(cands, PEER_NTOP, exact_ties)
    excess = excess + sum(jnp.abs(_n_finite(v) - (PEER_NCAND - PEER_NTOP)) for v in left)

    outs = []
    for hd in range(nh):
        top = tops[hd]
        tau = 0.5 * (top[PEER_TOPK - 1] + top[PEER_TOPK])
        z = sum(jnp.exp(t - top[0]) for t in top[:PEER_TOPK])
        thr = tau - s0s[hd]
        cnt = sum(jnp.where(b_r >= thr, 1.0, 0.0) for b_r in rows[nh + hd])
        outs.append((cnt,
                     jnp.exp(s0s[hd] - rows[hd][0]) * (0.5 / z),
                     ranks[nh + hd],
                     jnp.exp(s1s[hd] - rows[nh + hd][0])))
    return outs, excess


def _peer_route_body(h_ref, wq_ref, sk_ref, cnt_ref, e0_ref, rank_ref, e1_ref, a_ref, b_ref):
    nh = sk_ref.shape[0]
    q = jnp.dot(h_ref[...], wq_ref[...].astype(BF16), preferred_element_type=F32).astype(BF16)
    s0s, s1s = [], []
    for hd in range(nh):
        c0 = 2 * hd * PEER_KEYS
        s0s.append(lax.dot_general(sk_ref[hd, 0], q[:, c0:c0 + PEER_KEYS], _NT,
                                   preferred_element_type=F32))
        s1s.append(lax.dot_general(sk_ref[hd, 1], q[:, c0 + PEER_KEYS:c0 + 2 * PEER_KEYS], _NT,
                                   preferred_element_type=F32))

    def emit(outs):
        for hd, (cnt, e0, rank1, e1) in enumerate(outs):
            cnt_ref[hd] = cnt
            e0_ref[hd] = e0
            rank_ref[hd] = rank1.astype(rank_ref.dtype)
            e1_ref[hd] = e1.astype(e1_ref.dtype)

    outs, excess = _peer_thresholds(s0s, s1s, a_ref, b_ref, exact_ties=False)
    emit(outs)

    @pl.when(jnp.max(excess) > 0.0)
    def _():
        emit(_peer_thresholds(s0s, s1s, a_ref, b_ref, exact_ties=True)[0])


def _peer_route(h, wq, layer, sub_keys):
    s, d = h.shape
    ph = sub_keys.shape[0]
    tt = _tile(s, 256)
    nh = 2 if ph % 2 == 0 else 1
    out_spec = pl.BlockSpec((nh, PEER_KEYS, tt), lambda p, t: (p, 0, t))
    out_sds = [jax.ShapeDtypeStruct((ph, PEER_KEYS, s), dt) for dt in (F32, F32, BF16, BF16)]
    return pl.pallas_call(
        _peer_route_body,
        grid=(ph // nh, s // tt),
        in_specs=[pl.BlockSpec((tt, d), lambda p, t: (t, 0)),
                  pl.BlockSpec((None, d, nh * 2 * PEER_KEYS), lambda p, t: (layer, 0, p)),
                  pl.BlockSpec((nh, 2, PEER_KEYS, PEER_KEYS), lambda p, t: (p, 0, 0, 0))],
        out_specs=[out_spec] * 4,
        out_shape=out_sds,
        scratch_shapes=[pltpu.VMEM((nh, PEER_TOP_ROWS, tt), F32), pltpu.VMEM((nh, PEER_TOP_ROWS, tt), F32)],
        compiler_params=_params(("parallel", "arbitrary")),
        name="peer_route",
    )(h, wq, sub_keys)


def _bf16_rows(row):
    tile = jnp.broadcast_to(row, (16, row.shape[1])).astype(BF16)
    return jnp.concatenate([tile] * (PEER_KEYS // 16), axis=0)


def _peer_act_body(u_ref, ht_ref, cnt_ref, e0_ref, rank_ref, e1_ref, w_ref, *, rows):
    n_heads = rank_ref.shape[0]
    te = u_ref.shape[0]
    for ch in range(te // rows):
        z = jnp.dot(u_ref[ch * rows:(ch + 1) * rows, :], ht_ref[...], preferred_element_type=F32)
        for ci in range(rows // PEER_KEYS):
            ii = ch * (rows // PEER_KEYS) + ci
            zi = z[ci * PEER_KEYS:(ci + 1) * PEER_KEYS, :]
            act = zi * (1.0 + lax.erf(zi * SQRT_HALF))
            gate = jnp.zeros(zi.shape, BF16)
            for hd in range(n_heads):
                sel = rank_ref[hd] < _bf16_rows(cnt_ref[hd, ii:ii + 1, :])
                gate = gate + jnp.where(sel, e1_ref[hd] * _bf16_rows(e0_ref[hd, ii:ii + 1, :]),
                                        jnp.zeros((), BF16))
            w_ref[:, ii * PEER_KEYS:(ii + 1) * PEER_KEYS] = (act * gate.astype(F32)).T.astype(w_ref.dtype)


def _peer_act(u, h_t, cnt, e0, rank, e1):
    n_exp, d = u.shape
    s = h_t.shape[1]
    ph = cnt.shape[0]
    te = 8 * PEER_KEYS
    tt = _tile(s, 512)
    assert n_exp % te == 0
    per_i = pl.BlockSpec((ph, 8, tt), lambda t, e: (0, e, t))
    per_j = pl.BlockSpec((ph, PEER_KEYS, tt), lambda t, e: (0, 0, t))
    return pl.pallas_call(
        functools.partial(_peer_act_body, rows=te // 2),
        grid=(s // tt, n_exp // te),
        in_specs=[pl.BlockSpec((te, d), lambda t, e: (e, 0)),
                  pl.BlockSpec((d, tt), lambda t, e: (0, t)),
                  per_i, per_i, per_j, per_j],
        out_specs=pl.BlockSpec((tt, te), lambda t, e: (t, e)),
        out_shape=jax.ShapeDtypeStruct((s, n_exp), BF16),
        compiler_params=_params(("parallel", "arbitrary")),
        name="peer_act",
    )(u, h_t, cnt, e0, rank, e1)


def _peer_down_body(w_ref, v_ref, x_ref, g_ref, o_ref, acc_ref):
    k = pl.program_id(2)

    @pl.when(k == 0)
    def _():
        acc_ref[...] = jnp.zeros_like(acc_ref)

    acc_ref[...] += jnp.dot(w_ref[...], v_ref[...], preferred_element_type=F32)

    @pl.when(k == pl.num_programs(2) - 1)
    def _():
        o_ref[...] = x_ref[...] + g_ref[...] * acc_ref[...]


def _peer_down(w, v, x, gate):
    s, n_exp = w.shape
    d = v.shape[1]
    tm, tn, tk = _tile(s, 1024), _tile(d, 1024), _tile(n_exp, 2048)
    return pl.pallas_call(
        _peer_down_body,
        grid=(s // tm, d // tn, n_exp // tk),
        in_specs=[pl.BlockSpec((tm, tk), lambda i, j, k: (i, k)),
                  pl.BlockSpec((tk, tn), lambda i, j, k: (k, j)),
                  pl.BlockSpec((tm, tn), lambda i, j, k: (i, j)),
                  pl.BlockSpec((1, tn), lambda i, j, k: (0, j))],
        out_specs=pl.BlockSpec((tm, tn), lambda i, j, k: (i, j)),
        out_shape=jax.ShapeDtypeStruct((s, d), F32),
        scratch_shapes=[pltpu.VMEM((tm, tn), F32)],
        compiler_params=_params(("parallel", "parallel", "arbitrary")),
        name="peer_down",
    )(w, v, x, gate)


def kernel(x, c, ada_w, ada_b, norm1_gain, norm2_gain, w_in, fox_f_bias, fox_q_gain, fox_k_gain,
           hgrn_lower_bounds, hgrn_out_gain, w_out, peer_w_query, peer_sub_keys, peer_u, peer_v):
    batch, seq, d = x.shape
    assert batch == 1, "one sequence per call"
    depth = ada_w.shape[0]
    fox_heads = fox_f_bias.shape[-1]
    hgrn_heads = hgrn_out_gain.shape[1]
    fox_width, hgrn_width = fox_heads * HEAD_DIM, hgrn_heads * HEAD_DIM
    assert fox_q_gain.shape[-1] == HEAD_DIM and hgrn_out_gain.shape[-1] == HEAD_DIM
    assert w_in.shape[-1] == 3 * fox_width + fox_heads + 4 * hgrn_width
    assert peer_sub_keys.shape[2:] == (2, PEER_KEYS, PEER_KEYS) and peer_u.shape[1] == PEER_KEYS ** 2
    o3 = 3 * fox_width
    o4 = o3 + fox_heads

    xs = x.reshape(seq, d)
    c_col = c.reshape(d, 1)
    w_t = jnp.swapaxes(w_in, 1, 2)
    for layer in range(depth):
        mod = _ada_mod(c_col, ada_w[layer], ada_b[layer].reshape(1, -1))
        shift1, scale1, gate1, shift2, scale2, gate2 = (mod[:, n * d:(n + 1) * d] for n in range(6))

        h = _norm_mod(xs, norm1_gain[layer].reshape(1, d), scale1, shift1)
        qk_gains = jnp.concatenate([fox_q_gain[layer].reshape(1, fox_width) * (HEAD_DIM ** -0.5 * LOG2_E),
                                    fox_k_gain[layer].reshape(1, fox_width)], axis=1)
        qk = _fox_proj(h, w_t, layer, qk_gains)
        v_t = _fox_v_proj(h, w_t, layer, 2 * fox_width, fox_width)
        pad = HEAD_DIM - fox_heads
        cum_f = _fox_gate(h, jnp.pad(w_t[layer, o3:o4].T, ((0, 0), (0, pad))).astype(BF16),
                          jnp.pad(fox_f_bias[layer].reshape(1, fox_heads), ((0, 0), (0, pad))), fox_heads)
        fox_out = _fox_attention(qk, v_t, cum_f, fox_heads)

        hproj = _hgrn_proj(h, w_t, layer, o4, 4 * hgrn_width, F32)
        hgrn_out = _hgrn(hproj, hgrn_lower_bounds, hgrn_out_gain[layer].reshape(1, hgrn_width),
                         hgrn_heads, layer)

        xs = _out_proj(fox_out, hgrn_out, w_out, layer, xs, gate1)

        h, h_t = _norm_mod(xs, norm2_gain[layer].reshape(1, d), scale2, shift2, with_transposed=True)
        cnt, e0, rank1, e1 = _peer_route(h, peer_w_query, layer, peer_sub_keys[layer].astype(BF16))
        w_act = _peer_act(peer_u[layer].astype(BF16), h_t, cnt, e0, rank1, e1)
        xs = _peer_down(w_act, peer_v[layer].astype(BF16), xs, gate2)
    return xs.reshape(batch, seq, d)
```

```python
import functools

import jax
import jax.numpy as jnp
from jax import lax
from jax.experimental import pallas as pl
from jax.experimental.pallas import tpu as pltpu

F32 = jnp.float32
BF16 = jnp.bfloat16

HEAD_DIM = 128
PEER_KEYS = 128
PEER_TOPK = 16
HGRN_CHUNK = 128
HGRN_HEAD_GROUP = 8
EPS = 1e-6
MASKED = -1e30
SQRT_HALF = 0.7071067811865476
LOG2_E = 1.4426950408889634
ATTN_LANE_GROUP = 256
ATTN_ONES_ROWS = 16
PEER_ACT_CHUNKS = 2
V7X_VMEM_LIMIT_BYTES = 52 * 1024 * 1024

_NT = (((1,), (1,)), ((), ()))


def _params(semantics):
    return pltpu.CompilerParams(dimension_semantics=semantics, vmem_limit_bytes=V7X_VMEM_LIMIT_BYTES)


def _tile(n, t):
    if n <= t:
        return n
    t -= t % HEAD_DIM
    while n % t:
        t -= HEAD_DIM
    assert t > 0, n
    return t


def _split3(x):
    hi = x.astype(BF16)
    r1 = x - hi.astype(F32)
    mid = r1.astype(BF16)
    lo = (r1 - mid.astype(F32)).astype(BF16)
    return hi, mid, lo


def _ada_body(c_ref, w_ref, b_ref, o_ref, acc_ref):
    k = pl.program_id(1)

    @pl.when(k == 0)
    def _():
        acc_ref[...] = jnp.zeros_like(acc_ref)

    c = c_ref[...]
    c_act = c * jax.nn.sigmoid(c)
    acc_ref[...] += jnp.sum(w_ref[...] * c_act, axis=0, keepdims=True)

    @pl.when(k == pl.num_programs(1) - 1)
    def _():
        o_ref[...] = acc_ref[...] + b_ref[...]


def _ada_mod(c_col, w, b_row):
    d, n = w.shape
    tk, tn = _tile(d, 512), _tile(n, 2048)
    return pl.pallas_call(
        _ada_body,
        grid=(n // tn, d // tk),
        in_specs=[pl.BlockSpec((tk, 1), lambda j, k: (k, 0)),
                  pl.BlockSpec((tk, tn), lambda j, k: (k, j)),
                  pl.BlockSpec((1, tn), lambda j, k: (0, j))],
        out_specs=pl.BlockSpec((1, tn), lambda j, k: (0, j)),
        out_shape=jax.ShapeDtypeStruct((1, n), F32),
        scratch_shapes=[pltpu.VMEM((1, tn), F32)],
        compiler_params=_params(("parallel", "arbitrary")),
        name="ada_mod",
    )(c_col, w, b_row)


def _norm_mod_body(x_ref, g_ref, sc_ref, sh_ref, o_ref, *maybe_ot_ref):
    x = x_ref[...]
    y = x * lax.rsqrt(jnp.mean(x * x, axis=-1, keepdims=True) + EPS)
    h = y * g_ref[...] * (1.0 + sc_ref[...]) + sh_ref[...]
    o_ref[...] = h.astype(o_ref.dtype)
    for ot_ref in maybe_ot_ref:
        ot_ref[...] = h.T.astype(ot_ref.dtype)


def _norm_mod(x, gain, scale, shift, with_transposed=False):
    s, d = x.shape
    tm = _tile(s, 256)
    row = pl.BlockSpec((1, d), lambda i: (0, 0))
    out_specs = [pl.BlockSpec((tm, d), lambda i: (i, 0))]
    out_shape = [jax.ShapeDtypeStruct((s, d), BF16)]
    if with_transposed:
        out_specs.append(pl.BlockSpec((d, tm), lambda i: (0, i)))
        out_shape.append(jax.ShapeDtypeStruct((d, s), BF16))
    outs = pl.pallas_call(
        _norm_mod_body,
        grid=(s // tm,),
        in_specs=[pl.BlockSpec((tm, d), lambda i: (i, 0)), row, row, row],
        out_specs=out_specs,
        out_shape=out_shape,
        compiler_params=_params(("parallel",)),
        name="norm_mod",
    )(x, gain, scale, shift)
    return outs if with_transposed else outs[0]


def _fox_proj_body(h_ref, w_ref, g_ref, o_ref):
    y = lax.dot_general(h_ref[...], w_ref[0].astype(BF16), _NT, preferred_element_type=F32)
    g = g_ref[...]
    for hh in range(y.shape[1] // HEAD_DIM):
        sl = slice(hh * HEAD_DIM, (hh + 1) * HEAD_DIM)
        yh = y[:, sl]
        r = lax.rsqrt(jnp.mean(yh * yh, axis=-1, keepdims=True) + EPS)
        o_ref[:, sl] = (yh * r * g[:, sl]).astype(o_ref.dtype)


def _w_rows_spec(layer, first_row, tn, d):
    assert first_row % 8 == 0 and tn % 8 == 0
    return pl.BlockSpec((pl.Element(1), pl.Element(tn), pl.Element(d)),
                        lambda i, j: (layer, pl.multiple_of(first_row + j * tn, 8), 0))


def _fox_proj(h, w_t, layer, gains):
    s, d = h.shape
    n = gains.shape[1]
    tm, tn = _tile(s, 1024), _tile(n, 512)
    return pl.pallas_call(
        _fox_proj_body,
        grid=(s // tm, n // tn),
        in_specs=[pl.BlockSpec((tm, d), lambda i, j: (i, 0)),
                  _w_rows_spec(layer, 0, tn, d),
                  pl.BlockSpec((1, tn), lambda i, j: (0, j))],
        out_specs=pl.BlockSpec((tm, tn), lambda i, j: (i, j)),
        out_shape=jax.ShapeDtypeStruct((s, n), BF16),
        compiler_params=_params(("parallel", "arbitrary")),
        name="fox_proj",
    )(h, w_t, gains)


def _fox_v_proj_body(h_ref, w_ref, o_ref):
    o_ref[...] = lax.dot_general(w_ref[0].astype(BF16), h_ref[...], _NT,
                                 preferred_element_type=F32).astype(o_ref.dtype)


def _fox_v_proj(h, w_t, layer, first_row, n):
    s, d = h.shape
    tm, tn = _tile(s, 1024), _tile(n, 512)
    return pl.pallas_call(
        _fox_v_proj_body,
        grid=(s // tm, n // tn),
        in_specs=[pl.BlockSpec((tm, d), lambda i, j: (i, 0)),
                  _w_rows_spec(layer, first_row, tn, d)],
        out_specs=pl.BlockSpec((tn, tm), lambda i, j: (j, i)),
        out_shape=jax.ShapeDtypeStruct((n, s), BF16),
        compiler_params=_params(("parallel", "arbitrary")),
        name="fox_v_proj",
    )(h, w_t)


def _hgrn_proj_body(h_ref, w_ref, o_ref):
    o_ref[...] = lax.dot_general(h_ref[...], w_ref[0].astype(BF16), _NT,
                                 preferred_element_type=F32).astype(o_ref.dtype)


def _hgrn_proj(h, w_t, layer, first_row, n, out_dtype):
    s, d = h.shape
    tm, tn = _tile(s, 1024), _tile(n, 512)
    return pl.pallas_call(
        _hgrn_proj_body,
        grid=(s // tm, n // tn),
        in_specs=[pl.BlockSpec((tm, d), lambda i, j: (i, 0)),
                  _w_rows_spec(layer, first_row, tn, d)],
        out_specs=pl.BlockSpec((tm, tn), lambda i, j: (i, j)),
        out_shape=jax.ShapeDtypeStruct((s, n), out_dtype),
        compiler_params=_params(("parallel", "arbitrary")),
        name="hgrn_proj",
    )(h, w_t)


def _fox_gate_body(h_ref, w_ref, b_ref, o_ref, carry_ref):
    i = pl.program_id(0)

    @pl.when(i == 0)
    def _():
        carry_ref[...] = jnp.zeros_like(carry_ref)

    logit = jnp.dot(h_ref[...], w_ref[...], preferred_element_type=F32) + b_ref[...]
    log_f = jnp.minimum(logit, 0.0) - jnp.log1p(jnp.exp(-jnp.abs(logit)))
    ts = log_f.shape[0]
    lower = (lax.broadcasted_iota(jnp.int32, (ts, ts), 1)
             <= lax.broadcasted_iota(jnp.int32, (ts, ts), 0)).astype(BF16)
    csum = sum(jnp.dot(lower, part, preferred_element_type=F32) for part in _split3(log_f))
    csum = csum + carry_ref[...]
    carry_ref[...] = csum[ts - 1:ts, :]
    scaled = csum * LOG2_E
    for hd in range(o_ref.shape[0]):
        o_ref[hd] = jnp.broadcast_to(scaled[:, hd:hd + 1], (ts, HEAD_DIM))


def _fox_gate(h, w_pad, bias_pad, n_heads):
    s, d = h.shape
    ts = _tile(s, 512)
    return pl.pallas_call(
        _fox_gate_body,
        grid=(s // ts,),
        in_specs=[pl.BlockSpec((ts, d), lambda i: (i, 0)),
                  pl.BlockSpec((d, HEAD_DIM), lambda i: (0, 0)),
                  pl.BlockSpec((1, HEAD_DIM), lambda i: (0, 0))],
        out_specs=pl.BlockSpec((n_heads, ts, HEAD_DIM), lambda i: (0, i, 0)),
        out_shape=jax.ShapeDtypeStruct((n_heads, s, HEAD_DIM), F32),
        scratch_shapes=[pltpu.VMEM((1, HEAD_DIM), F32)],
        compiler_params=_params(("arbitrary",)),
        name="fox_gate",
    )(h, w_pad, bias_pad)


def _fox_attn_body(q_ref, k_ref, vt_ref, cb_ref, *refs, t, n_tables):
    table_refs, o_ref = refs[:n_tables], refs[n_tables]
    table_out_refs = refs[n_tables + 1:2 * n_tables + 1]
    sa_ref, sb_ref, m_ref, acc_ref = refs[2 * n_tables + 1:]
    for src_ref, dst_ref in zip(table_refs, table_out_refs):
        dst_ref[...] = src_ref[...].astype(dst_ref.dtype)

    i = pl.program_id(1)
    q = q_ref[...]
    q_start = pl.multiple_of(i * t, t)
    c_first = cb_ref[pl.ds(q_start, 8), :][0:1]
    m_ref[...] = jnp.full_like(m_ref, MASKED)
    acc_ref[...] = jnp.zeros_like(acc_ref)
    group = min(t, ATTN_LANE_GROUP)
    ones_rows = jnp.ones((ATTN_ONES_ROWS, t), BF16)

    def scores(j):
        k_start = pl.multiple_of(j * t, t)
        return lax.dot_general(k_ref[pl.ds(k_start, t), :], q, _NT, preferred_element_type=F32)

    def consume(j, sc_ref, masked):
        k_start = pl.multiple_of(j * t, t)
        bias = c_first - cb_ref[pl.ds(k_start, t), :]
        bias = jnp.concatenate([bias] * (group // HEAD_DIM), axis=1)
        v_t = jnp.concatenate([vt_ref[:, pl.ds(k_start, t)], ones_rows], axis=0)
        for g in range(t // group):
            lanes = slice(g * group, (g + 1) * group)
            s = sc_ref[:, lanes] + bias
            if masked:
                k_pos = lax.broadcasted_iota(jnp.int32, (t, group), 0)
                q_pos = lax.broadcasted_iota(jnp.int32, (t, group), 1) + g * group
                s = jnp.where(k_pos <= q_pos, s, MASKED)
            m_old = m_ref[:, lanes]
            m_new = jnp.maximum(m_old, jnp.max(s, axis=0, keepdims=True))
            alpha = jnp.exp2(m_old - m_new)
            p = jnp.exp2(s - m_new)
            acc_ref[:, lanes] = alpha * acc_ref[:, lanes] + jnp.dot(
                v_t, p.astype(BF16), preferred_element_type=F32)
            m_ref[:, lanes] = m_new

    sa_ref[...] = scores(0)

    def pair(jj, carry):
        j = 2 * jj
        sb_ref[...] = scores(j + 1)
        consume(j, sa_ref, masked=False)
        sa_ref[...] = scores(j + 2)
        consume(j + 1, sb_ref, masked=False)
        return carry

    lax.fori_loop(0, lax.shift_right_logical(i, 1), pair, 0)
    odd = (i & 1) == 1

    @pl.when(jnp.logical_not(odd))
    def _():
        consume(i, sa_ref, masked=True)

    @pl.when(odd)
    def _():
        sb_ref[...] = scores(i)
        consume(i - 1, sa_ref, masked=False)
        consume(i, sb_ref, masked=True)

    o_ref[...] = (acc_ref[:HEAD_DIM, :] / acc_ref[HEAD_DIM:HEAD_DIM + 1, :]).T.astype(o_ref.dtype)


def _fox_attention(qk, v_t, cum_f, n_heads, tables, layer):
    s = qk.shape[0]
    t = _tile(s, 512)
    n_q = s // t
    steps = n_heads * n_q
    table_in, table_out, table_shape = [], [], []
    for tab in tables:
        rows, d = tab.shape[1:]
        assert rows % steps == 0 and (rows // steps) % 16 == 0, (rows, steps)
        table_in.append(pl.BlockSpec((None, rows // steps, d), lambda h, i: (layer, h * n_q + i, 0)))
        table_out.append(pl.BlockSpec((rows // steps, d), lambda h, i: (h * n_q + i, 0)))
        table_shape.append(jax.ShapeDtypeStruct((rows, d), BF16))
    return pl.pallas_call(
        functools.partial(_fox_attn_body, t=t, n_tables=len(tables)),
        grid=(n_heads, n_q),
        in_specs=[pl.BlockSpec((t, HEAD_DIM), lambda h, i: (i, h)),
                  pl.BlockSpec((s, HEAD_DIM), lambda h, i: (0, n_heads + h)),
                  pl.BlockSpec((HEAD_DIM, s), lambda h, i: (h, 0)),
                  pl.BlockSpec((None, s, HEAD_DIM), lambda h, i: (h, 0, 0))] + table_in,
        out_specs=[pl.BlockSpec((t, HEAD_DIM), lambda h, i: (i, h))] + table_out,
        out_shape=[jax.ShapeDtypeStruct((s, n_heads * HEAD_DIM), BF16)] + table_shape,
        scratch_shapes=[pltpu.VMEM((t, t), F32), pltpu.VMEM((t, t), F32),
                        pltpu.VMEM((1, t), F32),
                        pltpu.VMEM((HEAD_DIM + ATTN_ONES_ROWS, t), F32)],
        compiler_params=_params(("parallel", "arbitrary")),
        name="fox_attention",
    )(qk, qk, v_t, cum_f, *tables)


def _hgrn_head(hq, hf, val, hg, lb, gain, state_t):
    forget = lb + (1.0 - lb) * jax.nn.sigmoid(hf)
    log_f = jnp.log(forget)
    key = 1.0 - forget
    query = hq * jax.nn.sigmoid(hq) * (HEAD_DIM ** -0.5)
    val_b = val.astype(BF16)
    c = log_f.shape[0]

    row = lax.broadcasted_iota(jnp.int32, (c, c), 0)
    col = lax.broadcasted_iota(jnp.int32, (c, c), 1)
    lower = (col <= row).astype(BF16)
    b = sum(jnp.dot(lower, part, preferred_element_type=F32) for part in _split3(log_f))
    b_last = b[c - 1:c, :]

    out = lax.dot_general((query * jnp.exp(b)).astype(BF16), state_t.astype(BF16), _NT,
                          preferred_element_type=F32)

    t_idx = lax.broadcasted_iota(jnp.int32, (c, HEAD_DIM), 0)
    sep = row ^ col
    scores = jnp.where(sep == 0, lax.dot_general(query.astype(BF16), key.astype(BF16), _NT,
                                                 preferred_element_type=F32), 0.0)
    half = c // 2
    while half >= 1:
        blk = 2 * half
        if half >= 4:
            ref_rows = jnp.broadcast_to(b.reshape(c // blk, blk, HEAD_DIM)[:, half - 1:half, :],
                                        (c // blk, blk, HEAD_DIM)).reshape(c, HEAD_DIM)
        elif half == 2:
            pos = t_idx & 3
            ref_rows = jnp.where(pos == 0, pltpu.roll(b, c - 1, 0),
                                 jnp.where(pos == 2, pltpu.roll(b, 1, 0),
                                           jnp.where(pos == 3, pltpu.roll(b, 2, 0), b)))
        else:
            ref_rows = jnp.where((t_idx & 1) == 1, pltpu.roll(b, 1, 0), b)
        in_upper = (t_idx & half) != 0
        x = jnp.where(in_upper, query, key) * jnp.exp(-jnp.abs(b - ref_rows))
        a_side = jnp.where(in_upper, x, 0.0).astype(BF16)
        b_side = jnp.where(in_upper, 0.0, x).astype(BF16)
        level = lax.dot_general(a_side, b_side, _NT, preferred_element_type=F32)
        scores = scores + (level if blk == c else jnp.where(sep < blk, level, 0.0))
        half //= 2
    out = out + jnp.dot(scores.astype(BF16), val_b, preferred_element_type=F32)

    key_dec = (key * jnp.exp(b_last - b)).astype(BF16)
    new_state_t = state_t * jnp.exp(b_last) + jnp.dot(val.T.astype(BF16), key_dec,
                                                      preferred_element_type=F32)

    normed = out * lax.rsqrt(jnp.mean(out * out, axis=-1, keepdims=True) + EPS) * gain
    return normed * (hg * jax.nn.sigmoid(hg)), new_state_t


def _hgrn_body(q_ref, f_ref, i_ref, g_ref, lb_ref, gain_ref, o_ref, state_ref, *, layer):
    @pl.when(pl.program_id(1) == 0)
    def _():
        state_ref[...] = jnp.zeros_like(state_ref)

    for hh in range(state_ref.shape[0]):
        sl = slice(hh * HEAD_DIM, (hh + 1) * HEAD_DIM)
        table = lb_ref[:, sl]
        e = jnp.exp(table - jnp.max(table, axis=0, keepdims=True))
        lb = jnp.sum(e[:layer + 1], axis=0, keepdims=True) / jnp.sum(e, axis=0, keepdims=True)
        out, state_ref[hh] = _hgrn_head(q_ref[:, sl], f_ref[:, sl], i_ref[:, sl], g_ref[:, sl],
                                        lb, gain_ref[:, sl], state_ref[hh])
        o_ref[:, sl] = out.astype(o_ref.dtype)


def _hgrn(proj, lb_table, out_gain, n_heads, layer):
    s = proj.shape[0]
    c = _tile(s, HGRN_CHUNK)
    n_slots = lb_table.shape[0]
    group = HGRN_HEAD_GROUP if n_heads % HGRN_HEAD_GROUP == 0 else 1
    n_groups = n_heads // group
    width = group * HEAD_DIM

    def col(block):
        return pl.BlockSpec((c, width), lambda g, t: (t, block * n_groups + g))

    return pl.pallas_call(
        functools.partial(_hgrn_body, layer=layer),
        grid=(n_groups, s // c),
        in_specs=[col(0), col(1), col(2), col(3),
                  pl.BlockSpec((n_slots, width), lambda g, t: (0, g)),
                  pl.BlockSpec((1, width), lambda g, t: (0, g))],
        out_specs=pl.BlockSpec((c, width), lambda g, t: (t, g)),
        out_shape=jax.ShapeDtypeStruct((s, n_heads * HEAD_DIM), BF16),
        scratch_shapes=[pltpu.VMEM((group, HEAD_DIM, HEAD_DIM), F32)],
        compiler_params=_params(("parallel", "arbitrary")),
        name="hgrn2",
    )(proj, proj, proj, proj, lb_table, out_gain)


def _out_proj_body(a_ref, b_ref, wa_ref, wb_ref, x_ref, g_ref, o_ref):
    mix = (jnp.dot(a_ref[...], wa_ref[0].astype(BF16), preferred_element_type=F32)
           + jnp.dot(b_ref[...], wb_ref[0].astype(BF16), preferred_element_type=F32))
    o_ref[...] = x_ref[...] + g_ref[...] * mix


def _out_proj(fox, hgrn, w_out, layer, x, gate):
    s, d = x.shape
    tm, tn = _tile(s, 1024), _tile(d, 512)
    ka, kb = fox.shape[1], hgrn.shape[1]

    def w_rows(first_row, k):
        return pl.BlockSpec((pl.Element(1), pl.Element(k), pl.Element(tn)),
                            lambda i, j: (layer, first_row, pl.multiple_of(j * tn, HEAD_DIM)))

    return pl.pallas_call(
        _out_proj_body,
        grid=(s // tm, d // tn),
        in_specs=[pl.BlockSpec((tm, ka), lambda i, j: (i, 0)),
                  pl.BlockSpec((tm, kb), lambda i, j: (i, 0)),
                  w_rows(0, ka),
                  w_rows(ka, kb),
                  pl.BlockSpec((tm, tn), lambda i, j: (i, j)),
                  pl.BlockSpec((1, tn), lambda i, j: (0, j))],
        out_specs=pl.BlockSpec((tm, tn), lambda i, j: (i, j)),
        out_shape=jax.ShapeDtypeStruct((s, d), F32),
        compiler_params=_params(("parallel", "arbitrary")),
        name="out_proj",
    )(fox, hgrn, w_out, w_out, x, gate)


PEER_NTOP = PEER_TOPK + 1
PEER_TOP_ROWS = 24
PEER_NCAND = 52


def _top_rows(arrays, k, exact_ties, ranked=()):
    vals = list(arrays)
    rows = [[] for _ in vals]
    ranks = {n: jnp.full(vals[n].shape, float(vals[n].shape[0]), F32) for n in ranked}
    for r in range(k):
        for n, v in enumerate(vals):
            m = jnp.max(v, axis=0, keepdims=True)
            rows[n].append(m)
            hit = v == m
            if exact_ties:
                idx = lax.broadcasted_iota(jnp.int32, v.shape, 0)
                first = jnp.min(jnp.where(hit, idx, v.shape[0]), axis=0, keepdims=True)
                hit = idx == first
            vals[n] = jnp.where(hit, -jnp.inf, v)
            if n in ranks:
                ranks[n] = jnp.where(hit, float(r), ranks[n])
    return rows, vals, ranks


def _n_finite(v):
    return jnp.sum(jnp.where(v > -jnp.inf, 1.0, 0.0), axis=0, keepdims=True)


def _peer_thresholds(s0s, s1s, a_ref, b_ref, exact_ties):
    nh = len(s0s)
    rows, left, ranks = _top_rows(s0s + s1s, PEER_NTOP, exact_ties, ranked=range(nh, 2 * nh))
    excess = sum(jnp.abs(_n_finite(v) - (PEER_KEYS - PEER_NTOP)) for v in left)

    r24 = lax.broadcasted_iota(jnp.int32, a_ref.shape[1:], 0)
    r8 = lax.broadcasted_iota(jnp.int32, (8, a_ref.shape[2]), 0)
    ninf = -jnp.inf

    def grp(x, r, lo, hi):
        return jnp.where((r >= lo) & (r <= hi), x, ninf)

    cands = []
    for hd in range(nh):
        a_ref[hd] = jnp.full(a_ref.shape[1:], ninf, F32)
        b_ref[hd] = jnp.full(b_ref.shape[1:], ninf, F32)
        for r in range(PEER_NTOP):
            a_ref[hd, r:r + 1, :] = rows[hd][r]
            b_ref[hd, r:r + 1, :] = rows[nh + hd][r]
        a24, b24 = a_ref[hd], b_ref[hd]
        a8, b8 = a24[:8], b24[:8]
        cands.append(jnp.concatenate([
            grp(a24[0:1] + b24, r24, 0, 16),
            grp(a24 + b24[0:1], r24, 1, 16),
            grp(a8[1:2] + b8, r8, 1, 7),
            grp(a8 + b8[1:2], r8, 2, 7),
            grp(a8[2:3] + b8, r8, 2, 4),
            grp(a8 + b8[2:3], r8, 3, 4),
            grp(a8[3:4] + b8, r8, 3, 3),
        ], axis=0))
    tops, left, _ = _top_rows(cands, PEER_NTOP, exact_ties)
    excess = excess + sum(jnp.abs(_n_finite(v) - (PEER_NCAND - PEER_NTOP)) for v in left)

    outs = []
    for hd in range(nh):
        top = tops[hd]
        tau = 0.5 * (top[PEER_TOPK - 1] + top[PEER_TOPK])
        z = sum(jnp.exp(t - top[0]) for t in top[:PEER_TOPK])
        thr = tau - s0s[hd]
        cnt = sum(jnp.where(b_r >= thr, 1.0, 0.0) for b_r in rows[nh + hd])
        outs.append((cnt,
                     jnp.exp(s0s[hd] - rows[hd][0]) * (0.5 / z),
                     ranks[nh + hd],
                     jnp.exp(s1s[hd] - rows[nh + hd][0])))
    return outs, excess


def _peer_route_body(h_ref, wq_ref, sk_ref, cnt_ref, e0_ref, rank_ref, e1_ref, a_ref, b_ref):
    nh = sk_ref.shape[0]
    q = jnp.dot(h_ref[...], wq_ref[...].astype(BF16), preferred_element_type=F32).astype(BF16)
    s0s, s1s = [], []
    for hd in range(nh):
        c0 = 2 * hd * PEER_KEYS
        s0s.append(lax.dot_general(sk_ref[hd, 0], q[:, c0:c0 + PEER_KEYS], _NT,
                                   preferred_element_type=F32))
        s1s.append(lax.dot_general(sk_ref[hd, 1], q[:, c0 + PEER_KEYS:c0 + 2 * PEER_KEYS], _NT,
                                   preferred_element_type=F32))

    def emit(outs):
        for hd, (cnt, e0, rank1, e1) in enumerate(outs):
            cnt_ref[hd] = cnt
            e0_ref[hd] = e0
            rank_ref[hd] = rank1.astype(rank_ref.dtype)
            e1_ref[hd] = e1.astype(e1_ref.dtype)

    outs, excess = _peer_thresholds(s0s, s1s, a_ref, b_ref, exact_ties=False)
    emit(outs)

    @pl.when(jnp.max(excess) > 0.0)
    def _():
        emit(_peer_thresholds(s0s, s1s, a_ref, b_ref, exact_ties=True)[0])


def _peer_route(h, wq, layer, sub_keys):
    s, d = h.shape
    ph = sub_keys.shape[0]
    tt = _tile(s, 256)
    nh = 2 if ph % 2 == 0 else 1
    out_spec = pl.BlockSpec((nh, PEER_KEYS, tt), lambda p, t: (p, 0, t))
    out_sds = [jax.ShapeDtypeStruct((ph, PEER_KEYS, s), dt) for dt in (F32, F32, BF16, BF16)]
    return pl.pallas_call(
        _peer_route_body,
        grid=(ph // nh, s // tt),
        in_specs=[pl.BlockSpec((tt, d), lambda p, t: (t, 0)),
                  pl.BlockSpec((None, d, nh * 2 * PEER_KEYS), lambda p, t: (layer, 0, p)),
                  pl.BlockSpec((nh, 2, PEER_KEYS, PEER_KEYS), lambda p, t: (p, 0, 0, 0))],
        out_specs=[out_spec] * 4,
        out_shape=out_sds,
        scratch_shapes=[pltpu.VMEM((nh, PEER_TOP_ROWS, tt), F32), pltpu.VMEM((nh, PEER_TOP_ROWS, tt), F32)],
        compiler_params=_params(("parallel", "arbitrary")),
        name="peer_route",
    )(h, wq, sub_keys)


def _bf16_rows(row):
    tile = jnp.broadcast_to(row, (16, row.shape[1])).astype(BF16)
    return jnp.concatenate([tile] * (PEER_KEYS // 16), axis=0)


def _peer_act_body(u_ref, ht_ref, cnt_ref, e0_ref, rank_ref, e1_ref, w_ref, *, rows):
    n_heads = rank_ref.shape[0]
    te = u_ref.shape[0]
    for ch in range(te // rows):
        z = jnp.dot(u_ref[ch * rows:(ch + 1) * rows, :], ht_ref[...], preferred_element_type=F32)
        for ci in range(rows // PEER_KEYS):
            ii = ch * (rows // PEER_KEYS) + ci
            zi = z[ci * PEER_KEYS:(ci + 1) * PEER_KEYS, :]
            act = zi * (1.0 + lax.erf(zi * SQRT_HALF))
            gate = jnp.zeros(zi.shape, BF16)
            for hd in range(n_heads):
                sel = rank_ref[hd] < _bf16_rows(cnt_ref[hd, ii:ii + 1, :])
                gate = gate + (jnp.where(sel, e1_ref[hd], jnp.zeros((), BF16))
                               * _bf16_rows(e0_ref[hd, ii:ii + 1, :]))
            w_ref[:, ii * PEER_KEYS:(ii + 1) * PEER_KEYS] = (act * gate.astype(F32)).T.astype(w_ref.dtype)


def _peer_act(u, h_t, cnt, e0, rank, e1):
    n_exp, d = u.shape
    s = h_t.shape[1]
    ph = cnt.shape[0]
    te = 8 * PEER_KEYS
    tt = _tile(s, 512)
    assert n_exp % te == 0
    per_i = pl.BlockSpec((ph, 8, tt), lambda t, e: (0, e, t))
    per_j = pl.BlockSpec((ph, PEER_KEYS, tt), lambda t, e: (0, 0, t))
    return pl.pallas_call(
        functools.partial(_peer_act_body, rows=te // PEER_ACT_CHUNKS),
        grid=(s // tt, n_exp // te),
        in_specs=[pl.BlockSpec((te, d), lambda t, e: (e, 0)),
                  pl.BlockSpec((d, tt), lambda t, e: (0, t)),
                  per_i, per_i, per_j, per_j],
        out_specs=pl.BlockSpec((tt, te), lambda t, e: (t, e)),
        out_shape=jax.ShapeDtypeStruct((s, n_exp), BF16),
        compiler_params=_params(("parallel", "arbitrary")),
        name="peer_act",
    )(u, h_t, cnt, e0, rank, e1)


def _peer_down_body(w_ref, v_ref, x_ref, g_ref, o_ref, acc_ref):
    k = pl.program_id(2)

    @pl.when(k == 0)
    def _():
        acc_ref[...] = jnp.zeros_like(acc_ref)

    acc_ref[...] += jnp.dot(w_ref[...], v_ref[...], preferred_element_type=F32)

    @pl.when(k == pl.num_programs(2) - 1)
    def _():
        o_ref[...] = x_ref[...] + g_ref[...] * acc_ref[...]


def _peer_down(w, v, x, gate):
    s, n_exp = w.shape
    d = v.shape[1]
    tm, tn, tk = _tile(s, 1024), _tile(d, 1024), _tile(n_exp, 2048)
    return pl.pallas_call(
        _peer_down_body,
        grid=(s // tm, d // tn, n_exp // tk),
        in_specs=[pl.BlockSpec((tm, tk), lambda i, j, k: (i, k)),
                  pl.BlockSpec((tk, tn), lambda i, j, k: (k, j)),
                  pl.BlockSpec((tm, tn), lambda i, j, k: (i, j)),
                  pl.BlockSpec((1, tn), lambda i, j, k: (0, j))],
        out_specs=pl.BlockSpec((tm, tn), lambda i, j, k: (i, j)),
        out_shape=jax.ShapeDtypeStruct((s, d), F32),
        scratch_shapes=[pltpu.VMEM((tm, tn), F32)],
        compiler_params=_params(("parallel", "parallel", "arbitrary")),
        name="peer_down",
    )(w, v, x, gate)


def kernel(x, c, ada_w, ada_b, norm1_gain, norm2_gain, w_in, fox_f_bias, fox_q_gain, fox_k_gain,
           hgrn_lower_bounds, hgrn_out_gain, w_out, peer_w_query, peer_sub_keys, peer_u, peer_v):
    batch, seq, d = x.shape
    assert batch == 1, "one sequence per call"
    depth = ada_w.shape[0]
    fox_heads = fox_f_bias.shape[-1]
    hgrn_heads = hgrn_out_gain.shape[1]
    fox_width, hgrn_width = fox_heads * HEAD_DIM, hgrn_heads * HEAD_DIM
    assert fox_q_gain.shape[-1] == HEAD_DIM and hgrn_out_gain.shape[-1] == HEAD_DIM
    assert w_in.shape[-1] == 3 * fox_width + fox_heads + 4 * hgrn_width
    assert peer_sub_keys.shape[2:] == (2, PEER_KEYS, PEER_KEYS) and peer_u.shape[1] == PEER_KEYS ** 2
    o3 = 3 * fox_width
    o4 = o3 + fox_heads

    xs = x.reshape(seq, d)
    c_col = c.reshape(d, 1)
    w_t = jnp.swapaxes(w_in, 1, 2)
    for layer in range(depth):
        mod = _ada_mod(c_col, ada_w[layer], ada_b[layer].reshape(1, -1))
        shift1, scale1, gate1, shift2, scale2, gate2 = (mod[:, n * d:(n + 1) * d] for n in range(6))

        h = _norm_mod(xs, norm1_gain[layer].reshape(1, d), scale1, shift1)
        qk_gains = jnp.concatenate([fox_q_gain[layer].reshape(1, fox_width) * (HEAD_DIM ** -0.5 * LOG2_E),
                                    fox_k_gain[layer].reshape(1, fox_width)], axis=1)
        qk = _fox_proj(h, w_t, layer, qk_gains)
        v_t = _fox_v_proj(h, w_t, layer, 2 * fox_width, fox_width)
        pad = HEAD_DIM - fox_heads
        cum_f = _fox_gate(h, jnp.pad(w_t[layer, o3:o4].T, ((0, 0), (0, pad))).astype(BF16),
                          jnp.pad(fox_f_bias[layer].reshape(1, fox_heads), ((0, 0), (0, pad))), fox_heads)
        fox_out, u_bf16, v_bf16 = _fox_attention(qk, v_t, cum_f, fox_heads, (peer_u, peer_v), layer)

        hproj = _hgrn_proj(h, w_t, layer, o4, 4 * hgrn_width, F32)
        hgrn_out = _hgrn(hproj, hgrn_lower_bounds, hgrn_out_gain[layer].reshape(1, hgrn_width),
                         hgrn_heads, layer)

        xs = _out_proj(fox_out, hgrn_out, w_out, layer, xs, gate1)

        h, h_t = _norm_mod(xs, norm2_gain[layer].reshape(1, d), scale2, shift2, with_transposed=True)
        cnt, e0, rank1, e1 = _peer_route(h, peer_w_query, layer, peer_sub_keys[layer].astype(BF16))
        w_act = _peer_act(u_bf16, h_t, cnt, e0, rank1, e1)
        xs = _peer_down(w_act, v_bf16, xs, gate2)
    return xs.reshape(batch, seq, d)
```

```python
import functools

import jax
import jax.numpy as jnp
from jax import lax
from jax.experimental import pallas as pl
from jax.experimental.pallas import tpu as pltpu

F32 = jnp.float32
BF16 = jnp.bfloat16

HEAD_DIM = 128
PEER_KEYS = 128
PEER_TOPK = 16
HGRN_CHUNK = 128
HGRN_HEAD_GROUP = 8
EPS = 1e-6
MASKED = -1e30
SQRT_HALF = 0.7071067811865476
LOG2_E = 1.4426950408889634
ATTN_LANE_GROUP = 256
ATTN_ONES_ROWS = 16
ATTN_UNROLL = 4
PEER_ACT_CHUNKS = 2
V7X_VMEM_LIMIT_BYTES = 52 * 1024 * 1024

_NT = (((1,), (1,)), ((), ()))


def _params(semantics):
    return pltpu.CompilerParams(dimension_semantics=semantics, vmem_limit_bytes=V7X_VMEM_LIMIT_BYTES)


def _tile(n, t):
    if n <= t:
        return n
    t -= t % HEAD_DIM
    while n % t:
        t -= HEAD_DIM
    assert t > 0, n
    return t


def _split3(x):
    hi = x.astype(BF16)
    r1 = x - hi.astype(F32)
    mid = r1.astype(BF16)
    lo = (r1 - mid.astype(F32)).astype(BF16)
    return hi, mid, lo


def _ada_body(c_ref, w_ref, b_ref, o_ref, acc_ref):
    k = pl.program_id(1)

    @pl.when(k == 0)
    def _():
        acc_ref[...] = jnp.zeros_like(acc_ref)

    c = c_ref[...]
    c_act = c * jax.nn.sigmoid(c)
    acc_ref[...] += jnp.sum(w_ref[...] * c_act, axis=0, keepdims=True)

    @pl.when(k == pl.num_programs(1) - 1)
    def _():
        o_ref[...] = acc_ref[...] + b_ref[...]


def _ada_mod(c_col, w, b_row):
    d, n = w.shape
    tk, tn = _tile(d, 512), _tile(n, 2048)
    return pl.pallas_call(
        _ada_body,
        grid=(n // tn, d // tk),
        in_specs=[pl.BlockSpec((tk, 1), lambda j, k: (k, 0)),
                  pl.BlockSpec((tk, tn), lambda j, k: (k, j)),
                  pl.BlockSpec((1, tn), lambda j, k: (0, j))],
        out_specs=pl.BlockSpec((1, tn), lambda j, k: (0, j)),
        out_shape=jax.ShapeDtypeStruct((1, n), F32),
        scratch_shapes=[pltpu.VMEM((1, tn), F32)],
        compiler_params=_params(("parallel", "arbitrary")),
        name="ada_mod",
    )(c_col, w, b_row)


def _norm_mod_body(x_ref, g_ref, sc_ref, sh_ref, o_ref, *maybe_ot_ref):
    x = x_ref[...]
    y = x * lax.rsqrt(jnp.mean(x * x, axis=-1, keepdims=True) + EPS)
    h = y * g_ref[...] * (1.0 + sc_ref[...]) + sh_ref[...]
    o_ref[...] = h.astype(o_ref.dtype)
    for ot_ref in maybe_ot_ref:
        ot_ref[...] = h.T.astype(ot_ref.dtype)


def _norm_mod(x, gain, scale, shift, with_transposed=False):
    s, d = x.shape
    tm = _tile(s, 256)
    row = pl.BlockSpec((1, d), lambda i: (0, 0))
    out_specs = [pl.BlockSpec((tm, d), lambda i: (i, 0))]
    out_shape = [jax.ShapeDtypeStruct((s, d), BF16)]
    if with_transposed:
        out_specs.append(pl.BlockSpec((d, tm), lambda i: (0, i)))
        out_shape.append(jax.ShapeDtypeStruct((d, s), BF16))
    outs = pl.pallas_call(
        _norm_mod_body,
        grid=(s // tm,),
        in_specs=[pl.BlockSpec((tm, d), lambda i: (i, 0)), row, row, row],
        out_specs=out_specs,
        out_shape=out_shape,
        compiler_params=_params(("parallel",)),
        name="norm_mod",
    )(x, gain, scale, shift)
    return outs if with_transposed else outs[0]


def _fox_proj_body(h_ref, w_ref, g_ref, o_ref):
    y = lax.dot_general(h_ref[...], w_ref[0].astype(BF16), _NT, preferred_element_type=F32)
    g = g_ref[...]
    for hh in range(y.shape[1] // HEAD_DIM):
        sl = slice(hh * HEAD_DIM, (hh + 1) * HEAD_DIM)
        yh = y[:, sl]
        r = lax.rsqrt(jnp.mean(yh * yh, axis=-1, keepdims=True) + EPS)
        o_ref[:, sl] = (yh * r * g[:, sl]).astype(o_ref.dtype)


def _w_rows_spec(layer, first_row, tn, d):
    assert first_row % 8 == 0 and tn % 8 == 0
    return pl.BlockSpec((pl.Element(1), pl.Element(tn), pl.Element(d)),
                        lambda i, j: (layer, pl.multiple_of(first_row + j * tn, 8), 0))


def _fox_proj(h, w_t, layer, gains):
    s, d = h.shape
    n = gains.shape[1]
    tm, tn = _tile(s, 1024), _tile(n, 512)
    return pl.pallas_call(
        _fox_proj_body,
        grid=(s // tm, n // tn),
        in_specs=[pl.BlockSpec((tm, d), lambda i, j: (i, 0)),
                  _w_rows_spec(layer, 0, tn, d),
                  pl.BlockSpec((1, tn), lambda i, j: (0, j))],
        out_specs=pl.BlockSpec((tm, tn), lambda i, j: (i, j)),
        out_shape=jax.ShapeDtypeStruct((s, n), BF16),
        compiler_params=_params(("parallel", "arbitrary")),
        name="fox_proj",
    )(h, w_t, gains)


def _fox_v_proj_body(h_ref, w_ref, o_ref):
    o_ref[...] = lax.dot_general(w_ref[0].astype(BF16), h_ref[...], _NT,
                                 preferred_element_type=F32).astype(o_ref.dtype)


def _fox_v_proj(h, w_t, layer, first_row, n):
    s, d = h.shape
    tm, tn = _tile(s, 1024), _tile(n, 512)
    return pl.pallas_call(
        _fox_v_proj_body,
        grid=(s // tm, n // tn),
        in_specs=[pl.BlockSpec((tm, d), lambda i, j: (i, 0)),
                  _w_rows_spec(layer, first_row, tn, d)],
        out_specs=pl.BlockSpec((tn, tm), lambda i, j: (j, i)),
        out_shape=jax.ShapeDtypeStruct((n, s), BF16),
        compiler_params=_params(("parallel", "arbitrary")),
        name="fox_v_proj",
    )(h, w_t)


def _hgrn_proj_body(h_ref, w_ref, o_ref):
    o_ref[...] = lax.dot_general(h_ref[...], w_ref[0].astype(BF16), _NT,
                                 preferred_element_type=F32).astype(o_ref.dtype)


def _hgrn_proj(h, w_t, layer, first_row, n, out_dtype):
    s, d = h.shape
    tm, tn = _tile(s, 1024), _tile(n, 512)
    return pl.pallas_call(
        _hgrn_proj_body,
        grid=(s // tm, n // tn),
        in_specs=[pl.BlockSpec((tm, d), lambda i, j: (i, 0)),
                  _w_rows_spec(layer, first_row, tn, d)],
        out_specs=pl.BlockSpec((tm, tn), lambda i, j: (i, j)),
        out_shape=jax.ShapeDtypeStruct((s, n), out_dtype),
        compiler_params=_params(("parallel", "arbitrary")),
        name="hgrn_proj",
    )(h, w_t)


def _fox_gate_body(h_ref, w_ref, b_ref, o_ref, carry_ref):
    i = pl.program_id(0)

    @pl.when(i == 0)
    def _():
        carry_ref[...] = jnp.zeros_like(carry_ref)

    logit = jnp.dot(h_ref[...], w_ref[...], preferred_element_type=F32) + b_ref[...]
    log_f = jnp.minimum(logit, 0.0) - jnp.log1p(jnp.exp(-jnp.abs(logit)))
    ts = log_f.shape[0]
    lower = (lax.broadcasted_iota(jnp.int32, (ts, ts), 1)
             <= lax.broadcasted_iota(jnp.int32, (ts, ts), 0)).astype(BF16)
    csum = sum(jnp.dot(lower, part, preferred_element_type=F32) for part in _split3(log_f))
    csum = csum + carry_ref[...]
    carry_ref[...] = csum[ts - 1:ts, :]
    scaled = csum * LOG2_E
    for hd in range(o_ref.shape[0]):
        o_ref[hd] = jnp.broadcast_to(scaled[:, hd:hd + 1], (ts, HEAD_DIM))


def _fox_gate(h, w_pad, bias_pad, n_heads):
    s, d = h.shape
    ts = _tile(s, 512)
    return pl.pallas_call(
        _fox_gate_body,
        grid=(s // ts,),
        in_specs=[pl.BlockSpec((ts, d), lambda i: (i, 0)),
                  pl.BlockSpec((d, HEAD_DIM), lambda i: (0, 0)),
                  pl.BlockSpec((1, HEAD_DIM), lambda i: (0, 0))],
        out_specs=pl.BlockSpec((n_heads, ts, HEAD_DIM), lambda i: (0, i, 0)),
        out_shape=jax.ShapeDtypeStruct((n_heads, s, HEAD_DIM), F32),
        scratch_shapes=[pltpu.VMEM((1, HEAD_DIM), F32)],
        compiler_params=_params(("arbitrary",)),
        name="fox_gate",
    )(h, w_pad, bias_pad)


def _fox_attn_body(q_ref, k_ref, vt_ref, cb_ref, *refs, t, n_tables):
    table_refs, o_ref = refs[:n_tables], refs[n_tables]
    table_out_refs = refs[n_tables + 1:2 * n_tables + 1]
    sa_ref, sb_ref, m_ref, acc_ref = refs[2 * n_tables + 1:]
    for src_ref, dst_ref in zip(table_refs, table_out_refs):
        dst_ref[...] = src_ref[...].astype(dst_ref.dtype)

    i = pl.program_id(1)
    q = q_ref[...]
    q_start = pl.multiple_of(i * t, t)
    c_first = cb_ref[pl.ds(q_start, 8), :][0:1]
    m_ref[...] = jnp.full_like(m_ref, MASKED)
    acc_ref[...] = jnp.zeros_like(acc_ref)
    group = min(t, ATTN_LANE_GROUP)
    ones_rows = jnp.ones((ATTN_ONES_ROWS, t), BF16)

    def scores(j):
        k_start = pl.multiple_of(j * t, t)
        return lax.dot_general(k_ref[pl.ds(k_start, t), :], q, _NT, preferred_element_type=F32)

    def consume(j, sc_ref, masked):
        k_start = pl.multiple_of(j * t, t)
        bias = c_first - cb_ref[pl.ds(k_start, t), :]
        bias = jnp.concatenate([bias] * (group // HEAD_DIM), axis=1)
        v_t = jnp.concatenate([vt_ref[:, pl.ds(k_start, t)], ones_rows], axis=0)
        for g in range(t // group):
            lanes = slice(g * group, (g + 1) * group)
            s = sc_ref[:, lanes] + bias
            if masked:
                k_pos = lax.broadcasted_iota(jnp.int32, (t, group), 0)
                q_pos = lax.broadcasted_iota(jnp.int32, (t, group), 1) + g * group
                s = jnp.where(k_pos <= q_pos, s, MASKED)
            m_old = m_ref[:, lanes]
            m_new = jnp.maximum(m_old, jnp.max(s, axis=0, keepdims=True))
            alpha = jnp.exp2(m_old - m_new)
            p = jnp.exp2(s - m_new)
            acc_ref[:, lanes] = alpha * acc_ref[:, lanes] + jnp.dot(
                v_t, p.astype(BF16), preferred_element_type=F32)
            m_ref[:, lanes] = m_new

    bufs = (sa_ref, sb_ref)
    sa_ref[...] = scores(0)

    def run(j0, n_blocks, last_is_diagonal):
        for b in range(n_blocks):
            is_last = b == n_blocks - 1
            if not (is_last and last_is_diagonal):
                bufs[(b + 1) % 2][...] = scores(j0 + b + 1)
            consume(j0 + b, bufs[b % 2], masked=is_last and last_is_diagonal)

    def unrolled(jj, carry):
        run(jj * ATTN_UNROLL, ATTN_UNROLL, last_is_diagonal=False)
        return carry

    n_unrolled = i // ATTN_UNROLL
    lax.fori_loop(0, n_unrolled, unrolled, 0)
    for rem in range(ATTN_UNROLL):
        @pl.when(i - n_unrolled * ATTN_UNROLL == rem)
        def _(rem=rem):
            run(n_unrolled * ATTN_UNROLL, rem + 1, last_is_diagonal=True)

    o_ref[...] = (acc_ref[:HEAD_DIM, :] / acc_ref[HEAD_DIM:HEAD_DIM + 1, :]).T.astype(o_ref.dtype)


def _fox_attention(qk, v_t, cum_f, n_heads, tables, layer):
    s = qk.shape[0]
    t = _tile(s, 512)
    n_q = s // t
    steps = n_heads * n_q
    table_in, table_out, table_shape = [], [], []
    for tab in tables:
        rows, d = tab.shape[1:]
        assert rows % steps == 0 and (rows // steps) % 16 == 0, (rows, steps)
        table_in.append(pl.BlockSpec((None, rows // steps, d), lambda h, i: (layer, h * n_q + i, 0)))
        table_out.append(pl.BlockSpec((rows // steps, d), lambda h, i: (h * n_q + i, 0)))
        table_shape.append(jax.ShapeDtypeStruct((rows, d), BF16))
    return pl.pallas_call(
        functools.partial(_fox_attn_body, t=t, n_tables=len(tables)),
        grid=(n_heads, n_q),
        in_specs=[pl.BlockSpec((t, HEAD_DIM), lambda h, i: (i, h)),
                  pl.BlockSpec((s, HEAD_DIM), lambda h, i: (0, n_heads + h)),
                  pl.BlockSpec((HEAD_DIM, s), lambda h, i: (h, 0)),
                  pl.BlockSpec((None, s, HEAD_DIM), lambda h, i: (h, 0, 0))] + table_in,
        out_specs=[pl.BlockSpec((t, HEAD_DIM), lambda h, i: (i, h))] + table_out,
        out_shape=[jax.ShapeDtypeStruct((s, n_heads * HEAD_DIM), BF16)] + table_shape,
        scratch_shapes=[pltpu.VMEM((t, t), F32), pltpu.VMEM((t, t), F32),
                        pltpu.VMEM((1, t), F32),
                        pltpu.VMEM((HEAD_DIM + ATTN_ONES_ROWS, t), F32)],
        compiler_params=_params(("parallel", "arbitrary")),
        name="fox_attention",
    )(qk, qk, v_t, cum_f, *tables)


def _hgrn_head(hq, hf, val, hg, lb, gain, state_t):
    forget = lb + (1.0 - lb) * jax.nn.sigmoid(hf)
    log_f = jnp.log(forget)
    key = 1.0 - forget
    query = hq * jax.nn.sigmoid(hq) * (HEAD_DIM ** -0.5)
    val_b = val.astype(BF16)
    c = log_f.shape[0]

    row = lax.broadcasted_iota(jnp.int32, (c, c), 0)
    col = lax.broadcasted_iota(jnp.int32, (c, c), 1)
    lower = (col <= row).astype(BF16)
    b = sum(jnp.dot(lower, part, preferred_element_type=F32) for part in _split3(log_f))
    b_last = b[c - 1:c, :]

    out = lax.dot_general((query * jnp.exp(b)).astype(BF16), state_t.astype(BF16), _NT,
                          preferred_element_type=F32)

    t_idx = lax.broadcasted_iota(jnp.int32, (c, HEAD_DIM), 0)
    sep = row ^ col
    scores = jnp.where(sep == 0, lax.dot_general(query.astype(BF16), key.astype(BF16), _NT,
                                                 preferred_element_type=F32), 0.0)
    half = c // 2
    while half >= 1:
        blk = 2 * half
        if half >= 4:
            ref_rows = jnp.broadcast_to(b.reshape(c // blk, blk, HEAD_DIM)[:, half - 1:half, :],
                                        (c // blk, blk, HEAD_DIM)).reshape(c, HEAD_DIM)
        elif half == 2:
            pos = t_idx & 3
            ref_rows = jnp.where(pos == 0, pltpu.roll(b, c - 1, 0),
                                 jnp.where(pos == 2, pltpu.roll(b, 1, 0),
                                           jnp.where(pos == 3, pltpu.roll(b, 2, 0), b)))
        else:
            ref_rows = jnp.where((t_idx & 1) == 1, pltpu.roll(b, 1, 0), b)
        in_upper = (t_idx & half) != 0
        x = jnp.where(in_upper, query, key) * jnp.exp(-jnp.abs(b - ref_rows))
        a_side = jnp.where(in_upper, x, 0.0).astype(BF16)
        b_side = jnp.where(in_upper, 0.0, x).astype(BF16)
        level = lax.dot_general(a_side, b_side, _NT, preferred_element_type=F32)
        scores = scores + (level if blk == c else jnp.where(sep < blk, level, 0.0))
        half //= 2
    out = out + jnp.dot(scores.astype(BF16), val_b, preferred_element_type=F32)

    key_dec = (key * jnp.exp(b_last - b)).astype(BF16)
    new_state_t = state_t * jnp.exp(b_last) + jnp.dot(val.T.astype(BF16), key_dec,
                                                      preferred_element_type=F32)

    normed = out * lax.rsqrt(jnp.mean(out * out, axis=-1, keepdims=True) + EPS) * gain
    return normed * (hg * jax.nn.sigmoid(hg)), new_state_t


def _hgrn_body(q_ref, f_ref, i_ref, g_ref, lb_ref, gain_ref, o_ref, state_ref, *, layer):
    @pl.when(pl.program_id(1) == 0)
    def _():
        state_ref[...] = jnp.zeros_like(state_ref)

    for hh in range(state_ref.shape[0]):
        sl = slice(hh * HEAD_DIM, (hh + 1) * HEAD_DIM)
        table = lb_ref[:, sl]
        e = jnp.exp(table - jnp.max(table, axis=0, keepdims=True))
        lb = jnp.sum(e[:layer + 1], axis=0, keepdims=True) / jnp.sum(e, axis=0, keepdims=True)
        out, state_ref[hh] = _hgrn_head(q_ref[:, sl], f_ref[:, sl], i_ref[:, sl], g_ref[:, sl],
                                        lb, gain_ref[:, sl], state_ref[hh])
        o_ref[:, sl] = out.astype(o_ref.dtype)


def _hgrn(proj, lb_table, out_gain, n_heads, layer):
    s = proj.shape[0]
    c = _tile(s, HGRN_CHUNK)
    n_slots = lb_table.shape[0]
    group = HGRN_HEAD_GROUP if n_heads % HGRN_HEAD_GROUP == 0 else 1
    n_groups = n_heads // group
    width = group * HEAD_DIM

    def col(block):
        return pl.BlockSpec((c, width), lambda g, t: (t, block * n_groups + g))

    return pl.pallas_call(
        functools.partial(_hgrn_body, layer=layer),
        grid=(n_groups, s // c),
        in_specs=[col(0), col(1), col(2), col(3),
                  pl.BlockSpec((n_slots, width), lambda g, t: (0, g)),
                  pl.BlockSpec((1, width), lambda g, t: (0, g))],
        out_specs=pl.BlockSpec((c, width), lambda g, t: (t, g)),
        out_shape=jax.ShapeDtypeStruct((s, n_heads * HEAD_DIM), BF16),
        scratch_shapes=[pltpu.VMEM((group, HEAD_DIM, HEAD_DIM), F32)],
        compiler_params=_params(("parallel", "arbitrary")),
        name="hgrn2",
    )(proj, proj, proj, proj, lb_table, out_gain)


def _out_proj_body(a_ref, b_ref, wa_ref, wb_ref, x_ref, g_ref, o_ref):
    mix = (jnp.dot(a_ref[...], wa_ref[0].astype(BF16), preferred_element_type=F32)
           + jnp.dot(b_ref[...], wb_ref[0].astype(BF16), preferred_element_type=F32))
    o_ref[...] = x_ref[...] + g_ref[...] * mix


def _out_proj(fox, hgrn, w_out, layer, x, gate):
    s, d = x.shape
    tm, tn = _tile(s, 1024), _tile(d, 512)
    ka, kb = fox.shape[1], hgrn.shape[1]

    def w_rows(first_row, k):
        return pl.BlockSpec((pl.Element(1), pl.Element(k), pl.Element(tn)),
                            lambda i, j: (layer, first_row, pl.multiple_of(j * tn, HEAD_DIM)))

    return pl.pallas_call(
        _out_proj_body,
        grid=(s // tm, d // tn),
        in_specs=[pl.BlockSpec((tm, ka), lambda i, j: (i, 0)),
                  pl.BlockSpec((tm, kb), lambda i, j: (i, 0)),
                  w_rows(0, ka),
                  w_rows(ka, kb),
                  pl.BlockSpec((tm, tn), lambda i, j: (i, j)),
                  pl.BlockSpec((1, tn), lambda i, j: (0, j))],
        out_specs=pl.BlockSpec((tm, tn), lambda i, j: (i, j)),
        out_shape=jax.ShapeDtypeStruct((s, d), F32),
        compiler_params=_params(("parallel", "arbitrary")),
        name="out_proj",
    )(fox, hgrn, w_out, w_out, x, gate)


PEER_NTOP = PEER_TOPK + 1
PEER_TOP_ROWS = 24
PEER_NCAND = 52


def _top_rows(arrays, k, exact_ties, ranked=()):
    vals = list(arrays)
    rows = [[] for _ in vals]
    ranks = {n: jnp.full(vals[n].shape, float(vals[n].shape[0]), F32) for n in ranked}
    for r in range(k):
        for n, v in enumerate(vals):
            m = jnp.max(v, axis=0, keepdims=True)
            rows[n].append(m)
            hit = v == m
            if exact_ties:
                idx = lax.broadcasted_iota(jnp.int32, v.shape, 0)
                first = jnp.min(jnp.where(hit, idx, v.shape[0]), axis=0, keepdims=True)
                hit = idx == first
            vals[n] = jnp.where(hit, -jnp.inf, v)
            if n in ranks:
                ranks[n] = jnp.where(hit, float(r), ranks[n])
    return rows, vals, ranks


def _n_finite(v):
    return jnp.sum(jnp.where(v > -jnp.inf, 1.0, 0.0), axis=0, keepdims=True)


def _peer_thresholds(s0s, s1s, a_ref, b_ref, exact_ties):
    nh = len(s0s)
    rows, left, ranks = _top_rows(s0s + s1s, PEER_NTOP, exact_ties, ranked=range(nh, 2 * nh))
    excess = sum(jnp.abs(_n_finite(v) - (PEER_KEYS - PEER_NTOP)) for v in left)

    r24 = lax.broadcasted_iota(jnp.int32, a_ref.shape[1:], 0)
    r8 = lax.broadcasted_iota(jnp.int32, (8, a_ref.shape[2]), 0)
    ninf = -jnp.inf

    def grp(x, r, lo, hi):
        return jnp.where((r >= lo) & (r <= hi), x, ninf)

    cands = []
    for hd in range(nh):
        a_ref[hd] = jnp.full(a_ref.shape[1:], ninf, F32)
        b_ref[hd] = jnp.full(b_ref.shape[1:], ninf, F32)
        for r in range(PEER_NTOP):
            a_ref[hd, r:r + 1, :] = rows[hd][r]
            b_ref[hd, r:r + 1, :] = rows[nh + hd][r]
        a24, b24 = a_ref[hd], b_ref[hd]
        a8, b8 = a24[:8], b24[:8]
        cands.append(jnp.concatenate([
            grp(a24[0:1] + b24, r24, 0, 16),
            grp(a24 + b24[0:1], r24, 1, 16),
            grp(a8[1:2] + b8, r8, 1, 7),
            grp(a8 + b8[1:2], r8, 2, 7),
            grp(a8[2:3] + b8, r8, 2, 4),
            grp(a8 + b8[2:3], r8, 3, 4),
            grp(a8[3:4] + b8, r8, 3, 3),
        ], axis=0))
    tops, left, _ = _top_rows(cands, PEER_NTOP, exact_ties)
    excess = excess + sum(jnp.abs(_n_finite(v) - (PEER_NCAND - PEER_NTOP)) for v in left)

    outs = []
    for hd in range(nh):
        top = tops[hd]
        tau = 0.5 * (top[PEER_TOPK - 1] + top[PEER_TOPK])
        z = sum(jnp.exp(t - top[0]) for t in top[:PEER_TOPK])
        thr = tau - s0s[hd]
        cnt = sum(jnp.where(b_r >= thr, 1.0, 0.0) for b_r in rows[nh + hd])
        outs.append((cnt,
                     jnp.exp(s0s[hd] - rows[hd][0]) * (0.5 / z),
                     ranks[nh + hd],
                     jnp.exp(s1s[hd] - rows[nh + hd][0])))
    return outs, excess


def _peer_route_body(h_ref, wq_ref, sk_ref, cnt_ref, e0_ref, rank_ref, e1_ref, a_ref, b_ref):
    nh = sk_ref.shape[0]
    q = jnp.dot(h_ref[...], wq_ref[...].astype(BF16), preferred_element_type=F32).astype(BF16)
    s0s, s1s = [], []
    for hd in range(nh):
        c0 = 2 * hd * PEER_KEYS
        s0s.append(lax.dot_general(sk_ref[hd, 0], q[:, c0:c0 + PEER_KEYS], _NT,
                                   preferred_element_type=F32))
        s1s.append(lax.dot_general(sk_ref[hd, 1], q[:, c0 + PEER_KEYS:c0 + 2 * PEER_KEYS], _NT,
                                   preferred_element_type=F32))

    def emit(outs):
        for hd, (cnt, e0, rank1, e1) in enumerate(outs):
            cnt_ref[hd] = cnt
            e0_ref[hd] = e0
            rank_ref[hd] = rank1.astype(rank_ref.dtype)
            e1_ref[hd] = e1.astype(e1_ref.dtype)

    outs, excess = _peer_thresholds(s0s, s1s, a_ref, b_ref, exact_ties=False)
    emit(outs)

    @pl.when(jnp.max(excess) > 0.0)
    def _():
        emit(_peer_thresholds(s0s, s1s, a_ref, b_ref, exact_ties=True)[0])


def _peer_route(h, wq, layer, sub_keys):
    s, d = h.shape
    ph = sub_keys.shape[0]
    tt = _tile(s, 128)
    nh = 4 if ph % 4 == 0 else 1
    out_spec = pl.BlockSpec((nh, PEER_KEYS, tt), lambda p, t: (p, 0, t))
    out_sds = [jax.ShapeDtypeStruct((ph, PEER_KEYS, s), dt) for dt in (F32, F32, BF16, BF16)]
    return pl.pallas_call(
        _peer_route_body,
        grid=(ph // nh, s // tt),
        in_specs=[pl.BlockSpec((tt, d), lambda p, t: (t, 0)),
                  pl.BlockSpec((None, d, nh * 2 * PEER_KEYS), lambda p, t: (layer, 0, p)),
                  pl.BlockSpec((nh, 2, PEER_KEYS, PEER_KEYS), lambda p, t: (p, 0, 0, 0))],
        out_specs=[out_spec] * 4,
        out_shape=out_sds,
        scratch_shapes=[pltpu.VMEM((nh, PEER_TOP_ROWS, tt), F32), pltpu.VMEM((nh, PEER_TOP_ROWS, tt), F32)],
        compiler_params=_params(("parallel", "arbitrary")),
        name="peer_route",
    )(h, wq, sub_keys)


def _bf16_rows(row):
    tile = jnp.broadcast_to(row, (16, row.shape[1])).astype(BF16)
    return jnp.concatenate([tile] * (PEER_KEYS // 16), axis=0)


def _peer_act_body(u_ref, ht_ref, cnt_ref, e0_ref, rank_ref, e1_ref, w_ref, *, rows):
    n_heads = rank_ref.shape[0]
    te = u_ref.shape[0]
    for ch in range(te // rows):
        z = jnp.dot(u_ref[ch * rows:(ch + 1) * rows, :], ht_ref[...], preferred_element_type=F32)
        for ci in range(rows // PEER_KEYS):
            ii = ch * (rows // PEER_KEYS) + ci
            zi = z[ci * PEER_KEYS:(ci + 1) * PEER_KEYS, :]
            act = zi * (1.0 + lax.erf(zi * SQRT_HALF))
            gate = jnp.zeros(zi.shape, BF16)
            for hd in range(n_heads):
                sel = rank_ref[hd] < _bf16_rows(cnt_ref[hd, ii:ii + 1, :])
                gate = gate + (jnp.where(sel, e1_ref[hd], jnp.zeros((), BF16))
                               * _bf16_rows(e0_ref[hd, ii:ii + 1, :]))
            w_ref[:, ii * PEER_KEYS:(ii + 1) * PEER_KEYS] = (act * gate.astype(F32)).T.astype(w_ref.dtype)


def _peer_act(u, h_t, cnt, e0, rank, e1):
    n_exp, d = u.shape
    s = h_t.shape[1]
    ph = cnt.shape[0]
    te = 8 * PEER_KEYS
    tt = _tile(s, 512)
    assert n_exp % te == 0
    per_i = pl.BlockSpec((ph, 8, tt), lambda t, e: (0, e, t))
    per_j = pl.BlockSpec((ph, PEER_KEYS, tt), lambda t, e: (0, 0, t))
    return pl.pallas_call(
        functools.partial(_peer_act_body, rows=te // PEER_ACT_CHUNKS),
        grid=(s // tt, n_exp // te),
        in_specs=[pl.BlockSpec((te, d), lambda t, e: (e, 0)),
                  pl.BlockSpec((d, tt), lambda t, e: (0, t)),
                  per_i, per_i, per_j, per_j],
        out_specs=pl.BlockSpec((tt, te), lambda t, e: (t, e)),
        out_shape=jax.ShapeDtypeStruct((s, n_exp), BF16),
        compiler_params=_params(("parallel", "arbitrary")),
        name="peer_act",
    )(u, h_t, cnt, e0, rank, e1)


def _peer_down_body(w_ref, v_ref, x_ref, g_ref, o_ref, acc_ref):
    k = pl.program_id(2)

    @pl.when(k == 0)
    def _():
        acc_ref[...] = jnp.zeros_like(acc_ref)

    acc_ref[...] += jnp.dot(w_ref[...], v_ref[...], preferred_element_type=F32)

    @pl.when(k == pl.num_programs(2) - 1)
    def _():
        o_ref[...] = x_ref[...] + g_ref[...] * acc_ref[...]


def _peer_down(w, v, x, gate):
    s, n_exp = w.shape
    d = v.shape[1]
    tm, tn, tk = _tile(s, 1024), _tile(d, 1024), _tile(n_exp, 2048)
    return pl.pallas_call(
        _peer_down_body,
        grid=(s // tm, d // tn, n_exp // tk),
        in_specs=[pl.BlockSpec((tm, tk), lambda i, j, k: (i, k)),
                  pl.BlockSpec((tk, tn), lambda i, j, k: (k, j)),
                  pl.BlockSpec((tm, tn), lambda i, j, k: (i, j)),
                  pl.BlockSpec((1, tn), lambda i, j, k: (0, j))],
        out_specs=pl.BlockSpec((tm, tn), lambda i, j, k: (i, j)),
        out_shape=jax.ShapeDtypeStruct((s, d), F32),
        scratch_shapes=[pltpu.VMEM((tm, tn), F32)],
        compiler_params=_params(("parallel", "parallel", "arbitrary")),
        name="peer_down",
    )(w, v, x, gate)


def kernel(x, c, ada_w, ada_b, norm1_gain, norm2_gain, w_in, fox_f_bias, fox_q_gain, fox_k_gain,
           hgrn_lower_bounds, hgrn_out_gain, w_out, peer_w_query, peer_sub_keys, peer_u, peer_v):
    batch, seq, d = x.shape
    assert batch == 1, "one sequence per call"
    depth = ada_w.shape[0]
    fox_heads = fox_f_bias.shape[-1]
    hgrn_heads = hgrn_out_gain.shape[1]
    fox_width, hgrn_width = fox_heads * HEAD_DIM, hgrn_heads * HEAD_DIM
    assert fox_q_gain.shape[-1] == HEAD_DIM and hgrn_out_gain.shape[-1] == HEAD_DIM
    assert w_in.shape[-1] == 3 * fox_width + fox_heads + 4 * hgrn_width
    assert peer_sub_keys.shape[2:] == (2, PEER_KEYS, PEER_KEYS) and peer_u.shape[1] == PEER_KEYS ** 2
    o3 = 3 * fox_width
    o4 = o3 + fox_heads

    xs = x.reshape(seq, d)
    c_col = c.reshape(d, 1)
    w_t = jnp.swapaxes(w_in, 1, 2)
    for layer in range(depth):
        mod = _ada_mod(c_col, ada_w[layer], ada_b[layer].reshape(1, -1))
        shift1, scale1, gate1, shift2, scale2, gate2 = (mod[:, n * d:(n + 1) * d] for n in range(6))

        h = _norm_mod(xs, norm1_gain[layer].reshape(1, d), scale1, shift1)
        qk_gains = jnp.concatenate([fox_q_gain[layer].reshape(1, fox_width) * (HEAD_DIM ** -0.5 * LOG2_E),
                                    fox_k_gain[layer].reshape(1, fox_width)], axis=1)
        qk = _fox_proj(h, w_t, layer, qk_gains)
        v_t = _fox_v_proj(h, w_t, layer, 2 * fox_width, fox_width)
        pad = HEAD_DIM - fox_heads
        cum_f = _fox_gate(h, jnp.pad(w_t[layer, o3:o4].T, ((0, 0), (0, pad))).astype(BF16),
                          jnp.pad(fox_f_bias[layer].reshape(1, fox_heads), ((0, 0), (0, pad))), fox_heads)
        fox_out, u_bf16, v_bf16 = _fox_attention(qk, v_t, cum_f, fox_heads, (peer_u, peer_v), layer)

        hproj = _hgrn_proj(h, w_t, layer, o4, 4 * hgrn_width, F32)
        hgrn_out = _hgrn(hproj, hgrn_lower_bounds, hgrn_out_gain[layer].reshape(1, hgrn_width),
                         hgrn_heads, layer)

        xs = _out_proj(fox_out, hgrn_out, w_out, layer, xs, gate1)

        h, h_t = _norm_mod(xs, norm2_gain[layer].reshape(1, d), scale2, shift2, with_transposed=True)
        cnt, e0, rank1, e1 = _peer_route(h, peer_w_query, layer, peer_sub_keys[layer].astype(BF16))
        w_act = _peer_act(u_bf16, h_t, cnt, e0, rank1, e1)
        xs = _peer_down(w_act, v_bf16, xs, gate2)
    return xs.reshape(batch, seq, d)
```

```python
import functools

import jax
import jax.numpy as jnp
from jax import lax
from jax.experimental import pallas as pl
from jax.experimental.pallas import tpu as pltpu

F32 = jnp.float32
BF16 = jnp.bfloat16

HEAD_DIM = 128
PEER_KEYS = 128
PEER_TOPK = 16
HGRN_CHUNK = 128
HGRN_HEAD_GROUP = 8
EPS = 1e-6
MASKED = -1e30
SQRT_HALF = 0.7071067811865476
LOG2_E = 1.4426950408889634
ATTN_LANE_GROUP = 256
ATTN_ONES_ROWS = 16
ATTN_UNROLL = 4
PEER_ACT_CHUNKS = 2
V7X_VMEM_LIMIT_BYTES = 52 * 1024 * 1024

_NT = (((1,), (1,)), ((), ()))


def _params(semantics):
    return pltpu.CompilerParams(dimension_semantics=semantics, vmem_limit_bytes=V7X_VMEM_LIMIT_BYTES)


def _tile(n, t):
    if n <= t:
        return n
    t -= t % HEAD_DIM
    while n % t:
        t -= HEAD_DIM
    assert t > 0, n
    return t


def _split3(x):
    hi = x.astype(BF16)
    r1 = x - hi.astype(F32)
    mid = r1.astype(BF16)
    lo = (r1 - mid.astype(F32)).astype(BF16)
    return hi, mid, lo


def _ada_body(c_ref, w_ref, b_ref, o_ref, acc_ref):
    k = pl.program_id(1)

    @pl.when(k == 0)
    def _():
        acc_ref[...] = jnp.zeros_like(acc_ref)

    c = c_ref[...]
    c_act = c * jax.nn.sigmoid(c)
    acc_ref[...] += jnp.sum(w_ref[...] * c_act, axis=0, keepdims=True)

    @pl.when(k == pl.num_programs(1) - 1)
    def _():
        o_ref[...] = acc_ref[...] + b_ref[...]


def _ada_mod(c_col, w, layer, b_row, n):
    d = w.shape[1]
    tk, tn = _tile(d, 512), _tile(n, 2048)
    return pl.pallas_call(
        _ada_body,
        grid=(n // tn, d // tk),
        in_specs=[pl.BlockSpec((tk, 1), lambda j, k: (k, 0)),
                  pl.BlockSpec((None, tk, tn), lambda j, k: (layer, k, j)),
                  pl.BlockSpec((1, tn), lambda j, k: (0, j))],
        out_specs=pl.BlockSpec((1, tn), lambda j, k: (0, j)),
        out_shape=jax.ShapeDtypeStruct((1, n), F32),
        scratch_shapes=[pltpu.VMEM((1, tn), F32)],
        compiler_params=_params(("parallel", "arbitrary")),
        name="ada_mod",
    )(c_col, w, b_row)


def _norm_mod_body(x_ref, g_ref, sc_ref, sh_ref, o_ref, *maybe_ot_ref):
    x = x_ref[...]
    y = x * lax.rsqrt(jnp.mean(x * x, axis=-1, keepdims=True) + EPS)
    h = y * g_ref[...] * (1.0 + sc_ref[...]) + sh_ref[...]
    o_ref[...] = h.astype(o_ref.dtype)
    for ot_ref in maybe_ot_ref:
        ot_ref[...] = h.T.astype(ot_ref.dtype)


def _norm_mod(x, gain, scale, shift, with_transposed=False):
    s, d = x.shape
    tm = _tile(s, 256)
    row = pl.BlockSpec((1, d), lambda i: (0, 0))
    out_specs = [pl.BlockSpec((tm, d), lambda i: (i, 0))]
    out_shape = [jax.ShapeDtypeStruct((s, d), BF16)]
    if with_transposed:
        out_specs.append(pl.BlockSpec((d, tm), lambda i: (0, i)))
        out_shape.append(jax.ShapeDtypeStruct((d, s), BF16))
    outs = pl.pallas_call(
        _norm_mod_body,
        grid=(s // tm,),
        in_specs=[pl.BlockSpec((tm, d), lambda i: (i, 0)), row, row, row],
        out_specs=out_specs,
        out_shape=out_shape,
        compiler_params=_params(("parallel",)),
        name="norm_mod",
    )(x, gain, scale, shift)
    return outs if with_transposed else outs[0]


def _fox_proj_body(h_ref, w_ref, g_ref, o_ref):
    y = lax.dot_general(h_ref[...], w_ref[0].astype(BF16), _NT, preferred_element_type=F32)
    g = g_ref[...]
    for hh in range(y.shape[1] // HEAD_DIM):
        sl = slice(hh * HEAD_DIM, (hh + 1) * HEAD_DIM)
        yh = y[:, sl]
        r = lax.rsqrt(jnp.mean(yh * yh, axis=-1, keepdims=True) + EPS)
        o_ref[:, sl] = (yh * r * g[:, sl]).astype(o_ref.dtype)


def _w_rows_spec(layer, first_row, tn, d):
    assert first_row % 8 == 0 and tn % 8 == 0
    return pl.BlockSpec((pl.Element(1), pl.Element(tn), pl.Element(d)),
                        lambda i, j: (layer, pl.multiple_of(first_row + j * tn, 8), 0))


def _fox_proj(h, w_t, layer, gains):
    s, d = h.shape
    n = gains.shape[1]
    tm, tn = _tile(s, 1024), _tile(n, 512)
    return pl.pallas_call(
        _fox_proj_body,
        grid=(s // tm, n // tn),
        in_specs=[pl.BlockSpec((tm, d), lambda i, j: (i, 0)),
                  _w_rows_spec(layer, 0, tn, d),
                  pl.BlockSpec((1, tn), lambda i, j: (0, j))],
        out_specs=pl.BlockSpec((tm, tn), lambda i, j: (i, j)),
        out_shape=jax.ShapeDtypeStruct((s, n), BF16),
        compiler_params=_params(("parallel", "arbitrary")),
        name="fox_proj",
    )(h, w_t, gains)


def _fox_v_proj_body(h_ref, w_ref, o_ref):
    o_ref[...] = lax.dot_general(w_ref[0].astype(BF16), h_ref[...], _NT,
                                 preferred_element_type=F32).astype(o_ref.dtype)


def _fox_v_proj(h, w_t, layer, first_row, n):
    s, d = h.shape
    tm, tn = _tile(s, 1024), _tile(n, 512)
    return pl.pallas_call(
        _fox_v_proj_body,
        grid=(s // tm, n // tn),
        in_specs=[pl.BlockSpec((tm, d), lambda i, j: (i, 0)),
                  _w_rows_spec(layer, first_row, tn, d)],
        out_specs=pl.BlockSpec((tn, tm), lambda i, j: (j, i)),
        out_shape=jax.ShapeDtypeStruct((n, s), BF16),
        compiler_params=_params(("parallel", "arbitrary")),
        name="fox_v_proj",
    )(h, w_t)


def _hgrn_proj_body(h_ref, w_ref, c_ref, aw_ref, ab_ref, o_ref, mod_ref):
    o_ref[...] = lax.dot_general(h_ref[...], w_ref[0].astype(BF16), _NT,
                                 preferred_element_type=F32).astype(o_ref.dtype)
    c = c_ref[...]
    mod_ref[...] = jnp.sum(aw_ref[...] * (c * jax.nn.sigmoid(c)), axis=0, keepdims=True) + ab_ref[...]


def _hgrn_proj(h, w_t, layer, first_row, n, out_dtype, c_col, ada_w, ada_b_row, mod_first_col):
    s, d = h.shape
    tm, tn = _tile(s, 1024), _tile(n, 512)
    n_i, n_j = s // tm, n // tn
    n_mod = ada_w.shape[2] - mod_first_col
    cps = n_mod // (n_i * n_j)
    assert cps * n_i * n_j == n_mod and cps % HEAD_DIM == 0 and mod_first_col % cps == 0
    first_blk = mod_first_col // cps
    return pl.pallas_call(
        _hgrn_proj_body,
        grid=(n_i, n_j),
        in_specs=[pl.BlockSpec((tm, d), lambda i, j: (i, 0)),
                  _w_rows_spec(layer, first_row, tn, d),
                  pl.BlockSpec((d, 1), lambda i, j: (0, 0)),
                  pl.BlockSpec((None, d, cps), lambda i, j: (layer, 0, first_blk + i * n_j + j)),
                  pl.BlockSpec((1, cps), lambda i, j: (0, first_blk + i * n_j + j))],
        out_specs=[pl.BlockSpec((tm, tn), lambda i, j: (i, j)),
                   pl.BlockSpec((1, cps), lambda i, j: (0, i * n_j + j))],
        out_shape=[jax.ShapeDtypeStruct((s, n), out_dtype),
                   jax.ShapeDtypeStruct((1, n_mod), F32)],
        compiler_params=_params(("parallel", "arbitrary")),
        name="hgrn_proj",
    )(h, w_t, c_col, ada_w, ada_b_row)


def _fox_gate_body(h_ref, w_ref, b_ref, o_ref, carry_ref):
    i = pl.program_id(0)

    @pl.when(i == 0)
    def _():
        carry_ref[...] = jnp.zeros_like(carry_ref)

    logit = jnp.dot(h_ref[...], w_ref[...], preferred_element_type=F32) + b_ref[...]
    log_f = jnp.minimum(logit, 0.0) - jnp.log1p(jnp.exp(-jnp.abs(logit)))
    ts = log_f.shape[0]
    lower = (lax.broadcasted_iota(jnp.int32, (ts, ts), 1)
             <= lax.broadcasted_iota(jnp.int32, (ts, ts), 0)).astype(BF16)
    csum = sum(jnp.dot(lower, part, preferred_element_type=F32) for part in _split3(log_f))
    csum = csum + carry_ref[...]
    carry_ref[...] = csum[ts - 1:ts, :]
    scaled = csum * LOG2_E
    for hd in range(o_ref.shape[0]):
        o_ref[hd] = jnp.broadcast_to(scaled[:, hd:hd + 1], (ts, HEAD_DIM))


def _fox_gate(h, w_pad, bias_pad, n_heads):
    s, d = h.shape
    ts = _tile(s, 512)
    return pl.pallas_call(
        _fox_gate_body,
        grid=(s // ts,),
        in_specs=[pl.BlockSpec((ts, d), lambda i: (i, 0)),
                  pl.BlockSpec((d, HEAD_DIM), lambda i: (0, 0)),
                  pl.BlockSpec((1, HEAD_DIM), lambda i: (0, 0))],
        out_specs=pl.BlockSpec((n_heads, ts, HEAD_DIM), lambda i: (0, i, 0)),
        out_shape=jax.ShapeDtypeStruct((n_heads, s, HEAD_DIM), F32),
        scratch_shapes=[pltpu.VMEM((1, HEAD_DIM), F32)],
        compiler_params=_params(("arbitrary",)),
        name="fox_gate",
    )(h, w_pad, bias_pad)


def _fox_attn_body(q_ref, k_ref, vt_ref, cb_ref, *refs, t, n_tables):
    table_refs, o_ref = refs[:n_tables], refs[n_tables]
    table_out_refs = refs[n_tables + 1:2 * n_tables + 1]
    sa_ref, sb_ref, m_ref, acc_ref = refs[2 * n_tables + 1:]
    for src_ref, dst_ref in zip(table_refs, table_out_refs):
        dst_ref[...] = src_ref[...].astype(dst_ref.dtype)

    i = pl.program_id(1)
    q = q_ref[...]
    q_start = pl.multiple_of(i * t, t)
    c_first = cb_ref[pl.ds(q_start, 8), :][0:1]
    m_ref[...] = jnp.full_like(m_ref, MASKED)
    acc_ref[...] = jnp.zeros_like(acc_ref)
    group = min(t, ATTN_LANE_GROUP)
    ones_rows = jnp.ones((ATTN_ONES_ROWS, t), BF16)

    def scores(j):
        k_start = pl.multiple_of(j * t, t)
        return lax.dot_general(k_ref[pl.ds(k_start, t), :], q, _NT, preferred_element_type=F32)

    def consume(j, sc_ref, masked):
        k_start = pl.multiple_of(j * t, t)
        bias = c_first - cb_ref[pl.ds(k_start, t), :]
        bias = jnp.concatenate([bias] * (group // HEAD_DIM), axis=1)
        v_t = jnp.concatenate([vt_ref[:, pl.ds(k_start, t)], ones_rows], axis=0)
        for g in range(t // group):
            lanes = slice(g * group, (g + 1) * group)
            s = sc_ref[:, lanes] + bias
            if masked:
                k_pos = lax.broadcasted_iota(jnp.int32, (t, group), 0)
                q_pos = lax.broadcasted_iota(jnp.int32, (t, group), 1) + g * group
                s = jnp.where(k_pos <= q_pos, s, MASKED)
            m_old = m_ref[:, lanes]
            m_new = jnp.maximum(m_old, jnp.max(s, axis=0, keepdims=True))
            alpha = jnp.exp2(m_old - m_new)
            p = jnp.exp2(s - m_new)
            acc_ref[:, lanes] = alpha * acc_ref[:, lanes] + jnp.dot(
                v_t, p.astype(BF16), preferred_element_type=F32)
            m_ref[:, lanes] = m_new

    bufs = (sa_ref, sb_ref)
    sa_ref[...] = scores(0)

    def run(j0, n_blocks, last_is_diagonal):
        for b in range(n_blocks):
            is_last = b == n_blocks - 1
            if not (is_last and last_is_diagonal):
                bufs[(b + 1) % 2][...] = scores(j0 + b + 1)
            consume(j0 + b, bufs[b % 2], masked=is_last and last_is_diagonal)

    def unrolled(jj, carry):
        run(jj * ATTN_UNROLL, ATTN_UNROLL, last_is_diagonal=False)
        return carry

    n_unrolled = i // ATTN_UNROLL
    lax.fori_loop(0, n_unrolled, unrolled, 0)
    for rem in range(ATTN_UNROLL):
        @pl.when(i - n_unrolled * ATTN_UNROLL == rem)
        def _(rem=rem):
            run(n_unrolled * ATTN_UNROLL, rem + 1, last_is_diagonal=True)

    o_ref[...] = (acc_ref[:HEAD_DIM, :] / acc_ref[HEAD_DIM:HEAD_DIM + 1, :]).T.astype(o_ref.dtype)


def _fox_attention(qk, v_t, cum_f, n_heads, tables, layer):
    s = qk.shape[0]
    t = _tile(s, 512)
    n_q = s // t
    steps = n_heads * n_q
    table_in, table_out, table_shape = [], [], []
    for tab in tables:
        rows, d = tab.shape[1:]
        assert rows % steps == 0 and (rows // steps) % 16 == 0, (rows, steps)
        table_in.append(pl.BlockSpec((None, rows // steps, d), lambda h, i: (layer, h * n_q + i, 0)))
        table_out.append(pl.BlockSpec((rows // steps, d), lambda h, i: (h * n_q + i, 0)))
        table_shape.append(jax.ShapeDtypeStruct((rows, d), BF16))
    return pl.pallas_call(
        functools.partial(_fox_attn_body, t=t, n_tables=len(tables)),
        grid=(n_heads, n_q),
        in_specs=[pl.BlockSpec((t, HEAD_DIM), lambda h, i: (i, h)),
                  pl.BlockSpec((s, HEAD_DIM), lambda h, i: (0, n_heads + h)),
                  pl.BlockSpec((HEAD_DIM, s), lambda h, i: (h, 0)),
                  pl.BlockSpec((None, s, HEAD_DIM), lambda h, i: (h, 0, 0))] + table_in,
        out_specs=[pl.BlockSpec((t, HEAD_DIM), lambda h, i: (i, h))] + table_out,
        out_shape=[jax.ShapeDtypeStruct((s, n_heads * HEAD_DIM), BF16)] + table_shape,
        scratch_shapes=[pltpu.VMEM((t, t), F32), pltpu.VMEM((t, t), F32),
                        pltpu.VMEM((1, t), F32),
                        pltpu.VMEM((HEAD_DIM + ATTN_ONES_ROWS, t), F32)],
        compiler_params=_params(("parallel", "arbitrary")),
        name="fox_attention",
    )(qk, qk, v_t, cum_f, *tables)


def _hgrn_head(hq, hf, val, hg, lb, gain, state_t):
    forget = lb + (1.0 - lb) * jax.nn.sigmoid(hf)
    log_f = jnp.log(forget)
    key = 1.0 - forget
    query = hq * jax.nn.sigmoid(hq) * (HEAD_DIM ** -0.5)
    val_b = val.astype(BF16)
    c = log_f.shape[0]

    row = lax.broadcasted_iota(jnp.int32, (c, c), 0)
    col = lax.broadcasted_iota(jnp.int32, (c, c), 1)
    lower = (col <= row).astype(BF16)
    b = sum(jnp.dot(lower, part, preferred_element_type=F32) for part in _split3(log_f))
    b_last = b[c - 1:c, :]

    out = lax.dot_general((query * jnp.exp(b)).astype(BF16), state_t.astype(BF16), _NT,
                          preferred_element_type=F32)

    t_idx = lax.broadcasted_iota(jnp.int32, (c, HEAD_DIM), 0)
    sep = row ^ col
    scores = jnp.where(sep == 0, lax.dot_general(query.astype(BF16), key.astype(BF16), _NT,
                                                 preferred_element_type=F32), 0.0)
    half = c // 2
    while half >= 1:
        blk = 2 * half
        if half >= 4:
            ref_rows = jnp.broadcast_to(b.reshape(c // blk, blk, HEAD_DIM)[:, half - 1:half, :],
                                        (c // blk, blk, HEAD_DIM)).reshape(c, HEAD_DIM)
        elif half == 2:
            pos = t_idx & 3
            ref_rows = jnp.where(pos == 0, pltpu.roll(b, c - 1, 0),
                                 jnp.where(pos == 2, pltpu.roll(b, 1, 0),
                                           jnp.where(pos == 3, pltpu.roll(b, 2, 0), b)))
        else:
            ref_rows = jnp.where((t_idx & 1) == 1, pltpu.roll(b, 1, 0), b)
        in_upper = (t_idx & half) != 0
        x = jnp.where(in_upper, query, key) * jnp.exp(-jnp.abs(b - ref_rows))
        a_side = jnp.where(in_upper, x, 0.0).astype(BF16)
        b_side = jnp.where(in_upper, 0.0, x).astype(BF16)
        level = lax.dot_general(a_side, b_side, _NT, preferred_element_type=F32)
        scores = scores + (level if blk == c else jnp.where(sep < blk, level, 0.0))
        half //= 2
    out = out + jnp.dot(scores.astype(BF16), val_b, preferred_element_type=F32)

    key_dec = (key * jnp.exp(b_last - b)).astype(BF16)
    new_state_t = state_t * jnp.exp(b_last) + jnp.dot(val.T.astype(BF16), key_dec,
                                                      preferred_element_type=F32)

    normed = out * lax.rsqrt(jnp.mean(out * out, axis=-1, keepdims=True) + EPS) * gain
    return normed * (hg * jax.nn.sigmoid(hg)), new_state_t


def _hgrn_body(q_ref, f_ref, i_ref, g_ref, lb_ref, gain_ref, o_ref, state_ref, *, layer):
    @pl.when(pl.program_id(1) == 0)
    def _():
        state_ref[...] = jnp.zeros_like(state_ref)

    for hh in range(state_ref.shape[0]):
        sl = slice(hh * HEAD_DIM, (hh + 1) * HEAD_DIM)
        table = lb_ref[:, sl]
        e = jnp.exp(table - jnp.max(table, axis=0, keepdims=True))
        lb = jnp.sum(e[:layer + 1], axis=0, keepdims=True) / jnp.sum(e, axis=0, keepdims=True)
        out, state_ref[hh] = _hgrn_head(q_ref[:, sl], f_ref[:, sl], i_ref[:, sl], g_ref[:, sl],
                                        lb, gain_ref[:, sl], state_ref[hh])
        o_ref[:, sl] = out.astype(o_ref.dtype)


def _hgrn(proj, lb_table, out_gain, n_heads, layer):
    s = proj.shape[0]
    c = _tile(s, HGRN_CHUNK)
    n_slots = lb_table.shape[0]
    group = HGRN_HEAD_GROUP if n_heads % HGRN_HEAD_GROUP == 0 else 1
    n_groups = n_heads // group
    width = group * HEAD_DIM

    def col(block):
        return pl.BlockSpec((c, width), lambda g, t: (t, block * n_groups + g))

    return pl.pallas_call(
        functools.partial(_hgrn_body, layer=layer),
        grid=(n_groups, s // c),
        in_specs=[col(0), col(1), col(2), col(3),
                  pl.BlockSpec((n_slots, width), lambda g, t: (0, g)),
                  pl.BlockSpec((1, width), lambda g, t: (0, g))],
        out_specs=pl.BlockSpec((c, width), lambda g, t: (t, g)),
        out_shape=jax.ShapeDtypeStruct((s, n_heads * HEAD_DIM), BF16),
        scratch_shapes=[pltpu.VMEM((group, HEAD_DIM, HEAD_DIM), F32)],
        compiler_params=_params(("parallel", "arbitrary")),
        name="hgrn2",
    )(proj, proj, proj, proj, lb_table, out_gain)


def _out_proj_body(a_ref, b_ref, wa_ref, wb_ref, x_ref, g_ref, o_ref):
    mix = (jnp.dot(a_ref[...], wa_ref[0].astype(BF16), preferred_element_type=F32)
           + jnp.dot(b_ref[...], wb_ref[0].astype(BF16), preferred_element_type=F32))
    o_ref[...] = x_ref[...] + g_ref[...] * mix


def _out_proj(fox, hgrn, w_out, layer, x, gate):
    s, d = x.shape
    tm, tn = _tile(s, 1024), _tile(d, 512)
    ka, kb = fox.shape[1], hgrn.shape[1]

    def w_rows(first_row, k):
        return pl.BlockSpec((pl.Element(1), pl.Element(k), pl.Element(tn)),
                            lambda i, j: (layer, first_row, pl.multiple_of(j * tn, HEAD_DIM)))

    return pl.pallas_call(
        _out_proj_body,
        grid=(s // tm, d // tn),
        in_specs=[pl.BlockSpec((tm, ka), lambda i, j: (i, 0)),
                  pl.BlockSpec((tm, kb), lambda i, j: (i, 0)),
                  w_rows(0, ka),
                  w_rows(ka, kb),
                  pl.BlockSpec((tm, tn), lambda i, j: (i, j)),
                  pl.BlockSpec((1, tn), lambda i, j: (0, j))],
        out_specs=pl.BlockSpec((tm, tn), lambda i, j: (i, j)),
        out_shape=jax.ShapeDtypeStruct((s, d), F32),
        compiler_params=_params(("parallel", "arbitrary")),
        name="out_proj",
    )(fox, hgrn, w_out, w_out, x, gate)


PEER_NTOP = PEER_TOPK + 1
PEER_TOP_ROWS = 24
PEER_NCAND = 52


def _top_rows(arrays, k, exact_ties, ranked=()):
    vals = list(arrays)
    rows = [[] for _ in vals]
    ranks = {n: jnp.full(vals[n].shape, float(vals[n].shape[0]), F32) for n in ranked}
    for r in range(k):
        for n, v in enumerate(vals):
            m = jnp.max(v, axis=0, keepdims=True)
            rows[n].append(m)
            hit = v == m
            if exact_ties:
                idx = lax.broadcasted_iota(jnp.int32, v.shape, 0)
                first = jnp.min(jnp.where(hit, idx, v.shape[0]), axis=0, keepdims=True)
                hit = idx == first
            vals[n] = jnp.where(hit, -jnp.inf, v)
            if n in ranks:
                ranks[n] = jnp.where(hit, float(r), ranks[n])
    return rows, vals, ranks


def _n_finite(v):
    return jnp.sum(jnp.where(v > -jnp.inf, 1.0, 0.0), axis=0, keepdims=True)


def _peer_thresholds(s0s, s1s, a_ref, b_ref, exact_ties):
    nh = len(s0s)
    rows, left, ranks = _top_rows(s0s + s1s, PEER_NTOP, exact_ties, ranked=range(nh, 2 * nh))
    excess = sum(jnp.abs(_n_finite(v) - (PEER_KEYS - PEER_NTOP)) for v in left)

    r24 = lax.broadcasted_iota(jnp.int32, a_ref.shape[1:], 0)
    r8 = lax.broadcasted_iota(jnp.int32, (8, a_ref.shape[2]), 0)
    ninf = -jnp.inf

    def grp(x, r, lo, hi):
        return jnp.where((r >= lo) & (r <= hi), x, ninf)

    cands = []
    for hd in range(nh):
        a_ref[hd] = jnp.full(a_ref.shape[1:], ninf, F32)
        b_ref[hd] = jnp.full(b_ref.shape[1:], ninf, F32)
        for r in range(PEER_NTOP):
            a_ref[hd, r:r + 1, :] = rows[hd][r]
            b_ref[hd, r:r + 1, :] = rows[nh + hd][r]
        a24, b24 = a_ref[hd], b_ref[hd]
        a8, b8 = a24[:8], b24[:8]
        cands.append(jnp.concatenate([
            grp(a24[0:1] + b24, r24, 0, 16),
            grp(a24 + b24[0:1], r24, 1, 16),
            grp(a8[1:2] + b8, r8, 1, 7),
            grp(a8 + b8[1:2], r8, 2, 7),
            grp(a8[2:3] + b8, r8, 2, 4),
            grp(a8 + b8[2:3], r8, 3, 4),
            grp(a8[3:4] + b8, r8, 3, 3),
        ], axis=0))
    tops, left, _ = _top_rows(cands, PEER_NTOP, exact_ties)
    excess = excess + sum(jnp.abs(_n_finite(v) - (PEER_NCAND - PEER_NTOP)) for v in left)

    outs = []
    for hd in range(nh):
        top = tops[hd]
        tau = 0.5 * (top[PEER_TOPK - 1] + top[PEER_TOPK])
        z = sum(jnp.exp(t - top[0]) for t in top[:PEER_TOPK])
        thr = tau - s0s[hd]
        cnt = sum(jnp.where(b_r >= thr, 1.0, 0.0) for b_r in rows[nh + hd])
        outs.append((cnt,
                     jnp.exp(s0s[hd] - rows[hd][0]) * (0.5 / z),
                     ranks[nh + hd],
                     jnp.exp(s1s[hd] - rows[nh + hd][0])))
    return outs, excess


def _peer_route_body(h_ref, wq_ref, sk_ref, cnt_ref, e0_ref, rank_ref, e1_ref, a_ref, b_ref):
    nh = sk_ref.shape[0]
    q = jnp.dot(h_ref[...], wq_ref[...].astype(BF16), preferred_element_type=F32).astype(BF16)
    s0s, s1s = [], []
    for hd in range(nh):
        c0 = 2 * hd * PEER_KEYS
        s0s.append(lax.dot_general(sk_ref[hd, 0], q[:, c0:c0 + PEER_KEYS], _NT,
                                   preferred_element_type=F32))
        s1s.append(lax.dot_general(sk_ref[hd, 1], q[:, c0 + PEER_KEYS:c0 + 2 * PEER_KEYS], _NT,
                                   preferred_element_type=F32))

    def emit(outs):
        for hd, (cnt, e0, rank1, e1) in enumerate(outs):
            cnt_ref[hd] = cnt
            e0_ref[hd] = e0
            rank_ref[hd] = rank1.astype(rank_ref.dtype)
            e1_ref[hd] = e1.astype(e1_ref.dtype)

    outs, excess = _peer_thresholds(s0s, s1s, a_ref, b_ref, exact_ties=False)
    emit(outs)

    @pl.when(jnp.max(excess) > 0.0)
    def _():
        emit(_peer_thresholds(s0s, s1s, a_ref, b_ref, exact_ties=True)[0])


def _peer_route(h, wq, layer, sub_keys):
    s, d = h.shape
    ph = sub_keys.shape[0]
    tt = _tile(s, 128)
    nh = 4 if ph % 4 == 0 else 1
    out_spec = pl.BlockSpec((nh, PEER_KEYS, tt), lambda p, t: (p, 0, t))
    out_sds = [jax.ShapeDtypeStruct((ph, PEER_KEYS, s), dt) for dt in (F32, F32, BF16, BF16)]
    return pl.pallas_call(
        _peer_route_body,
        grid=(ph // nh, s // tt),
        in_specs=[pl.BlockSpec((tt, d), lambda p, t: (t, 0)),
                  pl.BlockSpec((None, d, nh * 2 * PEER_KEYS), lambda p, t: (layer, 0, p)),
                  pl.BlockSpec((nh, 2, PEER_KEYS, PEER_KEYS), lambda p, t: (p, 0, 0, 0))],
        out_specs=[out_spec] * 4,
        out_shape=out_sds,
        scratch_shapes=[pltpu.VMEM((nh, PEER_TOP_ROWS, tt), F32), pltpu.VMEM((nh, PEER_TOP_ROWS, tt), F32)],
        compiler_params=_params(("parallel", "arbitrary")),
        name="peer_route",
    )(h, wq, sub_keys)


def _bf16_rows(row):
    tile = jnp.broadcast_to(row, (16, row.shape[1])).astype(BF16)
    return jnp.concatenate([tile] * (PEER_KEYS // 16), axis=0)


def _peer_act_body(u_ref, ht_ref, cnt_ref, e0_ref, rank_ref, e1_ref, w_ref, *, rows):
    n_heads = rank_ref.shape[0]
    te = u_ref.shape[0]
    for ch in range(te // rows):
        z = jnp.dot(u_ref[ch * rows:(ch + 1) * rows, :], ht_ref[...], preferred_element_type=F32)
        for ci in range(rows // PEER_KEYS):
            ii = ch * (rows // PEER_KEYS) + ci
            zi = z[ci * PEER_KEYS:(ci + 1) * PEER_KEYS, :]
            act = zi * (1.0 + lax.erf(zi * SQRT_HALF))
            gate = jnp.zeros(zi.shape, BF16)
            for hd in range(n_heads):
                sel = rank_ref[hd] < _bf16_rows(cnt_ref[hd, ii:ii + 1, :])
                gate = gate + (jnp.where(sel, e1_ref[hd], jnp.zeros((), BF16))
                               * _bf16_rows(e0_ref[hd, ii:ii + 1, :]))
            w_ref[:, ii * PEER_KEYS:(ii + 1) * PEER_KEYS] = (act * gate.astype(F32)).T.astype(w_ref.dtype)


def _peer_act(u, h_t, cnt, e0, rank, e1):
    n_exp, d = u.shape
    s = h_t.shape[1]
    ph = cnt.shape[0]
    te = 8 * PEER_KEYS
    tt = _tile(s, 512)
    assert n_exp % te == 0
    per_i = pl.BlockSpec((ph, 8, tt), lambda t, e: (0, e, t))
    per_j = pl.BlockSpec((ph, PEER_KEYS, tt), lambda t, e: (0, 0, t))
    return pl.pallas_call(
        functools.partial(_peer_act_body, rows=te // PEER_ACT_CHUNKS),
        grid=(s // tt, n_exp // te),
        in_specs=[pl.BlockSpec((te, d), lambda t, e: (e, 0)),
                  pl.BlockSpec((d, tt), lambda t, e: (0, t)),
                  per_i, per_i, per_j, per_j],
        out_specs=pl.BlockSpec((tt, te), lambda t, e: (t, e)),
        out_shape=jax.ShapeDtypeStruct((s, n_exp), BF16),
        compiler_params=_params(("parallel", "arbitrary")),
        name="peer_act",
    )(u, h_t, cnt, e0, rank, e1)


def _peer_down_body(w_ref, v_ref, x_ref, g_ref, o_ref, acc_ref):
    k = pl.program_id(2)

    @pl.when(k == 0)
    def _():
        acc_ref[...] = jnp.zeros_like(acc_ref)

    acc_ref[...] += jnp.dot(w_ref[...], v_ref[...], preferred_element_type=F32)

    @pl.when(k == pl.num_programs(2) - 1)
    def _():
        o_ref[...] = x_ref[...] + g_ref[...] * acc_ref[...]


def _peer_down(w, v, x, gate):
    s, n_exp = w.shape
    d = v.shape[1]
    tm, tn, tk = _tile(s, 1024), _tile(d, 1024), _tile(n_exp, 2048)
    return pl.pallas_call(
        _peer_down_body,
        grid=(s // tm, d // tn, n_exp // tk),
        in_specs=[pl.BlockSpec((tm, tk), lambda i, j, k: (i, k)),
                  pl.BlockSpec((tk, tn), lambda i, j, k: (k, j)),
                  pl.BlockSpec((tm, tn), lambda i, j, k: (i, j)),
                  pl.BlockSpec((1, tn), lambda i, j, k: (0, j))],
        out_specs=pl.BlockSpec((tm, tn), lambda i, j, k: (i, j)),
        out_shape=jax.ShapeDtypeStruct((s, d), F32),
        scratch_shapes=[pltpu.VMEM((tm, tn), F32)],
        compiler_params=_params(("parallel", "parallel", "arbitrary")),
        name="peer_down",
    )(w, v, x, gate)


def kernel(x, c, ada_w, ada_b, norm1_gain, norm2_gain, w_in, fox_f_bias, fox_q_gain, fox_k_gain,
           hgrn_lower_bounds, hgrn_out_gain, w_out, peer_w_query, peer_sub_keys, peer_u, peer_v):
    batch, seq, d = x.shape
    assert batch == 1, "one sequence per call"
    depth = ada_w.shape[0]
    fox_heads = fox_f_bias.shape[-1]
    hgrn_heads = hgrn_out_gain.shape[1]
    fox_width, hgrn_width = fox_heads * HEAD_DIM, hgrn_heads * HEAD_DIM
    assert fox_q_gain.shape[-1] == HEAD_DIM and hgrn_out_gain.shape[-1] == HEAD_DIM
    assert w_in.shape[-1] == 3 * fox_width + fox_heads + 4 * hgrn_width
    assert peer_sub_keys.shape[2:] == (2, PEER_KEYS, PEER_KEYS) and peer_u.shape[1] == PEER_KEYS ** 2
    o3 = 3 * fox_width
    o4 = o3 + fox_heads

    xs = x.reshape(seq, d)
    c_col = c.reshape(d, 1)
    w_t = jnp.swapaxes(w_in, 1, 2)
    for layer in range(depth):
        ada_b_row = ada_b[layer].reshape(1, -1)
        mod_early = _ada_mod(c_col, ada_w, layer, ada_b_row, 2 * d)
        shift1, scale1 = mod_early[:, :d], mod_early[:, d:]

        h = _norm_mod(xs, norm1_gain[layer].reshape(1, d), scale1, shift1)
        qk_gains = jnp.concatenate([fox_q_gain[layer].reshape(1, fox_width) * (HEAD_DIM ** -0.5 * LOG2_E),
                                    fox_k_gain[layer].reshape(1, fox_width)], axis=1)
        qk = _fox_proj(h, w_t, layer, qk_gains)
        v_t = _fox_v_proj(h, w_t, layer, 2 * fox_width, fox_width)
        pad = HEAD_DIM - fox_heads
        cum_f = _fox_gate(h, jnp.pad(w_t[layer, o3:o4].T, ((0, 0), (0, pad))).astype(BF16),
                          jnp.pad(fox_f_bias[layer].reshape(1, fox_heads), ((0, 0), (0, pad))), fox_heads)
        fox_out, u_bf16, v_bf16 = _fox_attention(qk, v_t, cum_f, fox_heads, (peer_u, peer_v), layer)

        hproj, mod_late = _hgrn_proj(h, w_t, layer, o4, 4 * hgrn_width, F32, c_col, ada_w, ada_b_row, 2 * d)
        gate1, shift2, scale2, gate2 = (mod_late[:, n * d:(n + 1) * d] for n in range(4))
        hgrn_out = _hgrn(hproj, hgrn_lower_bounds, hgrn_out_gain[layer].reshape(1, hgrn_width),
                         hgrn_heads, layer)

        xs = _out_proj(fox_out, hgrn_out, w_out, layer, xs, gate1)

        h, h_t = _norm_mod(xs, norm2_gain[layer].reshape(1, d), scale2, shift2, with_transposed=True)
        cnt, e0, rank1, e1 = _peer_route(h, peer_w_query, layer, peer_sub_keys[layer].astype(BF16))
        w_act = _peer_act(u_bf16, h_t, cnt, e0, rank1, e1)
        xs = _peer_down(w_act, v_bf16, xs, gate2)
    return xs.reshape(batch, seq, d)
```

```python
import functools

import jax
import jax.numpy as jnp
from jax import lax
from jax.experimental import pallas as pl
from jax.experimental.pallas import tpu as pltpu

F32 = jnp.float32
BF16 = jnp.bfloat16

HEAD_DIM = 128
PEER_KEYS = 128
PEER_TOPK = 16
HGRN_CHUNK = 128
HGRN_HEAD_GROUP = 16
EPS = 1e-6
MASKED = -1e30
SQRT_HALF = 0.7071067811865476
LOG2_E = 1.4426950408889634
ATTN_LANE_GROUP = 256
ATTN_ONES_ROWS = 16
ATTN_UNROLL = 4
PEER_ACT_CHUNKS = 2
V7X_VMEM_LIMIT_BYTES = 52 * 1024 * 1024

_NT = (((1,), (1,)), ((), ()))


def _params(semantics):
    return pltpu.CompilerParams(dimension_semantics=semantics, vmem_limit_bytes=V7X_VMEM_LIMIT_BYTES)


def _tile(n, t):
    if n <= t:
        return n
    t -= t % HEAD_DIM
    while n % t:
        t -= HEAD_DIM
    assert t > 0, n
    return t


def _split3(x):
    hi = x.astype(BF16)
    r1 = x - hi.astype(F32)
    mid = r1.astype(BF16)
    lo = (r1 - mid.astype(F32)).astype(BF16)
    return hi, mid, lo


def _ada_body(c_ref, w_ref, b_ref, o_ref, acc_ref):
    k = pl.program_id(1)

    @pl.when(k == 0)
    def _():
        acc_ref[...] = jnp.zeros_like(acc_ref)

    c = c_ref[...]
    c_act = c * jax.nn.sigmoid(c)
    acc_ref[...] += jnp.sum(w_ref[...] * c_act, axis=0, keepdims=True)

    @pl.when(k == pl.num_programs(1) - 1)
    def _():
        o_ref[...] = acc_ref[...] + b_ref[...]


def _ada_mod(c_col, w, layer, b_row, n):
    d = w.shape[1]
    tk, tn = _tile(d, 512), _tile(n, 2048)
    return pl.pallas_call(
        _ada_body,
        grid=(n // tn, d // tk),
        in_specs=[pl.BlockSpec((tk, 1), lambda j, k: (k, 0)),
                  pl.BlockSpec((None, tk, tn), lambda j, k: (layer, k, j)),
                  pl.BlockSpec((1, tn), lambda j, k: (0, j))],
        out_specs=pl.BlockSpec((1, tn), lambda j, k: (0, j)),
        out_shape=jax.ShapeDtypeStruct((1, n), F32),
        scratch_shapes=[pltpu.VMEM((1, tn), F32)],
        compiler_params=_params(("parallel", "arbitrary")),
        name="ada_mod",
    )(c_col, w, b_row)


def _norm_mod_body(x_ref, g_ref, sc_ref, sh_ref, o_ref, *maybe_ot_ref):
    x = x_ref[...]
    y = x * lax.rsqrt(jnp.mean(x * x, axis=-1, keepdims=True) + EPS)
    h = y * g_ref[...] * (1.0 + sc_ref[...]) + sh_ref[...]
    o_ref[...] = h.astype(o_ref.dtype)
    for ot_ref in maybe_ot_ref:
        ot_ref[...] = h.T.astype(ot_ref.dtype)


def _norm_mod(x, gain, scale, shift, with_transposed=False):
    s, d = x.shape
    tm = _tile(s, 256)
    row = pl.BlockSpec((1, d), lambda i: (0, 0))
    out_specs = [pl.BlockSpec((tm, d), lambda i: (i, 0))]
    out_shape = [jax.ShapeDtypeStruct((s, d), BF16)]
    if with_transposed:
        out_specs.append(pl.BlockSpec((d, tm), lambda i: (0, i)))
        out_shape.append(jax.ShapeDtypeStruct((d, s), BF16))
    outs = pl.pallas_call(
        _norm_mod_body,
        grid=(s // tm,),
        in_specs=[pl.BlockSpec((tm, d), lambda i: (i, 0)), row, row, row],
        out_specs=out_specs,
        out_shape=out_shape,
        compiler_params=_params(("parallel",)),
        name="norm_mod",
    )(x, gain, scale, shift)
    return outs if with_transposed else outs[0]


def _fox_proj_body(h_ref, w_ref, g_ref, o_ref):
    y = lax.dot_general(h_ref[...], w_ref[0].astype(BF16), _NT, preferred_element_type=F32)
    g = g_ref[...]
    for hh in range(y.shape[1] // HEAD_DIM):
        sl = slice(hh * HEAD_DIM, (hh + 1) * HEAD_DIM)
        yh = y[:, sl]
        r = lax.rsqrt(jnp.mean(yh * yh, axis=-1, keepdims=True) + EPS)
        o_ref[:, sl] = (yh * r * g[:, sl]).astype(o_ref.dtype)


def _w_rows_spec(layer, first_row, tn, d):
    assert first_row % 8 == 0 and tn % 8 == 0
    return pl.BlockSpec((pl.Element(1), pl.Element(tn), pl.Element(d)),
                        lambda i, j: (layer, pl.multiple_of(first_row + j * tn, 8), 0))


def _fox_proj(h, w_t, layer, gains):
    s, d = h.shape
    n = gains.shape[1]
    tm, tn = _tile(s, 1024), _tile(n, 512)
    return pl.pallas_call(
        _fox_proj_body,
        grid=(s // tm, n // tn),
        in_specs=[pl.BlockSpec((tm, d), lambda i, j: (i, 0)),
                  _w_rows_spec(layer, 0, tn, d),
                  pl.BlockSpec((1, tn), lambda i, j: (0, j))],
        out_specs=pl.BlockSpec((tm, tn), lambda i, j: (i, j)),
        out_shape=jax.ShapeDtypeStruct((s, n), BF16),
        compiler_params=_params(("parallel", "arbitrary")),
        name="fox_proj",
    )(h, w_t, gains)


def _fox_v_proj_body(h_ref, w_ref, o_ref):
    o_ref[...] = lax.dot_general(w_ref[0].astype(BF16), h_ref[...], _NT,
                                 preferred_element_type=F32).astype(o_ref.dtype)


def _fox_v_proj(h, w_t, layer, first_row, n):
    s, d = h.shape
    tm, tn = _tile(s, 1024), _tile(n, 512)
    return pl.pallas_call(
        _fox_v_proj_body,
        grid=(s // tm, n // tn),
        in_specs=[pl.BlockSpec((tm, d), lambda i, j: (i, 0)),
                  _w_rows_spec(layer, first_row, tn, d)],
        out_specs=pl.BlockSpec((tn, tm), lambda i, j: (j, i)),
        out_shape=jax.ShapeDtypeStruct((n, s), BF16),
        compiler_params=_params(("parallel", "arbitrary")),
        name="fox_v_proj",
    )(h, w_t)


def _hgrn_proj_body(h_ref, w_ref, c_ref, aw_ref, ab_ref, o_ref, mod_ref):
    o_ref[...] = lax.dot_general(h_ref[...], w_ref[0].astype(BF16), _NT,
                                 preferred_element_type=F32).astype(o_ref.dtype)
    c = c_ref[...]
    mod_ref[...] = jnp.sum(aw_ref[...] * (c * jax.nn.sigmoid(c)), axis=0, keepdims=True) + ab_ref[...]


def _hgrn_proj(h, w_t, layer, first_row, n, out_dtype, c_col, ada_w, ada_b_row, mod_first_col):
    s, d = h.shape
    tm, tn = _tile(s, 1024), _tile(n, 512)
    n_i, n_j = s // tm, n // tn
    n_mod = ada_w.shape[2] - mod_first_col
    cps = n_mod // (n_i * n_j)
    assert cps * n_i * n_j == n_mod and cps % HEAD_DIM == 0 and mod_first_col % cps == 0
    first_blk = mod_first_col // cps
    return pl.pallas_call(
        _hgrn_proj_body,
        grid=(n_i, n_j),
        in_specs=[pl.BlockSpec((tm, d), lambda i, j: (i, 0)),
                  _w_rows_spec(layer, first_row, tn, d),
                  pl.BlockSpec((d, 1), lambda i, j: (0, 0)),
                  pl.BlockSpec((None, d, cps), lambda i, j: (layer, 0, first_blk + i * n_j + j)),
                  pl.BlockSpec((1, cps), lambda i, j: (0, first_blk + i * n_j + j))],
        out_specs=[pl.BlockSpec((tm, tn), lambda i, j: (i, j)),
                   pl.BlockSpec((1, cps), lambda i, j: (0, i * n_j + j))],
        out_shape=[jax.ShapeDtypeStruct((s, n), out_dtype),
                   jax.ShapeDtypeStruct((1, n_mod), F32)],
        compiler_params=_params(("parallel", "arbitrary")),
        name="hgrn_proj",
    )(h, w_t, c_col, ada_w, ada_b_row)


def _fox_gate_body(h_ref, w_ref, b_ref, o_ref, carry_ref):
    i = pl.program_id(0)

    @pl.when(i == 0)
    def _():
        carry_ref[...] = jnp.zeros_like(carry_ref)

    logit = jnp.dot(h_ref[...], w_ref[...], preferred_element_type=F32) + b_ref[...]
    log_f = jnp.minimum(logit, 0.0) - jnp.log1p(jnp.exp(-jnp.abs(logit)))
    ts = log_f.shape[0]
    lower = (lax.broadcasted_iota(jnp.int32, (ts, ts), 1)
             <= lax.broadcasted_iota(jnp.int32, (ts, ts), 0)).astype(BF16)
    csum = sum(jnp.dot(lower, part, preferred_element_type=F32) for part in _split3(log_f))
    csum = csum + carry_ref[...]
    carry_ref[...] = csum[ts - 1:ts, :]
    scaled = csum * LOG2_E
    for hd in range(o_ref.shape[0]):
        o_ref[hd] = jnp.broadcast_to(scaled[:, hd:hd + 1], (ts, HEAD_DIM))


def _fox_gate(h, w_pad, bias_pad, n_heads):
    s, d = h.shape
    ts = _tile(s, 512)
    return pl.pallas_call(
        _fox_gate_body,
        grid=(s // ts,),
        in_specs=[pl.BlockSpec((ts, d), lambda i: (i, 0)),
                  pl.BlockSpec((d, HEAD_DIM), lambda i: (0, 0)),
                  pl.BlockSpec((1, HEAD_DIM), lambda i: (0, 0))],
        out_specs=pl.BlockSpec((n_heads, ts, HEAD_DIM), lambda i: (0, i, 0)),
        out_shape=jax.ShapeDtypeStruct((n_heads, s, HEAD_DIM), F32),
        scratch_shapes=[pltpu.VMEM((1, HEAD_DIM), F32)],
        compiler_params=_params(("arbitrary",)),
        name="fox_gate",
    )(h, w_pad, bias_pad)


def _fox_attn_body(q_ref, k_ref, vt_ref, cb_ref, *refs, t, n_tables):
    table_refs, o_ref = refs[:n_tables], refs[n_tables]
    table_out_refs = refs[n_tables + 1:2 * n_tables + 1]
    sa_ref, sb_ref, m_ref, acc_ref = refs[2 * n_tables + 1:]
    for src_ref, dst_ref in zip(table_refs, table_out_refs):
        dst_ref[...] = src_ref[...].astype(dst_ref.dtype)

    i = pl.program_id(1)
    q = q_ref[...]
    q_start = pl.multiple_of(i * t, t)
    c_first = cb_ref[pl.ds(q_start, 8), :][0:1]
    m_ref[...] = jnp.full_like(m_ref, MASKED)
    acc_ref[...] = jnp.zeros_like(acc_ref)
    group = min(t, ATTN_LANE_GROUP)
    ones_rows = jnp.ones((ATTN_ONES_ROWS, t), BF16)

    def scores(j):
        k_start = pl.multiple_of(j * t, t)
        return lax.dot_general(k_ref[pl.ds(k_start, t), :], q, _NT, preferred_element_type=F32)

    def consume(j, sc_ref, masked):
        k_start = pl.multiple_of(j * t, t)
        bias = c_first - cb_ref[pl.ds(k_start, t), :]
        bias = jnp.concatenate([bias] * (group // HEAD_DIM), axis=1)
        v_t = jnp.concatenate([vt_ref[:, pl.ds(k_start, t)], ones_rows], axis=0)
        for g in range(t // group):
            lanes = slice(g * group, (g + 1) * group)
            n_keys = (g + 1) * group if masked else t
            s = sc_ref[:n_keys, lanes] + bias[:n_keys]
            if masked:
                k_pos = lax.broadcasted_iota(jnp.int32, (n_keys, group), 0)
                q_pos = lax.broadcasted_iota(jnp.int32, (n_keys, group), 1) + g * group
                s = jnp.where(k_pos <= q_pos, s, MASKED)
            m_old = m_ref[:, lanes]
            m_new = jnp.maximum(m_old, jnp.max(s, axis=0, keepdims=True))
            alpha = jnp.exp2(m_old - m_new)
            p = jnp.exp2(s - m_new)
            acc_ref[:, lanes] = alpha * acc_ref[:, lanes] + jnp.dot(
                v_t[:, :n_keys], p.astype(BF16), preferred_element_type=F32)
            m_ref[:, lanes] = m_new

    bufs = (sa_ref, sb_ref)
    sa_ref[...] = scores(0)

    def run(j0, n_blocks, last_is_diagonal):
        for b in range(n_blocks):
            is_last = b == n_blocks - 1
            if not (is_last and last_is_diagonal):
                bufs[(b + 1) % 2][...] = scores(j0 + b + 1)
            consume(j0 + b, bufs[b % 2], masked=is_last and last_is_diagonal)

    def unrolled(jj, carry):
        run(jj * ATTN_UNROLL, ATTN_UNROLL, last_is_diagonal=False)
        return carry

    n_unrolled = i // ATTN_UNROLL
    lax.fori_loop(0, n_unrolled, unrolled, 0)
    for rem in range(ATTN_UNROLL):
        @pl.when(i - n_unrolled * ATTN_UNROLL == rem)
        def _(rem=rem):
            run(n_unrolled * ATTN_UNROLL, rem + 1, last_is_diagonal=True)

    o_ref[...] = (acc_ref[:HEAD_DIM, :] / acc_ref[HEAD_DIM:HEAD_DIM + 1, :]).T.astype(o_ref.dtype)


def _fox_attention(qk, v_t, cum_f, n_heads, tables, layer):
    s = qk.shape[0]
    t = _tile(s, 512)
    n_q = s // t
    steps = n_heads * n_q
    table_in, table_out, table_shape = [], [], []
    for tab in tables:
        rows, d = tab.shape[1:]
        assert rows % steps == 0 and (rows // steps) % 16 == 0, (rows, steps)
        table_in.append(pl.BlockSpec((None, rows // steps, d), lambda h, i: (layer, h * n_q + i, 0)))
        table_out.append(pl.BlockSpec((rows // steps, d), lambda h, i: (h * n_q + i, 0)))
        table_shape.append(jax.ShapeDtypeStruct((rows, d), BF16))
    return pl.pallas_call(
        functools.partial(_fox_attn_body, t=t, n_tables=len(tables)),
        grid=(n_heads, n_q),
        in_specs=[pl.BlockSpec((t, HEAD_DIM), lambda h, i: (i, h)),
                  pl.BlockSpec((s, HEAD_DIM), lambda h, i: (0, n_heads + h)),
                  pl.BlockSpec((HEAD_DIM, s), lambda h, i: (h, 0)),
                  pl.BlockSpec((None, s, HEAD_DIM), lambda h, i: (h, 0, 0))] + table_in,
        out_specs=[pl.BlockSpec((t, HEAD_DIM), lambda h, i: (i, h))] + table_out,
        out_shape=[jax.ShapeDtypeStruct((s, n_heads * HEAD_DIM), BF16)] + table_shape,
        scratch_shapes=[pltpu.VMEM((t, t), F32), pltpu.VMEM((t, t), F32),
                        pltpu.VMEM((1, t), F32),
                        pltpu.VMEM((HEAD_DIM + ATTN_ONES_ROWS, t), F32)],
        compiler_params=_params(("parallel", "arbitrary")),
        name="fox_attention",
    )(qk, qk, v_t, cum_f, *tables)


def _hgrn_head(hq, hf, val, hg, lb, gain, state_t):
    forget = lb + (1.0 - lb) * jax.nn.sigmoid(hf)
    log_f = jnp.log(forget)
    key = 1.0 - forget
    query = hq * jax.nn.sigmoid(hq) * (HEAD_DIM ** -0.5)
    val_b = val.astype(BF16)
    c = log_f.shape[0]

    row = lax.broadcasted_iota(jnp.int32, (c, c), 0)
    col = lax.broadcasted_iota(jnp.int32, (c, c), 1)
    lower = (col <= row).astype(BF16)
    b = sum(jnp.dot(lower, part, preferred_element_type=F32) for part in _split3(log_f))
    b_last = b[c - 1:c, :]

    out = lax.dot_general((query * jnp.exp(b)).astype(BF16), state_t.astype(BF16), _NT,
                          preferred_element_type=F32)

    t_idx = lax.broadcasted_iota(jnp.int32, (c, HEAD_DIM), 0)
    sep = row ^ col
    scores = jnp.where(sep == 0, lax.dot_general(query.astype(BF16), key.astype(BF16), _NT,
                                                 preferred_element_type=F32), 0.0)
    half = c // 2
    while half >= 1:
        blk = 2 * half
        if half >= 4:
            ref_rows = jnp.broadcast_to(b.reshape(c // blk, blk, HEAD_DIM)[:, half - 1:half, :],
                                        (c // blk, blk, HEAD_DIM)).reshape(c, HEAD_DIM)
        elif half == 2:
            pos = t_idx & 3
            ref_rows = jnp.where(pos == 0, pltpu.roll(b, c - 1, 0),
                                 jnp.where(pos == 2, pltpu.roll(b, 1, 0),
                                           jnp.where(pos == 3, pltpu.roll(b, 2, 0), b)))
        else:
            ref_rows = jnp.where((t_idx & 1) == 1, pltpu.roll(b, 1, 0), b)
        in_upper = (t_idx & half) != 0
        x = jnp.where(in_upper, query, key) * jnp.exp(-jnp.abs(b - ref_rows))
        a_side = jnp.where(in_upper, x, 0.0).astype(BF16)
        b_side = jnp.where(in_upper, 0.0, x).astype(BF16)
        level = lax.dot_general(a_side, b_side, _NT, preferred_element_type=F32)
        scores = scores + (level if blk == c else jnp.where(sep < blk, level, 0.0))
        half //= 2
    out = out + jnp.dot(scores.astype(BF16), val_b, preferred_element_type=F32)

    key_dec = (key * jnp.exp(b_last - b)).astype(BF16)
    new_state_t = state_t * jnp.exp(b_last) + jnp.dot(val.T.astype(BF16), key_dec,
                                                      preferred_element_type=F32)

    normed = out * lax.rsqrt(jnp.mean(out * out, axis=-1, keepdims=True) + EPS) * gain
    return normed * (hg * jax.nn.sigmoid(hg)), new_state_t


def _hgrn_body(q_ref, f_ref, i_ref, g_ref, lb_ref, gain_ref, o_ref, state_ref, *, layer):
    @pl.when(pl.program_id(1) == 0)
    def _():
        state_ref[...] = jnp.zeros_like(state_ref)

    for hh in range(state_ref.shape[0]):
        sl = slice(hh * HEAD_DIM, (hh + 1) * HEAD_DIM)
        table = lb_ref[:, sl]
        e = jnp.exp(table - jnp.max(table, axis=0, keepdims=True))
        lb = jnp.sum(e[:layer + 1], axis=0, keepdims=True) / jnp.sum(e, axis=0, keepdims=True)
        out, state_ref[hh] = _hgrn_head(q_ref[:, sl], f_ref[:, sl], i_ref[:, sl], g_ref[:, sl],
                                        lb, gain_ref[:, sl], state_ref[hh])
        o_ref[:, sl] = out.astype(o_ref.dtype)


def _hgrn(proj, lb_table, out_gain, n_heads, layer):
    s = proj.shape[0]
    c = _tile(s, HGRN_CHUNK)
    n_slots = lb_table.shape[0]
    group = HGRN_HEAD_GROUP if n_heads % HGRN_HEAD_GROUP == 0 else 1
    n_groups = n_heads // group
    width = group * HEAD_DIM

    def col(block):
        return pl.BlockSpec((c, width), lambda g, t: (t, block * n_groups + g))

    return pl.pallas_call(
        functools.partial(_hgrn_body, layer=layer),
        grid=(n_groups, s // c),
        in_specs=[col(0), col(1), col(2), col(3),
                  pl.BlockSpec((n_slots, width), lambda g, t: (0, g)),
                  pl.BlockSpec((1, width), lambda g, t: (0, g))],
        out_specs=pl.BlockSpec((c, width), lambda g, t: (t, g)),
        out_shape=jax.ShapeDtypeStruct((s, n_heads * HEAD_DIM), BF16),
        scratch_shapes=[pltpu.VMEM((group, HEAD_DIM, HEAD_DIM), F32)],
        compiler_params=_params(("parallel", "arbitrary")),
        name="hgrn2",
    )(proj, proj, proj, proj, lb_table, out_gain)


def _out_proj_body(a_ref, b_ref, wa_ref, wb_ref, x_ref, g_ref, o_ref):
    mix = (jnp.dot(a_ref[...], wa_ref[0].astype(BF16), preferred_element_type=F32)
           + jnp.dot(b_ref[...], wb_ref[0].astype(BF16), preferred_element_type=F32))
    o_ref[...] = x_ref[...] + g_ref[...] * mix


def _out_proj(fox, hgrn, w_out, layer, x, gate):
    s, d = x.shape
    tm, tn = _tile(s, 1024), _tile(d, 512)
    ka, kb = fox.shape[1], hgrn.shape[1]

    def w_rows(first_row, k):
        return pl.BlockSpec((pl.Element(1), pl.Element(k), pl.Element(tn)),
                            lambda i, j: (layer, first_row, pl.multiple_of(j * tn, HEAD_DIM)))

    return pl.pallas_call(
        _out_proj_body,
        grid=(s // tm, d // tn),
        in_specs=[pl.BlockSpec((tm, ka), lambda i, j: (i, 0)),
                  pl.BlockSpec((tm, kb), lambda i, j: (i, 0)),
                  w_rows(0, ka),
                  w_rows(ka, kb),
                  pl.BlockSpec((tm, tn), lambda i, j: (i, j)),
                  pl.BlockSpec((1, tn), lambda i, j: (0, j))],
        out_specs=pl.BlockSpec((tm, tn), lambda i, j: (i, j)),
        out_shape=jax.ShapeDtypeStruct((s, d), F32),
        compiler_params=_params(("parallel", "arbitrary")),
        name="out_proj",
    )(fox, hgrn, w_out, w_out, x, gate)


PEER_NTOP = PEER_TOPK + 1
PEER_TOP_ROWS = 24
PEER_NCAND = 52


def _top_rows(arrays, k, exact_ties, ranked=()):
    vals = list(arrays)
    rows = [[] for _ in vals]
    ranks = {n: jnp.full(vals[n].shape, float(vals[n].shape[0]), F32) for n in ranked}
    for r in range(k):
        for n, v in enumerate(vals):
            m = jnp.max(v, axis=0, keepdims=True)
            rows[n].append(m)
            hit = v == m
            if exact_ties:
                idx = lax.broadcasted_iota(jnp.int32, v.shape, 0)
                first = jnp.min(jnp.where(hit, idx, v.shape[0]), axis=0, keepdims=True)
                hit = idx == first
            vals[n] = jnp.where(hit, -jnp.inf, v)
            if n in ranks:
                ranks[n] = jnp.where(hit, float(r), ranks[n])
    return rows, vals, ranks


def _n_finite(v):
    return jnp.sum(jnp.where(v > -jnp.inf, 1.0, 0.0), axis=0, keepdims=True)


def _peer_thresholds(s0s, s1s, a_ref, b_ref, exact_ties):
    nh = len(s0s)
    rows, left, ranks = _top_rows(s0s + s1s, PEER_NTOP, exact_ties, ranked=range(nh, 2 * nh))
    excess = sum(jnp.abs(_n_finite(v) - (PEER_KEYS - PEER_NTOP)) for v in left)

    r24 = lax.broadcasted_iota(jnp.int32, a_ref.shape[1:], 0)
    r8 = lax.broadcasted_iota(jnp.int32, (8, a_ref.shape[2]), 0)
    ninf = -jnp.inf

    def grp(x, r, lo, hi):
        return jnp.where((r >= lo) & (r <= hi), x, ninf)

    cands = []
    for hd in range(nh):
        a_ref[hd] = jnp.full(a_ref.shape[1:], ninf, F32)
        b_ref[hd] = jnp.full(b_ref.shape[1:], ninf, F32)
        for r in range(PEER_NTOP):
            a_ref[hd, r:r + 1, :] = rows[hd][r]
            b_ref[hd, r:r + 1, :] = rows[nh + hd][r]
        a24, b24 = a_ref[hd], b_ref[hd]
        a8, b8 = a24[:8], b24[:8]
        cands.append(jnp.concatenate([
            grp(a24[0:1] + b24, r24, 0, 16),
            grp(a24 + b24[0:1], r24, 1, 16),
            grp(a8[1:2] + b8, r8, 1, 7),
            grp(a8 + b8[1:2], r8, 2, 7),
            grp(a8[2:3] + b8, r8, 2, 4),
            grp(a8 + b8[2:3], r8, 3, 4),
            grp(a8[3:4] + b8, r8, 3, 3),
        ], axis=0))
    tops, left, _ = _top_rows(cands, PEER_NTOP, exact_ties)
    excess = excess + sum(jnp.abs(_n_finite(v) - (PEER_NCAND - PEER_NTOP)) for v in left)

    outs = []
    for hd in range(nh):
        top = tops[hd]
        tau = 0.5 * (top[PEER_TOPK - 1] + top[PEER_TOPK])
        z = sum(jnp.exp(t - top[0]) for t in top[:PEER_TOPK])
        thr = tau - s0s[hd]
        cnt = sum(jnp.where(b_r >= thr, 1.0, 0.0) for b_r in rows[nh + hd])
        outs.append((cnt,
                     jnp.exp(s0s[hd] - rows[hd][0]) * (0.5 / z),
                     ranks[nh + hd],
                     jnp.exp(s1s[hd] - rows[nh + hd][0])))
    return outs, excess


def _peer_route_body(h_ref, wq_ref, sk_ref, cnt_ref, e0_ref, rank_ref, e1_ref, a_ref, b_ref):
    nh = sk_ref.shape[0]
    q = jnp.dot(h_ref[...], wq_ref[...].astype(BF16), preferred_element_type=F32).astype(BF16)
    s0s, s1s = [], []
    for hd in range(nh):
        c0 = 2 * hd * PEER_KEYS
        s0s.append(lax.dot_general(sk_ref[hd, 0], q[:, c0:c0 + PEER_KEYS], _NT,
                                   preferred_element_type=F32))
        s1s.append(lax.dot_general(sk_ref[hd, 1], q[:, c0 + PEER_KEYS:c0 + 2 * PEER_KEYS], _NT,
                                   preferred_element_type=F32))

    def emit(outs):
        for hd, (cnt, e0, rank1, e1) in enumerate(outs):
            cnt_ref[hd] = cnt
            e0_ref[hd] = e0
            rank_ref[hd] = rank1.astype(rank_ref.dtype)
            e1_ref[hd] = e1.astype(e1_ref.dtype)

    outs, excess = _peer_thresholds(s0s, s1s, a_ref, b_ref, exact_ties=False)
    emit(outs)

    @pl.when(jnp.max(excess) > 0.0)
    def _():
        emit(_peer_thresholds(s0s, s1s, a_ref, b_ref, exact_ties=True)[0])


def _peer_route(h, wq, layer, sub_keys):
    s, d = h.shape
    ph = sub_keys.shape[0]
    tt = _tile(s, 128)
    nh = 4 if ph % 4 == 0 else 1
    out_spec = pl.BlockSpec((nh, PEER_KEYS, tt), lambda p, t: (p, 0, t))
    out_sds = [jax.ShapeDtypeStruct((ph, PEER_KEYS, s), dt) for dt in (F32, F32, BF16, BF16)]
    return pl.pallas_call(
        _peer_route_body,
        grid=(ph // nh, s // tt),
        in_specs=[pl.BlockSpec((tt, d), lambda p, t: (t, 0)),
                  pl.BlockSpec((None, d, nh * 2 * PEER_KEYS), lambda p, t: (layer, 0, p)),
                  pl.BlockSpec((nh, 2, PEER_KEYS, PEER_KEYS), lambda p, t: (p, 0, 0, 0))],
        out_specs=[out_spec] * 4,
        out_shape=out_sds,
        scratch_shapes=[pltpu.VMEM((nh, PEER_TOP_ROWS, tt), F32), pltpu.VMEM((nh, PEER_TOP_ROWS, tt), F32)],
        compiler_params=_params(("parallel", "arbitrary")),
        name="peer_route",
    )(h, wq, sub_keys)


def _bf16_rows(row):
    tile = jnp.broadcast_to(row, (16, row.shape[1])).astype(BF16)
    return jnp.concatenate([tile] * (PEER_KEYS // 16), axis=0)


def _peer_act_body(u_ref, ht_ref, cnt_ref, e0_ref, rank_ref, e1_ref, w_ref, *, rows):
    n_heads = rank_ref.shape[0]
    te = u_ref.shape[0]
    for ch in range(te // rows):
        z = jnp.dot(u_ref[ch * rows:(ch + 1) * rows, :], ht_ref[...], preferred_element_type=F32)
        for ci in range(rows // PEER_KEYS):
            ii = ch * (rows // PEER_KEYS) + ci
            zi = z[ci * PEER_KEYS:(ci + 1) * PEER_KEYS, :]
            act = zi * (1.0 + lax.erf(zi * SQRT_HALF))
            gate = jnp.zeros(zi.shape, BF16)
            for hd in range(n_heads):
                sel = rank_ref[hd] < _bf16_rows(cnt_ref[hd, ii:ii + 1, :])
                gate = gate + (jnp.where(sel, e1_ref[hd], jnp.zeros((), BF16))
                               * _bf16_rows(e0_ref[hd, ii:ii + 1, :]))
            w_ref[:, ii * PEER_KEYS:(ii + 1) * PEER_KEYS] = (act * gate.astype(F32)).T.astype(w_ref.dtype)


def _peer_act(u, h_t, cnt, e0, rank, e1):
    n_exp, d = u.shape
    s = h_t.shape[1]
    ph = cnt.shape[0]
    te = 8 * PEER_KEYS
    tt = _tile(s, 512)
    assert n_exp % te == 0
    per_i = pl.BlockSpec((ph, 8, tt), lambda t, e: (0, e, t))
    per_j = pl.BlockSpec((ph, PEER_KEYS, tt), lambda t, e: (0, 0, t))
    return pl.pallas_call(
        functools.partial(_peer_act_body, rows=te // PEER_ACT_CHUNKS),
        grid=(s // tt, n_exp // te),
        in_specs=[pl.BlockSpec((te, d), lambda t, e: (e, 0)),
                  pl.BlockSpec((d, tt), lambda t, e: (0, t)),
                  per_i, per_i, per_j, per_j],
        out_specs=pl.BlockSpec((tt, te), lambda t, e: (t, e)),
        out_shape=jax.ShapeDtypeStruct((s, n_exp), BF16),
        compiler_params=_params(("parallel", "arbitrary")),
        name="peer_act",
    )(u, h_t, cnt, e0, rank, e1)


def _peer_down_body(w_ref, v_ref, x_ref, g_ref, o_ref, acc_ref):
    k = pl.program_id(2)

    @pl.when(k == 0)
    def _():
        acc_ref[...] = jnp.zeros_like(acc_ref)

    acc_ref[...] += jnp.dot(w_ref[...], v_ref[...], preferred_element_type=F32)

    @pl.when(k == pl.num_programs(2) - 1)
    def _():
        o_ref[...] = x_ref[...] + g_ref[...] * acc_ref[...]


def _peer_down(w, v, x, gate):
    s, n_exp = w.shape
    d = v.shape[1]
    tm, tn, tk = _tile(s, 1024), _tile(d, 1024), _tile(n_exp, 2048)
    return pl.pallas_call(
        _peer_down_body,
        grid=(s // tm, d // tn, n_exp // tk),
        in_specs=[pl.BlockSpec((tm, tk), lambda i, j, k: (i, k)),
                  pl.BlockSpec((tk, tn), lambda i, j, k: (k, j)),
                  pl.BlockSpec((tm, tn), lambda i, j, k: (i, j)),
                  pl.BlockSpec((1, tn), lambda i, j, k: (0, j))],
        out_specs=pl.BlockSpec((tm, tn), lambda i, j, k: (i, j)),
        out_shape=jax.ShapeDtypeStruct((s, d), F32),
        scratch_shapes=[pltpu.VMEM((tm, tn), F32)],
        compiler_params=_params(("parallel", "parallel", "arbitrary")),
        name="peer_down",
    )(w, v, x, gate)


def kernel(x, c, ada_w, ada_b, norm1_gain, norm2_gain, w_in, fox_f_bias, fox_q_gain, fox_k_gain,
           hgrn_lower_bounds, hgrn_out_gain, w_out, peer_w_query, peer_sub_keys, peer_u, peer_v):
    batch, seq, d = x.shape
    assert batch == 1, "one sequence per call"
    depth = ada_w.shape[0]
    fox_heads = fox_f_bias.shape[-1]
    hgrn_heads = hgrn_out_gain.shape[1]
    fox_width, hgrn_width = fox_heads * HEAD_DIM, hgrn_heads * HEAD_DIM
    assert fox_q_gain.shape[-1] == HEAD_DIM and hgrn_out_gain.shape[-1] == HEAD_DIM
    assert w_in.shape[-1] == 3 * fox_width + fox_heads + 4 * hgrn_width
    assert peer_sub_keys.shape[2:] == (2, PEER_KEYS, PEER_KEYS) and peer_u.shape[1] == PEER_KEYS ** 2
    o3 = 3 * fox_width
    o4 = o3 + fox_heads

    xs = x.reshape(seq, d)
    c_col = c.reshape(d, 1)
    w_t = jnp.swapaxes(w_in, 1, 2)
    for layer in range(depth):
        ada_b_row = ada_b[layer].reshape(1, -1)
        mod_early = _ada_mod(c_col, ada_w, layer, ada_b_row, 2 * d)
        shift1, scale1 = mod_early[:, :d], mod_early[:, d:]

        h = _norm_mod(xs, norm1_gain[layer].reshape(1, d), scale1, shift1)
        qk_gains = jnp.concatenate([fox_q_gain[layer].reshape(1, fox_width) * (HEAD_DIM ** -0.5 * LOG2_E),
                                    fox_k_gain[layer].reshape(1, fox_width)], axis=1)
        qk = _fox_proj(h, w_t, layer, qk_gains)
        v_t = _fox_v_proj(h, w_t, layer, 2 * fox_width, fox_width)
        pad = HEAD_DIM - fox_heads
        cum_f = _fox_gate(h, jnp.pad(w_t[layer, o3:o4].T, ((0, 0), (0, pad))).astype(BF16),
                          jnp.pad(fox_f_bias[layer].reshape(1, fox_heads), ((0, 0), (0, pad))), fox_heads)
        fox_out, u_bf16, v_bf16 = _fox_attention(qk, v_t, cum_f, fox_heads, (peer_u, peer_v), layer)

        hproj, mod_late = _hgrn_proj(h, w_t, layer, o4, 4 * hgrn_width, F32, c_col, ada_w, ada_b_row, 2 * d)
        gate1, shift2, scale2, gate2 = (mod_late[:, n * d:(n + 1) * d] for n in range(4))
        hgrn_out = _hgrn(hproj, hgrn_lower_bounds, hgrn_out_gain[layer].reshape(1, hgrn_width),
                         hgrn_heads, layer)

        xs = _out_proj(fox_out, hgrn_out, w_out, layer, xs, gate1)

        h, h_t = _norm_mod(xs, norm2_gain[layer].reshape(1, d), scale2, shift2, with_transposed=True)
        cnt, e0, rank1, e1 = _peer_route(h, peer_w_query, layer, peer_sub_keys[layer].astype(BF16))
        w_act = _peer_act(u_bf16, h_t, cnt, e0, rank1, e1)
        xs = _peer_down(w_act, v_bf16, xs, gate2)
    return xs.reshape(batch, seq, d)
```

```python
import functools

import jax
import jax.numpy as jnp
from jax import lax
from jax.experimental import pallas as pl
from jax.experimental.pallas import tpu as pltpu

F32 = jnp.float32
BF16 = jnp.bfloat16

HEAD_DIM = 128
PEER_KEYS = 128
PEER_TOPK = 16
HGRN_CHUNK = 128
HGRN_HEAD_GROUP = 16
EPS = 1e-6
MASKED = -1e30
SQRT_HALF = 0.7071067811865476
LOG2_E = 1.4426950408889634
ATTN_LANE_GROUP = 256
ATTN_ONES_ROWS = 16
ATTN_UNROLL = 4
PEER_ACT_CHUNKS = 8
V7X_VMEM_LIMIT_BYTES = 52 * 1024 * 1024

_NT = (((1,), (1,)), ((), ()))


def _params(semantics):
    return pltpu.CompilerParams(dimension_semantics=semantics, vmem_limit_bytes=V7X_VMEM_LIMIT_BYTES)


def _tile(n, t):
    if n <= t:
        return n
    t -= t % HEAD_DIM
    while n % t:
        t -= HEAD_DIM
    assert t > 0, n
    return t


def _split3(x):
    hi = x.astype(BF16)
    r1 = x - hi.astype(F32)
    mid = r1.astype(BF16)
    lo = (r1 - mid.astype(F32)).astype(BF16)
    return hi, mid, lo


def _ada_body(c_ref, w_ref, b_ref, o_ref, acc_ref):
    k = pl.program_id(1)

    @pl.when(k == 0)
    def _():
        acc_ref[...] = jnp.zeros_like(acc_ref)

    c = c_ref[...]
    c_act = c * jax.nn.sigmoid(c)
    acc_ref[...] += jnp.sum(w_ref[...] * c_act, axis=0, keepdims=True)

    @pl.when(k == pl.num_programs(1) - 1)
    def _():
        o_ref[...] = acc_ref[...] + b_ref[...]


def _ada_mod(c_col, w, layer, b_row, n):
    d = w.shape[1]
    tk, tn = _tile(d, 512), _tile(n, 2048)
    return pl.pallas_call(
        _ada_body,
        grid=(n // tn, d // tk),
        in_specs=[pl.BlockSpec((tk, 1), lambda j, k: (k, 0)),
                  pl.BlockSpec((None, tk, tn), lambda j, k: (layer, k, j)),
                  pl.BlockSpec((1, tn), lambda j, k: (0, j))],
        out_specs=pl.BlockSpec((1, tn), lambda j, k: (0, j)),
        out_shape=jax.ShapeDtypeStruct((1, n), F32),
        scratch_shapes=[pltpu.VMEM((1, tn), F32)],
        compiler_params=_params(("parallel", "arbitrary")),
        name="ada_mod",
    )(c_col, w, b_row)


def _norm_mod_body(x_ref, g_ref, sc_ref, sh_ref, o_ref, *maybe_ot_ref):
    x = x_ref[...]
    y = x * lax.rsqrt(jnp.mean(x * x, axis=-1, keepdims=True) + EPS)
    h = y * g_ref[...] * (1.0 + sc_ref[...]) + sh_ref[...]
    o_ref[...] = h.astype(o_ref.dtype)
    for ot_ref in maybe_ot_ref:
        ot_ref[...] = h.T.astype(ot_ref.dtype)


def _norm_mod(x, gain, scale, shift, with_transposed=False):
    s, d = x.shape
    tm = _tile(s, 256)
    row = pl.BlockSpec((1, d), lambda i: (0, 0))
    out_specs = [pl.BlockSpec((tm, d), lambda i: (i, 0))]
    out_shape = [jax.ShapeDtypeStruct((s, d), BF16)]
    if with_transposed:
        out_specs.append(pl.BlockSpec((d, tm), lambda i: (0, i)))
        out_shape.append(jax.ShapeDtypeStruct((d, s), BF16))
    outs = pl.pallas_call(
        _norm_mod_body,
        grid=(s // tm,),
        in_specs=[pl.BlockSpec((tm, d), lambda i: (i, 0)), row, row, row],
        out_specs=out_specs,
        out_shape=out_shape,
        compiler_params=_params(("parallel",)),
        name="norm_mod",
    )(x, gain, scale, shift)
    return outs if with_transposed else outs[0]


def _fox_proj_body(h_ref, w_ref, g_ref, o_ref):
    y = lax.dot_general(h_ref[...], w_ref[0].astype(BF16), _NT, preferred_element_type=F32)
    g = g_ref[...]
    for hh in range(y.shape[1] // HEAD_DIM):
        sl = slice(hh * HEAD_DIM, (hh + 1) * HEAD_DIM)
        yh = y[:, sl]
        r = lax.rsqrt(jnp.mean(yh * yh, axis=-1, keepdims=True) + EPS)
        o_ref[:, sl] = (yh * r * g[:, sl]).astype(o_ref.dtype)


def _w_rows_spec(layer, first_row, tn, d):
    assert first_row % 8 == 0 and tn % 8 == 0
    return pl.BlockSpec((pl.Element(1), pl.Element(tn), pl.Element(d)),
                        lambda i, j: (layer, pl.multiple_of(first_row + j * tn, 8), 0))


def _fox_proj(h, w_t, layer, gains):
    s, d = h.shape
    n = gains.shape[1]
    tm, tn = _tile(s, 1024), _tile(n, 512)
    return pl.pallas_call(
        _fox_proj_body,
        grid=(s // tm, n // tn),
        in_specs=[pl.BlockSpec((tm, d), lambda i, j: (i, 0)),
                  _w_rows_spec(layer, 0, tn, d),
                  pl.BlockSpec((1, tn), lambda i, j: (0, j))],
        out_specs=pl.BlockSpec((tm, tn), lambda i, j: (i, j)),
        out_shape=jax.ShapeDtypeStruct((s, n), BF16),
        compiler_params=_params(("parallel", "arbitrary")),
        name="fox_proj",
    )(h, w_t, gains)


def _fox_v_proj_body(h_ref, w_ref, o_ref):
    o_ref[...] = lax.dot_general(w_ref[0].astype(BF16), h_ref[...], _NT,
                                 preferred_element_type=F32).astype(o_ref.dtype)


def _fox_v_proj(h, w_t, layer, first_row, n):
    s, d = h.shape
    tm, tn = _tile(s, 1024), _tile(n, 512)
    return pl.pallas_call(
        _fox_v_proj_body,
        grid=(s // tm, n // tn),
        in_specs=[pl.BlockSpec((tm, d), lambda i, j: (i, 0)),
                  _w_rows_spec(layer, first_row, tn, d)],
        out_specs=pl.BlockSpec((tn, tm), lambda i, j: (j, i)),
        out_shape=jax.ShapeDtypeStruct((n, s), BF16),
        compiler_params=_params(("parallel", "arbitrary")),
        name="fox_v_proj",
    )(h, w_t)


def _hgrn_proj_body(h_ref, w_ref, c_ref, aw_ref, ab_ref, o_ref, mod_ref):
    o_ref[...] = lax.dot_general(h_ref[...], w_ref[0].astype(BF16), _NT,
                                 preferred_element_type=F32).astype(o_ref.dtype)
    c = c_ref[...]
    mod_ref[...] = jnp.sum(aw_ref[...] * (c * jax.nn.sigmoid(c)), axis=0, keepdims=True) + ab_ref[...]


def _hgrn_proj(h, w_t, layer, first_row, n, out_dtype, c_col, ada_w, ada_b_row, mod_first_col):
    s, d = h.shape
    tm, tn = _tile(s, 1024), _tile(n, 512)
    n_i, n_j = s // tm, n // tn
    n_mod = ada_w.shape[2] - mod_first_col
    cps = n_mod // (n_i * n_j)
    assert cps * n_i * n_j == n_mod and cps % HEAD_DIM == 0 and mod_first_col % cps == 0
    first_blk = mod_first_col // cps
    return pl.pallas_call(
        _hgrn_proj_body,
        grid=(n_i, n_j),
        in_specs=[pl.BlockSpec((tm, d), lambda i, j: (i, 0)),
                  _w_rows_spec(layer, first_row, tn, d),
                  pl.BlockSpec((d, 1), lambda i, j: (0, 0)),
                  pl.BlockSpec((None, d, cps), lambda i, j: (layer, 0, first_blk + i * n_j + j)),
                  pl.BlockSpec((1, cps), lambda i, j: (0, first_blk + i * n_j + j))],
        out_specs=[pl.BlockSpec((tm, tn), lambda i, j: (i, j)),
                   pl.BlockSpec((1, cps), lambda i, j: (0, i * n_j + j))],
        out_shape=[jax.ShapeDtypeStruct((s, n), out_dtype),
                   jax.ShapeDtypeStruct((1, n_mod), F32)],
        compiler_params=_params(("parallel", "arbitrary")),
        name="hgrn_proj",
    )(h, w_t, c_col, ada_w, ada_b_row)


def _fox_gate_body(h_ref, w_ref, b_ref, o_ref, carry_ref):
    i = pl.program_id(0)

    @pl.when(i == 0)
    def _():
        carry_ref[...] = jnp.zeros_like(carry_ref)

    logit = jnp.dot(h_ref[...], w_ref[...], preferred_element_type=F32) + b_ref[...]
    log_f = jnp.minimum(logit, 0.0) - jnp.log1p(jnp.exp(-jnp.abs(logit)))
    ts = log_f.shape[0]
    lower = (lax.broadcasted_iota(jnp.int32, (ts, ts), 1)
             <= lax.broadcasted_iota(jnp.int32, (ts, ts), 0)).astype(BF16)
    csum = sum(jnp.dot(lower, part, preferred_element_type=F32) for part in _split3(log_f))
    csum = csum + carry_ref[...]
    carry_ref[...] = csum[ts - 1:ts, :]
    scaled = csum * LOG2_E
    for hd in range(o_ref.shape[0]):
        o_ref[hd] = jnp.broadcast_to(scaled[:, hd:hd + 1], (ts, HEAD_DIM))


def _fox_gate(h, w_pad, bias_pad, n_heads):
    s, d = h.shape
    ts = _tile(s, 512)
    return pl.pallas_call(
        _fox_gate_body,
        grid=(s // ts,),
        in_specs=[pl.BlockSpec((ts, d), lambda i: (i, 0)),
                  pl.BlockSpec((d, HEAD_DIM), lambda i: (0, 0)),
                  pl.BlockSpec((1, HEAD_DIM), lambda i: (0, 0))],
        out_specs=pl.BlockSpec((n_heads, ts, HEAD_DIM), lambda i: (0, i, 0)),
        out_shape=jax.ShapeDtypeStruct((n_heads, s, HEAD_DIM), F32),
        scratch_shapes=[pltpu.VMEM((1, HEAD_DIM), F32)],
        compiler_params=_params(("arbitrary",)),
        name="fox_gate",
    )(h, w_pad, bias_pad)


def _fox_attn_body(q_ref, k_ref, vt_ref, cb_ref, *refs, t, n_tables):
    table_refs, o_ref = refs[:n_tables], refs[n_tables]
    table_out_refs = refs[n_tables + 1:2 * n_tables + 1]
    sa_ref, sb_ref, m_ref, acc_ref = refs[2 * n_tables + 1:]
    for src_ref, dst_ref in zip(table_refs, table_out_refs):
        dst_ref[...] = src_ref[...].astype(dst_ref.dtype)

    i = pl.program_id(1)
    q = q_ref[...]
    q_start = pl.multiple_of(i * t, t)
    c_first = cb_ref[pl.ds(q_start, 8), :][0:1]
    m_ref[...] = jnp.full_like(m_ref, MASKED)
    acc_ref[...] = jnp.zeros_like(acc_ref)
    group = min(t, ATTN_LANE_GROUP)
    ones_rows = jnp.ones((ATTN_ONES_ROWS, t), BF16)

    def scores(j):
        k_start = pl.multiple_of(j * t, t)
        return lax.dot_general(k_ref[pl.ds(k_start, t), :], q, _NT, preferred_element_type=F32)

    def consume(j, sc_ref, masked):
        k_start = pl.multiple_of(j * t, t)
        bias = c_first - cb_ref[pl.ds(k_start, t), :]
        bias = jnp.concatenate([bias] * (group // HEAD_DIM), axis=1)
        v_t = jnp.concatenate([vt_ref[:, pl.ds(k_start, t)], ones_rows], axis=0)
        for g in range(t // group):
            lanes = slice(g * group, (g + 1) * group)
            n_keys = (g + 1) * group if masked else t
            s = sc_ref[:n_keys, lanes] + bias[:n_keys]
            if masked:
                k_pos = lax.broadcasted_iota(jnp.int32, (n_keys, group), 0)
                q_pos = lax.broadcasted_iota(jnp.int32, (n_keys, group), 1) + g * group
                s = jnp.where(k_pos <= q_pos, s, MASKED)
            m_old = m_ref[:, lanes]
            m_new = jnp.maximum(m_old, jnp.max(s, axis=0, keepdims=True))
            alpha = jnp.exp2(m_old - m_new)
            p = jnp.exp2(s - m_new)
            acc_ref[:, lanes] = alpha * acc_ref[:, lanes] + jnp.dot(
                v_t[:, :n_keys], p.astype(BF16), preferred_element_type=F32)
            m_ref[:, lanes] = m_new

    bufs = (sa_ref, sb_ref)
    sa_ref[...] = scores(0)

    def run(j0, n_blocks, last_is_diagonal):
        for b in range(n_blocks):
            is_last = b == n_blocks - 1
            if not (is_last and last_is_diagonal):
                bufs[(b + 1) % 2][...] = scores(j0 + b + 1)
            consume(j0 + b, bufs[b % 2], masked=is_last and last_is_diagonal)

    def unrolled(jj, carry):
        run(jj * ATTN_UNROLL, ATTN_UNROLL, last_is_diagonal=False)
        return carry

    n_unrolled = i // ATTN_UNROLL
    lax.fori_loop(0, n_unrolled, unrolled, 0)
    for rem in range(ATTN_UNROLL):
        @pl.when(i - n_unrolled * ATTN_UNROLL == rem)
        def _(rem=rem):
            run(n_unrolled * ATTN_UNROLL, rem + 1, last_is_diagonal=True)

    o_ref[...] = (acc_ref[:HEAD_DIM, :] / acc_ref[HEAD_DIM:HEAD_DIM + 1, :]).T.astype(o_ref.dtype)


def _fox_attention(qk, v_t, cum_f, n_heads, tables, layer):
    s = qk.shape[0]
    t = _tile(s, 512)
    n_q = s // t
    steps = n_heads * n_q
    table_in, table_out, table_shape = [], [], []
    for tab in tables:
        rows, d = tab.shape[1:]
        assert rows % steps == 0 and (rows // steps) % 16 == 0, (rows, steps)
        table_in.append(pl.BlockSpec((None, rows // steps, d), lambda h, i: (layer, h * n_q + i, 0)))
        table_out.append(pl.BlockSpec((rows // steps, d), lambda h, i: (h * n_q + i, 0)))
        table_shape.append(jax.ShapeDtypeStruct((rows, d), BF16))
    return pl.pallas_call(
        functools.partial(_fox_attn_body, t=t, n_tables=len(tables)),
        grid=(n_heads, n_q),
        in_specs=[pl.BlockSpec((t, HEAD_DIM), lambda h, i: (i, h)),
                  pl.BlockSpec((s, HEAD_DIM), lambda h, i: (0, n_heads + h)),
                  pl.BlockSpec((HEAD_DIM, s), lambda h, i: (h, 0)),
                  pl.BlockSpec((None, s, HEAD_DIM), lambda h, i: (h, 0, 0))] + table_in,
        out_specs=[pl.BlockSpec((t, HEAD_DIM), lambda h, i: (i, h))] + table_out,
        out_shape=[jax.ShapeDtypeStruct((s, n_heads * HEAD_DIM), BF16)] + table_shape,
        scratch_shapes=[pltpu.VMEM((t, t), F32), pltpu.VMEM((t, t), F32),
                        pltpu.VMEM((1, t), F32),
                        pltpu.VMEM((HEAD_DIM + ATTN_ONES_ROWS, t), F32)],
        compiler_params=_params(("parallel", "arbitrary")),
        name="fox_attention",
    )(qk, qk, v_t, cum_f, *tables)


def _hgrn_head(hq, hf, val, hg, lb, gain, state_t):
    forget = lb + (1.0 - lb) * jax.nn.sigmoid(hf)
    log_f = jnp.log(forget)
    key = 1.0 - forget
    query = hq * jax.nn.sigmoid(hq) * (HEAD_DIM ** -0.5)
    val_b = val.astype(BF16)
    c = log_f.shape[0]

    row = lax.broadcasted_iota(jnp.int32, (c, c), 0)
    col = lax.broadcasted_iota(jnp.int32, (c, c), 1)
    lower = (col <= row).astype(BF16)
    b = sum(jnp.dot(lower, part, preferred_element_type=F32) for part in _split3(log_f))
    b_last = b[c - 1:c, :]

    out = lax.dot_general((query * jnp.exp(b)).astype(BF16), state_t.astype(BF16), _NT,
                          preferred_element_type=F32)

    t_idx = lax.broadcasted_iota(jnp.int32, (c, HEAD_DIM), 0)
    sep = row ^ col
    scores = jnp.where(sep == 0, lax.dot_general(query.astype(BF16), key.astype(BF16), _NT,
                                                 preferred_element_type=F32), 0.0)
    half = c // 2
    while half >= 1:
        blk = 2 * half
        if half >= 4:
            ref_rows = jnp.broadcast_to(b.reshape(c // blk, blk, HEAD_DIM)[:, half - 1:half, :],
                                        (c // blk, blk, HEAD_DIM)).reshape(c, HEAD_DIM)
        elif half == 2:
            pos = t_idx & 3
            ref_rows = jnp.where(pos == 0, pltpu.roll(b, c - 1, 0),
                                 jnp.where(pos == 2, pltpu.roll(b, 1, 0),
                                           jnp.where(pos == 3, pltpu.roll(b, 2, 0), b)))
        else:
            ref_rows = jnp.where((t_idx & 1) == 1, pltpu.roll(b, 1, 0), b)
        in_upper = (t_idx & half) != 0
        x = jnp.where(in_upper, query, key) * jnp.exp(-jnp.abs(b - ref_rows))
        a_side = jnp.where(in_upper, x, 0.0).astype(BF16)
        b_side = jnp.where(in_upper, 0.0, x).astype(BF16)
        level = lax.dot_general(a_side, b_side, _NT, preferred_element_type=F32)
        scores = scores + (level if blk == c else jnp.where(sep < blk, level, 0.0))
        half //= 2
    out = out + jnp.dot(scores.astype(BF16), val_b, preferred_element_type=F32)

    key_dec = (key * jnp.exp(b_last - b)).astype(BF16)
    new_state_t = state_t * jnp.exp(b_last) + jnp.dot(val.T.astype(BF16), key_dec,
                                                      preferred_element_type=F32)

    normed = out * lax.rsqrt(jnp.mean(out * out, axis=-1, keepdims=True) + EPS) * gain
    return normed * (hg * jax.nn.sigmoid(hg)), new_state_t


def _hgrn_body(q_ref, f_ref, i_ref, g_ref, lb_ref, gain_ref, o_ref, state_ref, *, layer):
    @pl.when(pl.program_id(1) == 0)
    def _():
        state_ref[...] = jnp.zeros_like(state_ref)

    for hh in range(state_ref.shape[0]):
        sl = slice(hh * HEAD_DIM, (hh + 1) * HEAD_DIM)
        table = lb_ref[:, sl]
        e = jnp.exp(table - jnp.max(table, axis=0, keepdims=True))
        lb = jnp.sum(e[:layer + 1], axis=0, keepdims=True) / jnp.sum(e, axis=0, keepdims=True)
        out, state_ref[hh] = _hgrn_head(q_ref[:, sl], f_ref[:, sl], i_ref[:, sl], g_ref[:, sl],
                                        lb, gain_ref[:, sl], state_ref[hh])
        o_ref[:, sl] = out.astype(o_ref.dtype)


def _hgrn(proj, lb_table, out_gain, n_heads, layer):
    s = proj.shape[0]
    c = _tile(s, HGRN_CHUNK)
    n_slots = lb_table.shape[0]
    group = HGRN_HEAD_GROUP if n_heads % HGRN_HEAD_GROUP == 0 else 1
    n_groups = n_heads // group
    width = group * HEAD_DIM

    def col(block):
        return pl.BlockSpec((c, width), lambda g, t: (t, block * n_groups + g))

    return pl.pallas_call(
        functools.partial(_hgrn_body, layer=layer),
        grid=(n_groups, s // c),
        in_specs=[col(0), col(1), col(2), col(3),
                  pl.BlockSpec((n_slots, width), lambda g, t: (0, g)),
                  pl.BlockSpec((1, width), lambda g, t: (0, g))],
        out_specs=pl.BlockSpec((c, width), lambda g, t: (t, g)),
        out_shape=jax.ShapeDtypeStruct((s, n_heads * HEAD_DIM), BF16),
        scratch_shapes=[pltpu.VMEM((group, HEAD_DIM, HEAD_DIM), F32)],
        compiler_params=_params(("parallel", "arbitrary")),
        name="hgrn2",
    )(proj, proj, proj, proj, lb_table, out_gain)


def _out_proj_body(a_ref, b_ref, wa_ref, wb_ref, x_ref, g_ref, o_ref):
    mix = (jnp.dot(a_ref[...], wa_ref[0].astype(BF16), preferred_element_type=F32)
           + jnp.dot(b_ref[...], wb_ref[0].astype(BF16), preferred_element_type=F32))
    o_ref[...] = x_ref[...] + g_ref[...] * mix


def _out_proj(fox, hgrn, w_out, layer, x, gate):
    s, d = x.shape
    tm, tn = _tile(s, 1024), _tile(d, 512)
    ka, kb = fox.shape[1], hgrn.shape[1]

    def w_rows(first_row, k):
        return pl.BlockSpec((pl.Element(1), pl.Element(k), pl.Element(tn)),
                            lambda i, j: (layer, first_row, pl.multiple_of(j * tn, HEAD_DIM)))

    return pl.pallas_call(
        _out_proj_body,
        grid=(s // tm, d // tn),
        in_specs=[pl.BlockSpec((tm, ka), lambda i, j: (i, 0)),
                  pl.BlockSpec((tm, kb), lambda i, j: (i, 0)),
                  w_rows(0, ka),
                  w_rows(ka, kb),
                  pl.BlockSpec((tm, tn), lambda i, j: (i, j)),
                  pl.BlockSpec((1, tn), lambda i, j: (0, j))],
        out_specs=pl.BlockSpec((tm, tn), lambda i, j: (i, j)),
        out_shape=jax.ShapeDtypeStruct((s, d), F32),
        compiler_params=_params(("parallel", "arbitrary")),
        name="out_proj",
    )(fox, hgrn, w_out, w_out, x, gate)


PEER_NTOP = PEER_TOPK + 1
PEER_TOP_ROWS = 24
PEER_NCAND = 52


def _top_rows(arrays, k, exact_ties, ranked=()):
    vals = list(arrays)
    rows = [[] for _ in vals]
    ranks = {n: jnp.full(vals[n].shape, float(vals[n].shape[0]), F32) for n in ranked}
    for r in range(k):
        for n, v in enumerate(vals):
            m = jnp.max(v, axis=0, keepdims=True)
            rows[n].append(m)
            hit = v == m
            if exact_ties:
                idx = lax.broadcasted_iota(jnp.int32, v.shape, 0)
                first = jnp.min(jnp.where(hit, idx, v.shape[0]), axis=0, keepdims=True)
                hit = idx == first
            vals[n] = jnp.where(hit, -jnp.inf, v)
            if n in ranks:
                ranks[n] = jnp.where(hit, float(r), ranks[n])
    return rows, vals, ranks


def _n_finite(v):
    return jnp.sum(jnp.where(v > -jnp.inf, 1.0, 0.0), axis=0, keepdims=True)


def _peer_thresholds(s0s, s1s, a_ref, b_ref, exact_ties):
    nh = len(s0s)
    rows, left, ranks = _top_rows(s0s + s1s, PEER_NTOP, exact_ties, ranked=range(nh, 2 * nh))
    excess = sum(jnp.abs(_n_finite(v) - (PEER_KEYS - PEER_NTOP)) for v in left)

    r24 = lax.broadcasted_iota(jnp.int32, a_ref.shape[1:], 0)
    r8 = lax.broadcasted_iota(jnp.int32, (8, a_ref.shape[2]), 0)
    ninf = -jnp.inf

    def grp(x, r, lo, hi):
        return jnp.where((r >= lo) & (r <= hi), x, ninf)

    cands = []
    for hd in range(nh):
        a_ref[hd] = jnp.full(a_ref.shape[1:], ninf, F32)
        b_ref[hd] = jnp.full(b_ref.shape[1:], ninf, F32)
        for r in range(PEER_NTOP):
            a_ref[hd, r:r + 1, :] = rows[hd][r]
            b_ref[hd, r:r + 1, :] = rows[nh + hd][r]
        a24, b24 = a_ref[hd], b_ref[hd]
        a8, b8 = a24[:8], b24[:8]
        cands.append(jnp.concatenate([
            grp(a24[0:1] + b24, r24, 0, 16),
            grp(a24 + b24[0:1], r24, 1, 16),
            grp(a8[1:2] + b8, r8, 1, 7),
            grp(a8 + b8[1:2], r8, 2, 7),
            grp(a8[2:3] + b8, r8, 2, 4),
            grp(a8 + b8[2:3], r8, 3, 4),
            grp(a8[3:4] + b8, r8, 3, 3),
        ], axis=0))
    tops, left, _ = _top_rows(cands, PEER_NTOP, exact_ties)
    excess = excess + sum(jnp.abs(_n_finite(v) - (PEER_NCAND - PEER_NTOP)) for v in left)

    outs = []
    for hd in range(nh):
        top = tops[hd]
        tau = 0.5 * (top[PEER_TOPK - 1] + top[PEER_TOPK])
        z = sum(jnp.exp(t - top[0]) for t in top[:PEER_TOPK])
        thr = tau - s0s[hd]
        cnt = sum(jnp.where(b_r >= thr, 1.0, 0.0) for b_r in rows[nh + hd])
        outs.append((cnt,
                     jnp.exp(s0s[hd] - rows[hd][0]) * (0.5 / z),
                     ranks[nh + hd],
                     jnp.exp(s1s[hd] - rows[nh + hd][0])))
    return outs, excess


def _peer_route_body(h_ref, wq_ref, sk_ref, cnt_ref, e0_ref, rank_ref, e1_ref, a_ref, b_ref):
    nh = sk_ref.shape[0]
    q = jnp.dot(h_ref[...], wq_ref[...].astype(BF16), preferred_element_type=F32).astype(BF16)
    s0s, s1s = [], []
    for hd in range(nh):
        c0 = 2 * hd * PEER_KEYS
        s0s.append(lax.dot_general(sk_ref[hd, 0], q[:, c0:c0 + PEER_KEYS], _NT,
                                   preferred_element_type=F32))
        s1s.append(lax.dot_general(sk_ref[hd, 1], q[:, c0 + PEER_KEYS:c0 + 2 * PEER_KEYS], _NT,
                                   preferred_element_type=F32))

    def emit(outs):
        for hd, (cnt, e0, rank1, e1) in enumerate(outs):
            cnt_ref[hd] = cnt
            e0_ref[hd] = e0
            rank_ref[hd] = rank1.astype(rank_ref.dtype)
            e1_ref[hd] = e1.astype(e1_ref.dtype)

    outs, excess = _peer_thresholds(s0s, s1s, a_ref, b_ref, exact_ties=False)
    emit(outs)

    @pl.when(jnp.max(excess) > 0.0)
    def _():
        emit(_peer_thresholds(s0s, s1s, a_ref, b_ref, exact_ties=True)[0])


def _peer_route(h, wq, layer, sub_keys):
    s, d = h.shape
    ph = sub_keys.shape[0]
    tt = _tile(s, 128)
    nh = 4 if ph % 4 == 0 else 1
    out_spec = pl.BlockSpec((nh, PEER_KEYS, tt), lambda p, t: (p, 0, t))
    out_sds = [jax.ShapeDtypeStruct((ph, PEER_KEYS, s), dt) for dt in (F32, F32, BF16, BF16)]
    return pl.pallas_call(
        _peer_route_body,
        grid=(ph // nh, s // tt),
        in_specs=[pl.BlockSpec((tt, d), lambda p, t: (t, 0)),
                  pl.BlockSpec((None, d, nh * 2 * PEER_KEYS), lambda p, t: (layer, 0, p)),
                  pl.BlockSpec((nh, 2, PEER_KEYS, PEER_KEYS), lambda p, t: (p, 0, 0, 0))],
        out_specs=[out_spec] * 4,
        out_shape=out_sds,
        scratch_shapes=[pltpu.VMEM((nh, PEER_TOP_ROWS, tt), F32), pltpu.VMEM((nh, PEER_TOP_ROWS, tt), F32)],
        compiler_params=_params(("parallel", "arbitrary")),
        name="peer_route",
    )(h, wq, sub_keys)


def _bf16_rows(row):
    tile = jnp.broadcast_to(row, (16, row.shape[1])).astype(BF16)
    return jnp.concatenate([tile] * (PEER_KEYS // 16), axis=0)


def _peer_act_body(u_ref, ht_ref, cnt_ref, e0_ref, rank_ref, e1_ref, w_ref, *, rows):
    n_heads = rank_ref.shape[0]
    te = u_ref.shape[0]
    for ch in range(te // rows):
        z = jnp.dot(u_ref[ch * rows:(ch + 1) * rows, :], ht_ref[...], preferred_element_type=F32)
        for ci in range(rows // PEER_KEYS):
            ii = ch * (rows // PEER_KEYS) + ci
            zi = z[ci * PEER_KEYS:(ci + 1) * PEER_KEYS, :]
            act = zi * (1.0 + lax.erf(zi * SQRT_HALF))
            gate = jnp.zeros(zi.shape, BF16)
            for hd in range(n_heads):
                sel = rank_ref[hd] < _bf16_rows(cnt_ref[hd, ii:ii + 1, :])
                gate = gate + (jnp.where(sel, e1_ref[hd], jnp.zeros((), BF16))
                               * _bf16_rows(e0_ref[hd, ii:ii + 1, :]))
            w_ref[:, ii * PEER_KEYS:(ii + 1) * PEER_KEYS] = (act * gate.astype(F32)).T.astype(w_ref.dtype)


def _peer_act(u, h_t, cnt, e0, rank, e1):
    n_exp, d = u.shape
    s = h_t.shape[1]
    ph = cnt.shape[0]
    te = 8 * PEER_KEYS
    tt = _tile(s, 512)
    assert n_exp % te == 0
    per_i = pl.BlockSpec((ph, 8, tt), lambda t, e: (0, e, t))
    per_j = pl.BlockSpec((ph, PEER_KEYS, tt), lambda t, e: (0, 0, t))
    return pl.pallas_call(
        functools.partial(_peer_act_body, rows=te // PEER_ACT_CHUNKS),
        grid=(s // tt, n_exp // te),
        in_specs=[pl.BlockSpec((te, d), lambda t, e: (e, 0)),
                  pl.BlockSpec((d, tt), lambda t, e: (0, t)),
                  per_i, per_i, per_j, per_j],
        out_specs=pl.BlockSpec((tt, te), lambda t, e: (t, e)),
        out_shape=jax.ShapeDtypeStruct((s, n_exp), BF16),
        compiler_params=_params(("parallel", "arbitrary")),
        name="peer_act",
    )(u, h_t, cnt, e0, rank, e1)


def _peer_down_body(w_ref, v_ref, x_ref, g_ref, o_ref, acc_ref):
    k = pl.program_id(2)

    @pl.when(k == 0)
    def _():
        acc_ref[...] = jnp.zeros_like(acc_ref)

    acc_ref[...] += jnp.dot(w_ref[...], v_ref[...], preferred_element_type=F32)

    @pl.when(k == pl.num_programs(2) - 1)
    def _():
        o_ref[...] = x_ref[...] + g_ref[...] * acc_ref[...]


def _peer_down(w, v, x, gate):
    s, n_exp = w.shape
    d = v.shape[1]
    tm, tn, tk = _tile(s, 1024), _tile(d, 1024), _tile(n_exp, 2048)
    return pl.pallas_call(
        _peer_down_body,
        grid=(s // tm, d // tn, n_exp // tk),
        in_specs=[pl.BlockSpec((tm, tk), lambda i, j, k: (i, k)),
                  pl.BlockSpec((tk, tn), lambda i, j, k: (k, j)),
                  pl.BlockSpec((tm, tn), lambda i, j, k: (i, j)),
                  pl.BlockSpec((1, tn), lambda i, j, k: (0, j))],
        out_specs=pl.BlockSpec((tm, tn), lambda i, j, k: (i, j)),
        out_shape=jax.ShapeDtypeStruct((s, d), F32),
        scratch_shapes=[pltpu.VMEM((tm, tn), F32)],
        compiler_params=_params(("parallel", "parallel", "arbitrary")),
        name="peer_down",
    )(w, v, x, gate)


def kernel(x, c, ada_w, ada_b, norm1_gain, norm2_gain, w_in, fox_f_bias, fox_q_gain, fox_k_gain,
           hgrn_lower_bounds, hgrn_out_gain, w_out, peer_w_query, peer_sub_keys, peer_u, peer_v):
    batch, seq, d = x.shape
    assert batch == 1, "one sequence per call"
    depth = ada_w.shape[0]
    fox_heads = fox_f_bias.shape[-1]
    hgrn_heads = hgrn_out_gain.shape[1]
    fox_width, hgrn_width = fox_heads * HEAD_DIM, hgrn_heads * HEAD_DIM
    assert fox_q_gain.shape[-1] == HEAD_DIM and hgrn_out_gain.shape[-1] == HEAD_DIM
    assert w_in.shape[-1] == 3 * fox_width + fox_heads + 4 * hgrn_width
    assert peer_sub_keys.shape[2:] == (2, PEER_KEYS, PEER_KEYS) and peer_u.shape[1] == PEER_KEYS ** 2
    o3 = 3 * fox_width
    o4 = o3 + fox_heads

    xs = x.reshape(seq, d)
    c_col = c.reshape(d, 1)
    w_t = jnp.swapaxes(w_in, 1, 2)
    for layer in range(depth):
        ada_b_row = ada_b[layer].reshape(1, -1)
        mod_early = _ada_mod(c_col, ada_w, layer, ada_b_row, 2 * d)
        shift1, scale1 = mod_early[:, :d], mod_early[:, d:]

        h = _norm_mod(xs, norm1_gain[layer].reshape(1, d), scale1, shift1)
        qk_gains = jnp.concatenate([fox_q_gain[layer].reshape(1, fox_width) * (HEAD_DIM ** -0.5 * LOG2_E),
                                    fox_k_gain[layer].reshape(1, fox_width)], axis=1)
        qk = _fox_proj(h, w_t, layer, qk_gains)
        v_t = _fox_v_proj(h, w_t, layer, 2 * fox_width, fox_width)
        pad = HEAD_DIM - fox_heads
        cum_f = _fox_gate(h, jnp.pad(w_t[layer, o3:o4].T, ((0, 0), (0, pad))).astype(BF16),
                          jnp.pad(fox_f_bias[layer].reshape(1, fox_heads), ((0, 0), (0, pad))), fox_heads)
        fox_out, u_bf16, v_bf16 = _fox_attention(qk, v_t, cum_f, fox_heads, (peer_u, peer_v), layer)

        hproj, mod_late = _hgrn_proj(h, w_t, layer, o4, 4 * hgrn_width, F32, c_col, ada_w, ada_b_row, 2 * d)
        gate1, shift2, scale2, gate2 = (mod_late[:, n * d:(n + 1) * d] for n in range(4))
        hgrn_out = _hgrn(hproj, hgrn_lower_bounds, hgrn_out_gain[layer].reshape(1, hgrn_width),
                         hgrn_heads, layer)

        xs = _out_proj(fox_out, hgrn_out, w_out, layer, xs, gate1)

        h, h_t = _norm_mod(xs, norm2_gain[layer].reshape(1, d), scale2, shift2, with_transposed=True)
        cnt, e0, rank1, e1 = _peer_route(h, peer_w_query, layer, peer_sub_keys[layer].astype(BF16))
        w_act = _peer_act(u_bf16, h_t, cnt, e0, rank1, e1)
        xs = _peer_down(w_act, v_bf16, xs, gate2)
    return xs.reshape(batch, seq, d)
```

```python
import functools

import jax
import jax.numpy as jnp
from jax import lax
from jax.experimental import pallas as pl
from jax.experimental.pallas import tpu as pltpu

F32 = jnp.float32
BF16 = jnp.bfloat16

HEAD_DIM = 128
PEER_KEYS = 128
PEER_TOPK = 16
HGRN_CHUNK = 128
HGRN_HEAD_GROUP = 16
EPS = 1e-6
MASKED = -1e30
SQRT_HALF = 0.7071067811865476
LOG2_E = 1.4426950408889634
ATTN_LANE_GROUP = 256
ATTN_ONES_ROWS = 16
ATTN_UNROLL = 4
PEER_ACT_CHUNKS = 2
V7X_VMEM_LIMIT_BYTES = 52 * 1024 * 1024

_NT = (((1,), (1,)), ((), ()))


def _params(semantics):
    return pltpu.CompilerParams(dimension_semantics=semantics, vmem_limit_bytes=V7X_VMEM_LIMIT_BYTES)


def _tile(n, t):
    if n <= t:
        return n
    t -= t % HEAD_DIM
    while n % t:
        t -= HEAD_DIM
    assert t > 0, n
    return t


def _split3(x):
    hi = x.astype(BF16)
    r1 = x - hi.astype(F32)
    mid = r1.astype(BF16)
    lo = (r1 - mid.astype(F32)).astype(BF16)
    return hi, mid, lo


def _ada_body(c_ref, w_ref, b_ref, o_ref, acc_ref):
    k = pl.program_id(1)

    @pl.when(k == 0)
    def _():
        acc_ref[...] = jnp.zeros_like(acc_ref)

    c = c_ref[...]
    c_act = c * jax.nn.sigmoid(c)
    acc_ref[...] += jnp.sum(w_ref[...] * c_act, axis=0, keepdims=True)

    @pl.when(k == pl.num_programs(1) - 1)
    def _():
        o_ref[...] = acc_ref[...] + b_ref[...]


def _ada_mod(c_col, w, layer, b_row, n):
    d = w.shape[1]
    tk, tn = _tile(d, 512), _tile(n, 2048)
    return pl.pallas_call(
        _ada_body,
        grid=(n // tn, d // tk),
        in_specs=[pl.BlockSpec((tk, 1), lambda j, k: (k, 0)),
                  pl.BlockSpec((None, tk, tn), lambda j, k: (layer, k, j)),
                  pl.BlockSpec((1, tn), lambda j, k: (0, j))],
        out_specs=pl.BlockSpec((1, tn), lambda j, k: (0, j)),
        out_shape=jax.ShapeDtypeStruct((1, n), F32),
        scratch_shapes=[pltpu.VMEM((1, tn), F32)],
        compiler_params=_params(("parallel", "arbitrary")),
        name="ada_mod",
    )(c_col, w, b_row)


def _norm_mod_body(x_ref, g_ref, sc_ref, sh_ref, o_ref, *maybe_ot_ref):
    x = x_ref[...]
    y = x * lax.rsqrt(jnp.mean(x * x, axis=-1, keepdims=True) + EPS)
    h = y * g_ref[...] * (1.0 + sc_ref[...]) + sh_ref[...]
    o_ref[...] = h.astype(o_ref.dtype)
    for ot_ref in maybe_ot_ref:
        ot_ref[...] = h.T.astype(ot_ref.dtype)


def _norm_mod(x, gain, scale, shift, with_transposed=False):
    s, d = x.shape
    tm = _tile(s, 256)
    row = pl.BlockSpec((1, d), lambda i: (0, 0))
    out_specs = [pl.BlockSpec((tm, d), lambda i: (i, 0))]
    out_shape = [jax.ShapeDtypeStruct((s, d), BF16)]
    if with_transposed:
        out_specs.append(pl.BlockSpec((d, tm), lambda i: (0, i)))
        out_shape.append(jax.ShapeDtypeStruct((d, s), BF16))
    outs = pl.pallas_call(
        _norm_mod_body,
        grid=(s // tm,),
        in_specs=[pl.BlockSpec((tm, d), lambda i: (i, 0)), row, row, row],
        out_specs=out_specs,
        out_shape=out_shape,
        compiler_params=_params(("parallel",)),
        name="norm_mod",
    )(x, gain, scale, shift)
    return outs if with_transposed else outs[0]


def _fox_proj_body(h_ref, w_ref, g_ref, o_ref):
    y = lax.dot_general(h_ref[...], w_ref[0].astype(BF16), _NT, preferred_element_type=F32)
    g = g_ref[...]
    for hh in range(y.shape[1] // HEAD_DIM):
        sl = slice(hh * HEAD_DIM, (hh + 1) * HEAD_DIM)
        yh = y[:, sl]
        r = lax.rsqrt(jnp.mean(yh * yh, axis=-1, keepdims=True) + EPS)
        o_ref[:, sl] = (yh * r * g[:, sl]).astype(o_ref.dtype)


def _w_rows_spec(layer, first_row, tn, d):
    assert first_row % 8 == 0 and tn % 8 == 0
    return pl.BlockSpec((pl.Element(1), pl.Element(tn), pl.Element(d)),
                        lambda i, j: (layer, pl.multiple_of(first_row + j * tn, 8), 0))


def _fox_proj(h, w_t, layer, gains):
    s, d = h.shape
    n = gains.shape[1]
    tm, tn = _tile(s, 1024), _tile(n, 512)
    return pl.pallas_call(
        _fox_proj_body,
        grid=(s // tm, n // tn),
        in_specs=[pl.BlockSpec((tm, d), lambda i, j: (i, 0)),
                  _w_rows_spec(layer, 0, tn, d),
                  pl.BlockSpec((1, tn), lambda i, j: (0, j))],
        out_specs=pl.BlockSpec((tm, tn), lambda i, j: (i, j)),
        out_shape=jax.ShapeDtypeStruct((s, n), BF16),
        compiler_params=_params(("parallel", "arbitrary")),
        name="fox_proj",
    )(h, w_t, gains)


def _fox_v_proj_body(h_ref, w_ref, o_ref):
    o_ref[...] = lax.dot_general(w_ref[0].astype(BF16), h_ref[...], _NT,
                                 preferred_element_type=F32).astype(o_ref.dtype)


def _fox_v_proj(h, w_t, layer, first_row, n):
    s, d = h.shape
    tm, tn = _tile(s, 1024), _tile(n, 512)
    return pl.pallas_call(
        _fox_v_proj_body,
        grid=(s // tm, n // tn),
        in_specs=[pl.BlockSpec((tm, d), lambda i, j: (i, 0)),
                  _w_rows_spec(layer, first_row, tn, d)],
        out_specs=pl.BlockSpec((tn, tm), lambda i, j: (j, i)),
        out_shape=jax.ShapeDtypeStruct((n, s), BF16),
        compiler_params=_params(("parallel", "arbitrary")),
        name="fox_v_proj",
    )(h, w_t)


def _hgrn_proj_body(h_ref, w_ref, c_ref, aw_ref, ab_ref, o_ref, mod_ref):
    o_ref[...] = lax.dot_general(h_ref[...], w_ref[0].astype(BF16), _NT,
                                 preferred_element_type=F32).astype(o_ref.dtype)
    c = c_ref[...]
    mod_ref[...] = jnp.sum(aw_ref[...] * (c * jax.nn.sigmoid(c)), axis=0, keepdims=True) + ab_ref[...]


def _hgrn_proj(h, w_t, layer, first_row, n, out_dtype, c_col, ada_w, ada_b_row, mod_first_col):
    s, d = h.shape
    tm, tn = _tile(s, 1024), _tile(n, 512)
    n_i, n_j = s // tm, n // tn
    n_mod = ada_w.shape[2] - mod_first_col
    cps = n_mod // (n_i * n_j)
    assert cps * n_i * n_j == n_mod and cps % HEAD_DIM == 0 and mod_first_col % cps == 0
    first_blk = mod_first_col // cps
    return pl.pallas_call(
        _hgrn_proj_body,
        grid=(n_i, n_j),
        in_specs=[pl.BlockSpec((tm, d), lambda i, j: (i, 0)),
                  _w_rows_spec(layer, first_row, tn, d),
                  pl.BlockSpec((d, 1), lambda i, j: (0, 0)),
                  pl.BlockSpec((None, d, cps), lambda i, j: (layer, 0, first_blk + i * n_j + j)),
                  pl.BlockSpec((1, cps), lambda i, j: (0, first_blk + i * n_j + j))],
        out_specs=[pl.BlockSpec((tm, tn), lambda i, j: (i, j)),
                   pl.BlockSpec((1, cps), lambda i, j: (0, i * n_j + j))],
        out_shape=[jax.ShapeDtypeStruct((s, n), out_dtype),
                   jax.ShapeDtypeStruct((1, n_mod), F32)],
        compiler_params=_params(("parallel", "arbitrary")),
        name="hgrn_proj",
    )(h, w_t, c_col, ada_w, ada_b_row)


def _fox_gate_body(h_ref, w_ref, b_ref, o_ref, carry_ref):
    i = pl.program_id(0)

    @pl.when(i == 0)
    def _():
        carry_ref[...] = jnp.zeros_like(carry_ref)

    logit = jnp.dot(h_ref[...], w_ref[...], preferred_element_type=F32) + b_ref[...]
    log_f = jnp.minimum(logit, 0.0) - jnp.log1p(jnp.exp(-jnp.abs(logit)))
    ts = log_f.shape[0]
    lower = (lax.broadcasted_iota(jnp.int32, (ts, ts), 1)
             <= lax.broadcasted_iota(jnp.int32, (ts, ts), 0)).astype(BF16)
    csum = sum(jnp.dot(lower, part, preferred_element_type=F32) for part in _split3(log_f))
    csum = csum + carry_ref[...]
    carry_ref[...] = csum[ts - 1:ts, :]
    scaled = csum * LOG2_E
    for hd in range(o_ref.shape[0]):
        o_ref[hd] = jnp.broadcast_to(scaled[:, hd:hd + 1], (ts, HEAD_DIM))


def _fox_gate(h, w_pad, bias_pad, n_heads):
    s, d = h.shape
    ts = _tile(s, 512)
    return pl.pallas_call(
        _fox_gate_body,
        grid=(s // ts,),
        in_specs=[pl.BlockSpec((ts, d), lambda i: (i, 0)),
                  pl.BlockSpec((d, HEAD_DIM), lambda i: (0, 0)),
                  pl.BlockSpec((1, HEAD_DIM), lambda i: (0, 0))],
        out_specs=pl.BlockSpec((n_heads, ts, HEAD_DIM), lambda i: (0, i, 0)),
        out_shape=jax.ShapeDtypeStruct((n_heads, s, HEAD_DIM), F32),
        scratch_shapes=[pltpu.VMEM((1, HEAD_DIM), F32)],
        compiler_params=_params(("arbitrary",)),
        name="fox_gate",
    )(h, w_pad, bias_pad)


def _fox_attn_body(q_ref, k_ref, vt_ref, cb_ref, *refs, t, n_tables):
    table_refs, o_ref = refs[:n_tables], refs[n_tables]
    table_out_refs = refs[n_tables + 1:2 * n_tables + 1]
    sa_ref, sb_ref, m_ref, acc_ref = refs[2 * n_tables + 1:]
    for src_ref, dst_ref in zip(table_refs, table_out_refs):
        dst_ref[...] = src_ref[...].astype(dst_ref.dtype)

    i = pl.program_id(1)
    q = q_ref[...]
    q_start = pl.multiple_of(i * t, t)
    c_first = cb_ref[pl.ds(q_start, 8), :][0:1]
    m_ref[...] = jnp.full_like(m_ref, MASKED)
    acc_ref[...] = jnp.zeros_like(acc_ref)
    group = min(t, ATTN_LANE_GROUP)
    ones_rows = jnp.ones((ATTN_ONES_ROWS, t), BF16)

    def scores(j):
        k_start = pl.multiple_of(j * t, t)
        return lax.dot_general(k_ref[pl.ds(k_start, t), :], q, _NT, preferred_element_type=F32)

    def consume(j, sc_ref, masked):
        k_start = pl.multiple_of(j * t, t)
        bias = c_first - cb_ref[pl.ds(k_start, t), :]
        bias = jnp.concatenate([bias] * (group // HEAD_DIM), axis=1)
        v_t = jnp.concatenate([vt_ref[:, pl.ds(k_start, t)], ones_rows], axis=0)
        for g in range(t // group):
            lanes = slice(g * group, (g + 1) * group)
            n_keys = (g + 1) * group if masked else t
            s = sc_ref[:n_keys, lanes] + bias[:n_keys]
            if masked:
                k_pos = lax.broadcasted_iota(jnp.int32, (n_keys, group), 0)
                q_pos = lax.broadcasted_iota(jnp.int32, (n_keys, group), 1) + g * group
                s = jnp.where(k_pos <= q_pos, s, MASKED)
            m_old = m_ref[:, lanes]
            m_new = jnp.maximum(m_old, jnp.max(s, axis=0, keepdims=True))
            alpha = jnp.exp2(m_old - m_new)
            p = jnp.exp2(s - m_new)
            acc_ref[:, lanes] = alpha * acc_ref[:, lanes] + jnp.dot(
                v_t[:, :n_keys], p.astype(BF16), preferred_element_type=F32)
            m_ref[:, lanes] = m_new

    bufs = (sa_ref, sb_ref)
    sa_ref[...] = scores(0)

    def run(j0, n_blocks, last_is_diagonal):
        for b in range(n_blocks):
            is_last = b == n_blocks - 1
            if not (is_last and last_is_diagonal):
                bufs[(b + 1) % 2][...] = scores(j0 + b + 1)
            consume(j0 + b, bufs[b % 2], masked=is_last and last_is_diagonal)

    def unrolled(jj, carry):
        run(jj * ATTN_UNROLL, ATTN_UNROLL, last_is_diagonal=False)
        return carry

    n_unrolled = i // ATTN_UNROLL
    lax.fori_loop(0, n_unrolled, unrolled, 0)
    for rem in range(ATTN_UNROLL):
        @pl.when(i - n_unrolled * ATTN_UNROLL == rem)
        def _(rem=rem):
            run(n_unrolled * ATTN_UNROLL, rem + 1, last_is_diagonal=True)

    o_ref[...] = (acc_ref[:HEAD_DIM, :] / acc_ref[HEAD_DIM:HEAD_DIM + 1, :]).T.astype(o_ref.dtype)


def _fox_attention(qk, v_t, cum_f, n_heads, tables, layer):
    s = qk.shape[0]
    t = _tile(s, 512)
    n_q = s // t
    steps = n_heads * n_q
    table_in, table_out, table_shape = [], [], []
    for tab in tables:
        rows, d = tab.shape[1:]
        assert rows % steps == 0 and (rows // steps) % 16 == 0, (rows, steps)
        table_in.append(pl.BlockSpec((None, rows // steps, d), lambda h, i: (layer, h * n_q + i, 0)))
        table_out.append(pl.BlockSpec((rows // steps, d), lambda h, i: (h * n_q + i, 0)))
        table_shape.append(jax.ShapeDtypeStruct((rows, d), BF16))
    return pl.pallas_call(
        functools.partial(_fox_attn_body, t=t, n_tables=len(tables)),
        grid=(n_heads, n_q),
        in_specs=[pl.BlockSpec((t, HEAD_DIM), lambda h, i: (i, h)),
                  pl.BlockSpec((s, HEAD_DIM), lambda h, i: (0, n_heads + h)),
                  pl.BlockSpec((HEAD_DIM, s), lambda h, i: (h, 0)),
                  pl.BlockSpec((None, s, HEAD_DIM), lambda h, i: (h, 0, 0))] + table_in,
        out_specs=[pl.BlockSpec((t, HEAD_DIM), lambda h, i: (i, h))] + table_out,
        out_shape=[jax.ShapeDtypeStruct((s, n_heads * HEAD_DIM), BF16)] + table_shape,
        scratch_shapes=[pltpu.VMEM((t, t), F32), pltpu.VMEM((t, t), F32),
                        pltpu.VMEM((1, t), F32),
                        pltpu.VMEM((HEAD_DIM + ATTN_ONES_ROWS, t), F32)],
        compiler_params=_params(("parallel", "arbitrary")),
        name="fox_attention",
    )(qk, qk, v_t, cum_f, *tables)


def _hgrn_head(hq, hf, val, hg, lb, gain, state_t):
    forget = lb + (1.0 - lb) * jax.nn.sigmoid(hf)
    log_f = jnp.log(forget)
    key = 1.0 - forget
    query = hq * jax.nn.sigmoid(hq) * (HEAD_DIM ** -0.5)
    val_b = val.astype(BF16)
    c = log_f.shape[0]

    row = lax.broadcasted_iota(jnp.int32, (c, c), 0)
    col = lax.broadcasted_iota(jnp.int32, (c, c), 1)
    lower = (col <= row).astype(BF16)
    b = sum(jnp.dot(lower, part, preferred_element_type=F32) for part in _split3(log_f))
    b_last = b[c - 1:c, :]

    out = lax.dot_general((query * jnp.exp(b)).astype(BF16), state_t.astype(BF16), _NT,
                          preferred_element_type=F32)

    t_idx = lax.broadcasted_iota(jnp.int32, (c, HEAD_DIM), 0)
    sep = row ^ col
    scores = jnp.where(sep == 0, lax.dot_general(query.astype(BF16), key.astype(BF16), _NT,
                                                 preferred_element_type=F32), 0.0)
    half = c // 2
    while half >= 1:
        blk = 2 * half
        if half >= 4:
            ref_rows = jnp.broadcast_to(b.reshape(c // blk, blk, HEAD_DIM)[:, half - 1:half, :],
                                        (c // blk, blk, HEAD_DIM)).reshape(c, HEAD_DIM)
        elif half == 2:
            pos = t_idx & 3
            ref_rows = jnp.where(pos == 0, pltpu.roll(b, c - 1, 0),
                                 jnp.where(pos == 2, pltpu.roll(b, 1, 0),
                                           jnp.where(pos == 3, pltpu.roll(b, 2, 0), b)))
        else:
            ref_rows = jnp.where((t_idx & 1) == 1, pltpu.roll(b, 1, 0), b)
        in_upper = (t_idx & half) != 0
        x = jnp.where(in_upper, query, key) * jnp.exp(-jnp.abs(b - ref_rows))
        a_side = jnp.where(in_upper, x, 0.0).astype(BF16)
        b_side = jnp.where(in_upper, 0.0, x).astype(BF16)
        level = lax.dot_general(a_side, b_side, _NT, preferred_element_type=F32)
        scores = scores + (level if blk == c else jnp.where(sep < blk, level, 0.0))
        half //= 2
    out = out + jnp.dot(scores.astype(BF16), val_b, preferred_element_type=F32)

    key_dec = (key * jnp.exp(b_last - b)).astype(BF16)
    new_state_t = state_t * jnp.exp(b_last) + jnp.dot(val.T.astype(BF16), key_dec,
                                                      preferred_element_type=F32)

    normed = out * lax.rsqrt(jnp.mean(out * out, axis=-1, keepdims=True) + EPS) * gain
    return normed * (hg * jax.nn.sigmoid(hg)), new_state_t


def _hgrn_body(q_ref, f_ref, i_ref, g_ref, lb_ref, gain_ref, o_ref, state_ref, *, layer):
    @pl.when(pl.program_id(1) == 0)
    def _():
        state_ref[...] = jnp.zeros_like(state_ref)

    for hh in range(state_ref.shape[0]):
        sl = slice(hh * HEAD_DIM, (hh + 1) * HEAD_DIM)
        table = lb_ref[:, sl]
        e = jnp.exp(table - jnp.max(table, axis=0, keepdims=True))
        lb = jnp.sum(e[:layer + 1], axis=0, keepdims=True) / jnp.sum(e, axis=0, keepdims=True)
        out, state_ref[hh] = _hgrn_head(q_ref[:, sl], f_ref[:, sl], i_ref[:, sl], g_ref[:, sl],
                                        lb, gain_ref[:, sl], state_ref[hh])
        o_ref[:, sl] = out.astype(o_ref.dtype)


def _hgrn(proj, lb_table, out_gain, n_heads, layer):
    s = proj.shape[0]
    c = _tile(s, HGRN_CHUNK)
    n_slots = lb_table.shape[0]
    group = HGRN_HEAD_GROUP if n_heads % HGRN_HEAD_GROUP == 0 else 1
    n_groups = n_heads // group
    width = group * HEAD_DIM

    def col(block):
        return pl.BlockSpec((c, width), lambda g, t: (t, block * n_groups + g))

    return pl.pallas_call(
        functools.partial(_hgrn_body, layer=layer),
        grid=(n_groups, s // c),
        in_specs=[col(0), col(1), col(2), col(3),
                  pl.BlockSpec((n_slots, width), lambda g, t: (0, g)),
                  pl.BlockSpec((1, width), lambda g, t: (0, g))],
        out_specs=pl.BlockSpec((c, width), lambda g, t: (t, g)),
        out_shape=jax.ShapeDtypeStruct((s, n_heads * HEAD_DIM), BF16),
        scratch_shapes=[pltpu.VMEM((group, HEAD_DIM, HEAD_DIM), F32)],
        compiler_params=_params(("parallel", "arbitrary")),
        name="hgrn2",
    )(proj, proj, proj, proj, lb_table, out_gain)


def _out_proj_body(a_ref, b_ref, wa_ref, wb_ref, x_ref, g_ref, o_ref):
    mix = (jnp.dot(a_ref[...], wa_ref[0].astype(BF16), preferred_element_type=F32)
           + jnp.dot(b_ref[...], wb_ref[0].astype(BF16), preferred_element_type=F32))
    o_ref[...] = x_ref[...] + g_ref[...] * mix


def _out_proj(fox, hgrn, w_out, layer, x, gate):
    s, d = x.shape
    tm, tn = _tile(s, 1024), _tile(d, 512)
    ka, kb = fox.shape[1], hgrn.shape[1]

    def w_rows(first_row, k):
        return pl.BlockSpec((pl.Element(1), pl.Element(k), pl.Element(tn)),
                            lambda i, j: (layer, first_row, pl.multiple_of(j * tn, HEAD_DIM)))

    return pl.pallas_call(
        _out_proj_body,
        grid=(s // tm, d // tn),
        in_specs=[pl.BlockSpec((tm, ka), lambda i, j: (i, 0)),
                  pl.BlockSpec((tm, kb), lambda i, j: (i, 0)),
                  w_rows(0, ka),
                  w_rows(ka, kb),
                  pl.BlockSpec((tm, tn), lambda i, j: (i, j)),
                  pl.BlockSpec((1, tn), lambda i, j: (0, j))],
        out_specs=pl.BlockSpec((tm, tn), lambda i, j: (i, j)),
        out_shape=jax.ShapeDtypeStruct((s, d), F32),
        compiler_params=_params(("parallel", "arbitrary")),
        name="out_proj",
    )(fox, hgrn, w_out, w_out, x, gate)


PEER_NTOP = PEER_TOPK + 1
PEER_TOP_ROWS = 24
PEER_NCAND = 52


def _top_rows(arrays, k, exact_ties, ranked=()):
    vals = list(arrays)
    rows = [[] for _ in vals]
    ranks = {n: jnp.full(vals[n].shape, float(vals[n].shape[0]), F32) for n in ranked}
    for r in range(k):
        for n, v in enumerate(vals):
            m = jnp.max(v, axis=0, keepdims=True)
            rows[n].append(m)
            hit = v == m
            if exact_ties:
                idx = lax.broadcasted_iota(jnp.int32, v.shape, 0)
                first = jnp.min(jnp.where(hit, idx, v.shape[0]), axis=0, keepdims=True)
                hit = idx == first
            vals[n] = jnp.where(hit, -jnp.inf, v)
            if n in ranks:
                ranks[n] = jnp.where(hit, float(r), ranks[n])
    return rows, vals, ranks


def _n_finite(v):
    return jnp.sum(jnp.where(v > -jnp.inf, 1.0, 0.0), axis=0, keepdims=True)


def _peer_thresholds(s0s, s1s, a_ref, b_ref, exact_ties):
    nh = len(s0s)
    rows, left, ranks = _top_rows(s0s + s1s, PEER_NTOP, exact_ties, ranked=range(nh, 2 * nh))
    excess = sum(jnp.abs(_n_finite(v) - (PEER_KEYS - PEER_NTOP)) for v in left)

    r24 = lax.broadcasted_iota(jnp.int32, a_ref.shape[1:], 0)
    r8 = lax.broadcasted_iota(jnp.int32, (8, a_ref.shape[2]), 0)
    ninf = -jnp.inf

    def grp(x, r, lo, hi):
        return jnp.where((r >= lo) & (r <= hi), x, ninf)

    cands = []
    for hd in range(nh):
        a_ref[hd] = jnp.full(a_ref.shape[1:], ninf, F32)
        b_ref[hd] = jnp.full(b_ref.shape[1:], ninf, F32)
        for r in range(PEER_NTOP):
            a_ref[hd, r:r + 1, :] = rows[hd][r]
            b_ref[hd, r:r + 1, :] = rows[nh + hd][r]
        a24, b24 = a_ref[hd], b_ref[hd]
        a8, b8 = a24[:8], b24[:8]
        cands.append(jnp.concatenate([
            grp(a24[0:1] + b24, r24, 0, 16),
            grp(a24 + b24[0:1], r24, 1, 16),
            grp(a8[1:2] + b8, r8, 1, 7),
            grp(a8 + b8[1:2], r8, 2, 7),
            grp(a8[2:3] + b8, r8, 2, 4),
            grp(a8 + b8[2:3], r8, 3, 4),
            grp(a8[3:4] + b8, r8, 3, 3),
        ], axis=0))
    tops, left, _ = _top_rows(cands, PEER_NTOP, exact_ties)
    excess = excess + sum(jnp.abs(_n_finite(v) - (PEER_NCAND - PEER_NTOP)) for v in left)

    outs = []
    for hd in range(nh):
        top = tops[hd]
        tau = 0.5 * (top[PEER_TOPK - 1] + top[PEER_TOPK])
        z = sum(jnp.exp(t - top[0]) for t in top[:PEER_TOPK])
        thr = tau - s0s[hd]
        cnt = sum(jnp.where(b_r >= thr, 1.0, 0.0) for b_r in rows[nh + hd])
        outs.append((cnt,
                     jnp.exp(s0s[hd] - rows[hd][0]) * (0.5 / z),
                     ranks[nh + hd],
                     jnp.exp(s1s[hd] - rows[nh + hd][0])))
    return outs, excess


def _peer_route_body(h_ref, wq_ref, sk_ref, cnt_ref, e0_ref, rank_ref, e1_ref, a_ref, b_ref):
    nh = sk_ref.shape[0]
    q = jnp.dot(h_ref[...], wq_ref[...].astype(BF16), preferred_element_type=F32).astype(BF16)
    s0s, s1s = [], []
    for hd in range(nh):
        c0 = 2 * hd * PEER_KEYS
        s0s.append(lax.dot_general(sk_ref[hd, 0], q[:, c0:c0 + PEER_KEYS], _NT,
                                   preferred_element_type=F32))
        s1s.append(lax.dot_general(sk_ref[hd, 1], q[:, c0 + PEER_KEYS:c0 + 2 * PEER_KEYS], _NT,
                                   preferred_element_type=F32))

    def emit(outs):
        for hd, (cnt, e0, rank1, e1) in enumerate(outs):
            cnt_ref[hd] = cnt
            e0_ref[hd] = e0
            rank_ref[hd] = rank1.astype(rank_ref.dtype)
            e1_ref[hd] = e1.astype(e1_ref.dtype)

    outs, excess = _peer_thresholds(s0s, s1s, a_ref, b_ref, exact_ties=False)
    emit(outs)

    @pl.when(jnp.max(excess) > 0.0)
    def _():
        emit(_peer_thresholds(s0s, s1s, a_ref, b_ref, exact_ties=True)[0])


def _peer_route(h, wq, layer, sub_keys):
    s, d = h.shape
    ph = sub_keys.shape[0]
    tt = _tile(s, 128)
    nh = 4 if ph % 4 == 0 else 1
    out_spec = pl.BlockSpec((nh, PEER_KEYS, tt), lambda p, t: (p, 0, t))
    out_sds = [jax.ShapeDtypeStruct((ph, PEER_KEYS, s), dt) for dt in (F32, F32, BF16, BF16)]
    return pl.pallas_call(
        _peer_route_body,
        grid=(ph // nh, s // tt),
        in_specs=[pl.BlockSpec((tt, d), lambda p, t: (t, 0)),
                  pl.BlockSpec((None, d, nh * 2 * PEER_KEYS), lambda p, t: (layer, 0, p)),
                  pl.BlockSpec((nh, 2, PEER_KEYS, PEER_KEYS), lambda p, t: (p, 0, 0, 0))],
        out_specs=[out_spec] * 4,
        out_shape=out_sds,
        scratch_shapes=[pltpu.VMEM((nh, PEER_TOP_ROWS, tt), F32), pltpu.VMEM((nh, PEER_TOP_ROWS, tt), F32)],
        compiler_params=_params(("parallel", "arbitrary")),
        name="peer_route",
    )(h, wq, sub_keys)


def _bf16_rows(row):
    tile = jnp.broadcast_to(row, (16, row.shape[1])).astype(BF16)
    return jnp.concatenate([tile] * (PEER_KEYS // 16), axis=0)


def _peer_act_body(u_ref, ht_ref, cnt_ref, e0_ref, rank_ref, e1_ref, w_ref, *, rows):
    n_heads = rank_ref.shape[0]
    te = u_ref.shape[0]
    for ch in range(te // rows):
        z = jnp.dot(u_ref[ch * rows:(ch + 1) * rows, :], ht_ref[...], preferred_element_type=F32)
        for ci in range(rows // PEER_KEYS):
            ii = ch * (rows // PEER_KEYS) + ci
            zi = z[ci * PEER_KEYS:(ci + 1) * PEER_KEYS, :]
            act = zi * (1.0 + lax.erf(zi * SQRT_HALF))
            gate = jnp.zeros(zi.shape, BF16)
            for hd in range(n_heads):
                sel = rank_ref[hd] < _bf16_rows(cnt_ref[hd, ii:ii + 1, :])
                gate = gate + (jnp.where(sel, e1_ref[hd], jnp.zeros((), BF16))
                               * _bf16_rows(e0_ref[hd, ii:ii + 1, :]))
            w_ref[:, ii * PEER_KEYS:(ii + 1) * PEER_KEYS] = (act * gate.astype(F32)).T.astype(w_ref.dtype)


def _peer_act(u, h_t, cnt, e0, rank, e1):
    n_exp, d = u.shape
    s = h_t.shape[1]
    ph = cnt.shape[0]
    te = 8 * PEER_KEYS
    tt = _tile(s, 512)
    assert n_exp % te == 0
    per_i = pl.BlockSpec((ph, 8, tt), lambda e, t: (0, e, t))
    per_j = pl.BlockSpec((ph, PEER_KEYS, tt), lambda e, t: (0, 0, t))
    return pl.pallas_call(
        functools.partial(_peer_act_body, rows=te // PEER_ACT_CHUNKS),
        grid=(n_exp // te, s // tt),
        in_specs=[pl.BlockSpec((te, d), lambda e, t: (e, 0)),
                  pl.BlockSpec((d, tt), lambda e, t: (0, t)),
                  per_i, per_i, per_j, per_j],
        out_specs=pl.BlockSpec((tt, te), lambda e, t: (t, e)),
        out_shape=jax.ShapeDtypeStruct((s, n_exp), BF16),
        compiler_params=_params(("parallel", "arbitrary")),
        name="peer_act",
    )(u, h_t, cnt, e0, rank, e1)


def _peer_down_body(w_ref, v_ref, x_ref, g_ref, o_ref, acc_ref):
    k = pl.program_id(2)

    @pl.when(k == 0)
    def _():
        acc_ref[...] = jnp.zeros_like(acc_ref)

    acc_ref[...] += jnp.dot(w_ref[...], v_ref[...], preferred_element_type=F32)

    @pl.when(k == pl.num_programs(2) - 1)
    def _():
        o_ref[...] = x_ref[...] + g_ref[...] * acc_ref[...]


def _peer_down(w, v, x, gate):
    s, n_exp = w.shape
    d = v.shape[1]
    tm, tn, tk = _tile(s, 1024), _tile(d, 1024), _tile(n_exp, 2048)
    return pl.pallas_call(
        _peer_down_body,
        grid=(s // tm, d // tn, n_exp // tk),
        in_specs=[pl.BlockSpec((tm, tk), lambda i, j, k: (i, k)),
                  pl.BlockSpec((tk, tn), lambda i, j, k: (k, j)),
                  pl.BlockSpec((tm, tn), lambda i, j, k: (i, j)),
                  pl.BlockSpec((1, tn), lambda i, j, k: (0, j))],
        out_specs=pl.BlockSpec((tm, tn), lambda i, j, k: (i, j)),
        out_shape=jax.ShapeDtypeStruct((s, d), F32),
        scratch_shapes=[pltpu.VMEM((tm, tn), F32)],
        compiler_params=_params(("parallel", "parallel", "arbitrary")),
        name="peer_down",
    )(w, v, x, gate)


def kernel(x, c, ada_w, ada_b, norm1_gain, norm2_gain, w_in, fox_f_bias, fox_q_gain, fox_k_gain,
           hgrn_lower_bounds, hgrn_out_gain, w_out, peer_w_query, peer_sub_keys, peer_u, peer_v):
    batch, seq, d = x.shape
    assert batch == 1, "one sequence per call"
    depth = ada_w.shape[0]
    fox_heads = fox_f_bias.shape[-1]
    hgrn_heads = hgrn_out_gain.shape[1]
    fox_width, hgrn_width = fox_heads * HEAD_DIM, hgrn_heads * HEAD_DIM
    assert fox_q_gain.shape[-1] == HEAD_DIM and hgrn_out_gain.shape[-1] == HEAD_DIM
    assert w_in.shape[-1] == 3 * fox_width + fox_heads + 4 * hgrn_width
    assert peer_sub_keys.shape[2:] == (2, PEER_KEYS, PEER_KEYS) and peer_u.shape[1] == PEER_KEYS ** 2
    o3 = 3 * fox_width
    o4 = o3 + fox_heads

    xs = x.reshape(seq, d)
    c_col = c.reshape(d, 1)
    w_t = jnp.swapaxes(w_in, 1, 2)
    for layer in range(depth):
        ada_b_row = ada_b[layer].reshape(1, -1)
        mod_early = _ada_mod(c_col, ada_w, layer, ada_b_row, 2 * d)
        shift1, scale1 = mod_early[:, :d], mod_early[:, d:]

        h = _norm_mod(xs, norm1_gain[layer].reshape(1, d), scale1, shift1)
        qk_gains = jnp.concatenate([fox_q_gain[layer].reshape(1, fox_width) * (HEAD_DIM ** -0.5 * LOG2_E),
                                    fox_k_gain[layer].reshape(1, fox_width)], axis=1)
        qk = _fox_proj(h, w_t, layer, qk_gains)
        v_t = _fox_v_proj(h, w_t, layer, 2 * fox_width, fox_width)
        pad = HEAD_DIM - fox_heads
        cum_f = _fox_gate(h, jnp.pad(w_t[layer, o3:o4].T, ((0, 0), (0, pad))).astype(BF16),
                          jnp.pad(fox_f_bias[layer].reshape(1, fox_heads), ((0, 0), (0, pad))), fox_heads)
        fox_out, u_bf16, v_bf16 = _fox_attention(qk, v_t, cum_f, fox_heads, (peer_u, peer_v), layer)

        hproj, mod_late = _hgrn_proj(h, w_t, layer, o4, 4 * hgrn_width, F32, c_col, ada_w, ada_b_row, 2 * d)
        gate1, shift2, scale2, gate2 = (mod_late[:, n * d:(n + 1) * d] for n in range(4))
        hgrn_out = _hgrn(hproj, hgrn_lower_bounds, hgrn_out_gain[layer].reshape(1, hgrn_width),
                         hgrn_heads, layer)

        xs = _out_proj(fox_out, hgrn_out, w_out, layer, xs, gate1)

        h, h_t = _norm_mod(xs, norm2_gain[layer].reshape(1, d), scale2, shift2, with_transposed=True)
        cnt, e0, rank1, e1 = _peer_route(h, peer_w_query, layer, peer_sub_keys[layer].astype(BF16))
        w_act = _peer_act(u_bf16, h_t, cnt, e0, rank1, e1)
        xs = _peer_down(w_act, v_bf16, xs, gate2)
    return xs.reshape(batch, seq, d)
```
